```python
import math
import jax
import jax.numpy as jnp
from jax import lax
import numpy as np

D_MODEL = 1024
BATCH = 32
SEQ = 256
DEPTH = 4
DEC_BATCH = 8
DEC_SEQ = 4096
PAST_LEN = 256

GRID_W = 64
HEAD_DIM = 64
GQA_Q_HEADS = 4
GQA_KV_HEADS = 2
GQA_GROUP = GQA_Q_HEADS // GQA_KV_HEADS
NA_HEADS = 4
NA_KH = 8
NA_KW = 16
DIFF_HEADS = 4
DIFF_QK_DIM = 32
DIFF_V_DIM = 64
HG_HEADS = 4
HG_DK = 64
HG_DV = 64
BRANCH_W = 256
N_BRANCH = 4
N_GROUPS = 4
EXPERTS_PER_GROUP = 4
TOP_K_IN_GROUP = 2
D_EXPERT = 512
Q_BLOCK = 128
SCAN_CHUNK = 64
ROPE_THETA = 10000.0
LN_EPS = 1e-5
RMS_EPS = 1e-6
F_FLOOR = 1e-30
DEEPNORM_ALPHA = (2 * DEPTH) ** 0.25
DEEPNORM_BETA = (8 * DEPTH) ** -0.25
F32 = jnp.float32

SPLIT_SIZES = (
    GQA_Q_HEADS * HEAD_DIM, GQA_KV_HEADS * HEAD_DIM, GQA_KV_HEADS * HEAD_DIM,
    NA_HEADS * HEAD_DIM, NA_HEADS * HEAD_DIM, NA_HEADS * HEAD_DIM,
    DIFF_HEADS * 2 * DIFF_QK_DIM, DIFF_HEADS * 2 * DIFF_QK_DIM, DIFF_HEADS * DIFF_V_DIM,
    HG_HEADS * HG_DK, HG_HEADS * HG_DK, HG_HEADS * HG_DK, HG_HEADS * HG_DV, HG_HEADS * HG_DV,
    N_BRANCH * D_MODEL,
)
SPLIT_POINTS = tuple(int(p) for p in np.cumsum(SPLIT_SIZES)[:-1])
D_IN = sum(SPLIT_SIZES)

kernel_name = 'hybrid_diffusion_prefix_trunk_step'


def layer_norm_plain(x):
    xf = x.astype(F32)
    mu = jnp.mean(xf, -1, keepdims=True)
    var = jnp.mean(jnp.square(xf - mu), -1, keepdims=True)
    return ((xf - mu) * lax.rsqrt(var + LN_EPS)).astype(x.dtype)


def layer_norm(x, g, b):
    return layer_norm_plain(x) * g + b


def rms_norm(x, g):
    xf = x.astype(F32)
    return (xf * lax.rsqrt(jnp.mean(jnp.square(xf), -1, keepdims=True) + RMS_EPS)).astype(x.dtype) * g


def modulate(x, shift, scale):
    return layer_norm_plain(x) * (1 + scale) + shift


def axial_rope(x):
    T, d = x.shape[1], x.shape[-1]
    quarter = d // 4
    t = jnp.arange(T)
    inv = ROPE_THETA ** (-jnp.arange(quarter, dtype=F32) / quarter)
    ang_r = (t // GRID_W).astype(F32)[:, None] * inv
    ang_c = (t % GRID_W).astype(F32)[:, None] * inv
    ang = jnp.concatenate([ang_r, ang_r, ang_c, ang_c], axis=-1)
    shape = (T,) + (1,) * (x.ndim - 3) + (d,)
    cos = jnp.cos(ang).reshape(shape).astype(x.dtype)
    sin = jnp.sin(ang).reshape(shape).astype(x.dtype)
    xr = x.reshape(x.shape[:-1] + (2, 2, quarter))
    rot = jnp.concatenate([-xr[..., 1:, :], xr[..., :1, :]], axis=-2).reshape(x.shape)
    return x * cos + rot * sin


def sweep_query_blocks(fn, *qs):
    B, T = qs[0].shape[:2]
    nb = T // Q_BLOCK
    blocks = tuple(jnp.moveaxis(q.reshape((B, nb, Q_BLOCK) + q.shape[2:]), 1, 0) for q in qs)
    out = lax.map(lambda blk: fn(*blk), blocks)
    out = jnp.moveaxis(out, 0, 1)
    return out.reshape((B, T) + out.shape[3:])


def grouped_attention(q, k, v, scale):
    s = jnp.einsum('bqhgd,blhd->bhgql', q, k).astype(F32) * scale
    p = jax.nn.softmax(s, axis=-1).astype(v.dtype)
    return jnp.einsum('bhgql,blhd->bqhgd', p, v)


def diff_attention(q1, q2, k1, k2, v, lam, scale):
    s1 = jnp.einsum('bqhd,blhd->bhql', q1, k1).astype(F32) * scale
    s2 = jnp.einsum('bqhd,blhd->bhql', q2, k2).astype(F32) * scale
    p = jax.nn.softmax(s1, axis=-1) - lam * jax.nn.softmax(s2, axis=-1)
    return jnp.einsum('bhql,blhd->bqhd', p.astype(v.dtype), v)


def differential_mixer(q, k, v, lp):
    B, T = q.shape[:2]
    k1, k2 = k[..., 0, :], k[..., 1, :]
    o = sweep_query_blocks(
        lambda q1, q2: diff_attention(q1, q2, k1, k2, v, lp['diff_lam'], DIFF_QK_DIM ** -0.5),
        q[..., 0, :], q[..., 1, :])
    o = rms_norm(o, lp['diff_subln']) * (1.0 - lp['diff_lam_init'])
    return o.reshape(B, T, DIFF_HEADS * DIFF_V_DIM)


def neighborhood_attention(q, k, v, k_ctx, v_ctx, rpb, scale):
    B, N, H, d = q.shape
    rows = N // GRID_W
    kh = min(NA_KH, rows)
    grid = lambda a: a.reshape(B, rows, GRID_W, H, d)
    kg, vg = grid(k), grid(v)
    col = jnp.arange(GRID_W)
    col_start = jnp.clip(col - NA_KW // 2, 0, GRID_W - NA_KW)
    col_idx = col_start[:, None] + jnp.arange(NA_KW)[None, :]
    col_off = col_idx - col[:, None] + (NA_KW - 1)
    n_loc = kh * NA_KW

    def row_block(args):
        r, q_r = args
        r_start = jnp.clip(r - kh // 2, 0, rows - kh)
        k_win = lax.dynamic_slice_in_dim(kg, r_start, kh, axis=1)[:, :, col_idx]
        v_win = lax.dynamic_slice_in_dim(vg, r_start, kh, axis=1)[:, :, col_idx]
        row_off = r_start + jnp.arange(kh) - r + (NA_KH - 1)
        bias = rpb[:, row_off[:, None, None], col_off[None, :, :]].transpose(0, 2, 1, 3)
        s_loc = jnp.einsum('bwhd,biwjhd->bhwij', q_r, k_win).astype(F32) * scale + bias.astype(F32)
        s_ctx = jnp.einsum('bwhd,blhd->bhwl', q_r, k_ctx).astype(F32) * scale
        p = jax.nn.softmax(jnp.concatenate([s_loc.reshape(B, H, GRID_W, n_loc), s_ctx], axis=-1), axis=-1)
        p = p.astype(v.dtype)
        o = jnp.einsum('bhwij,biwjhd->bwhd', p[..., :n_loc].reshape(B, H, GRID_W, kh, NA_KW), v_win)
        return o + jnp.einsum('bhwl,blhd->bwhd', p[..., n_loc:], v_ctx)

    out = lax.map(row_block, (jnp.arange(rows), jnp.moveaxis(grid(q), 1, 0)))
    return jnp.moveaxis(out, 0, 1).reshape(B, N, H, d)


def hgrn_forget(z, lb):
    B, T, _ = z.shape
    zf = z.astype(F32)
    f = lb + (1.0 - lb) * jax.nn.sigmoid(zf)
    log_f = jnp.log(jnp.maximum(f, F_FLOOR))
    k = (1.0 - lb) * jax.nn.sigmoid(-zf)
    return log_f.reshape(B, T, HG_HEADS, HG_DK), k.reshape(B, T, HG_HEADS, HG_DK)


def hgrn_chunk_scan(q, k, v, log_f, s0):
    B, T, H, dk = q.shape
    dv = v.shape[-1]
    nc = T // SCAN_CHUNK

    def chunks(a):
        return a.astype(F32).reshape(B, nc, SCAN_CHUNK, H, a.shape[-1]).transpose(1, 0, 3, 2, 4)

    tri = jnp.tril(jnp.ones((SCAN_CHUNK, SCAN_CHUNK), bool))[:, :, None]

    def step(S, blk):
        qc, kc, vc, lc = blk
        b = jnp.cumsum(lc, axis=2)
        b_end = b[:, :, -1:, :]
        o_inter = jnp.einsum('bhtk,bhkv->bhtv', qc * jnp.exp(b), S)
        diff = jnp.where(tri, b[:, :, :, None, :] - b[:, :, None, :, :], 0.0)
        decay = jnp.where(tri, jnp.exp(diff), 0.0)
        scores = jnp.einsum('bhtk,bhtsk,bhsk->bhts', qc, decay, kc)
        o = o_inter + jnp.einsum('bhts,bhsv->bhtv', scores, vc)
        S = jnp.exp(b_end[:, :, 0, :, None]) * S + jnp.einsum('bhsk,bhsv->bhkv', kc * jnp.exp(b_end - b), vc)
        return S, o

    S, o = lax.scan(step, s0.astype(F32), (chunks(q), chunks(k), chunks(v), chunks(log_f)))
    o = o.transpose(1, 0, 3, 2, 4).reshape(B, T, H, dv)
    return o.astype(v.dtype), S.astype(v.dtype)


def hgrn_mixer(hd, lp, s0_f, s0_b):
    B, T = hd['hg_q'].shape[:2]
    q, v = hd['hg_q'], hd['hg_v']
    logf_f, k_f = hgrn_forget(hd['hg_zf'], lp['hg_lb_f'])
    logf_b, k_b = hgrn_forget(hd['hg_zb'], lp['hg_lb_b'])
    o_f, s_f = hgrn_chunk_scan(q, k_f, v, logf_f, s0_f)
    rev = lambda a: jnp.flip(a, axis=1)
    o_b, s_b = hgrn_chunk_scan(rev(q), rev(k_b), rev(v), rev(logf_b), s0_b)
    o = rms_norm(o_f + rev(o_b), lp['hg_norm']) * jax.nn.silu(hd['hg_g'])
    return o.reshape(B, T, HG_HEADS * HG_DV), jnp.stack([s_f, s_b], axis=1)


def hier_moe(h, lp):
    B, T, D = h.shape
    x = h.reshape(B * T, D)
    g_logits = (x @ lp['w_group'] + lp['b_group']).astype(F32)
    g_idx = jnp.argmax(g_logits, axis=-1)
    g_onehot = jax.nn.one_hot(g_idx, N_GROUPS, dtype=F32)
    g_prob = jnp.max(jax.nn.softmax(g_logits, axis=-1), axis=-1, keepdims=True)
    e_logits = (jnp.einsum('nd,dge->nge', x, lp['w_router']) + lp['b_router']).astype(F32)
    e_logits = jnp.einsum('nge,ng->ne', e_logits, g_onehot)
    top_v, top_i = lax.top_k(e_logits, TOP_K_IN_GROUP)
    top_w = jax.nn.softmax(top_v, axis=-1) * g_prob
    w_expert = jnp.einsum('nk,nke->ne', top_w, jax.nn.one_hot(top_i, EXPERTS_PER_GROUP, dtype=F32))
    combine = (g_onehot[:, :, None] * w_expert[:, None, :]).astype(x.dtype)
    y = jnp.zeros_like(x)
    for g in range(N_GROUPS):
        hid = jax.nn.silu(jnp.einsum('nd,edf->nef', x, lp['w_gate'][g])) * jnp.einsum('nd,edf->nef', x, lp['w_up'][g])
        y = y + jnp.einsum('nef,efd->nd', hid * combine[:, g, :, None], lp['w_down'][g])
    return y.reshape(B, T, D)


def mixer_heads(h, lp):
    B, T, _ = h.shape
    (gq, gk, gv, nq, nk, nv, dq, dk, dv, hq, hzf, hzb, hv, hg, gates) = jnp.split(
        h @ lp['w_in'], SPLIT_POINTS, axis=-1)
    heads = lambda a, n, d: a.reshape(B, T, n, d)
    return {
        'gqa_q': rms_norm(heads(gq, GQA_Q_HEADS, HEAD_DIM), lp['gqa_qn']),
        'gqa_k': rms_norm(heads(gk, GQA_KV_HEADS, HEAD_DIM), lp['gqa_kn']),
        'gqa_v': heads(gv, GQA_KV_HEADS, HEAD_DIM),
        'na_q': heads(nq, NA_HEADS, HEAD_DIM),
        'na_k': heads(nk, NA_HEADS, HEAD_DIM),
        'na_v': heads(nv, NA_HEADS, HEAD_DIM),
        'diff_q': dq.reshape(B, T, DIFF_HEADS, 2, DIFF_QK_DIM),
        'diff_k': dk.reshape(B, T, DIFF_HEADS, 2, DIFF_QK_DIM),
        'diff_v': heads(dv, DIFF_HEADS, DIFF_V_DIM),
        'hg_q': jax.nn.silu(heads(hq, HG_HEADS, HG_DK)),
        'hg_zf': hzf,
        'hg_zb': hzb,
        'hg_v': heads(hv, HG_HEADS, HG_DV),
        'hg_g': heads(hg, HG_HEADS, HG_DV),
        'gates': gates,
    }


def post_mixer(x, branches, gate_logits, mods, lp):
    B, T, _ = x.shape
    _, _, gate1, shift2, scale2, gate2 = mods
    proj = jnp.einsum('btnc,ncd->btnd', branches, lp['w_branch'])
    gates = jax.nn.sigmoid(gate_logits.reshape(B, T, N_BRANCH, D_MODEL))
    mixed = jnp.sum(gates * proj, axis=2) @ lp['w_out']
    x = layer_norm(DEEPNORM_ALPHA * x + gate1 * mixed, lp['ln_mix_g'], lp['ln_mix_b'])
    m = hier_moe(modulate(x, shift2, scale2), lp)
    return layer_norm(DEEPNORM_ALPHA * x + gate2 * m, lp['ln_ffn_g'], lp['ln_ffn_b'])


def context_layer(x, mods, lp):
    B, S, _ = x.shape
    hd = mixer_heads(modulate(x, mods[0], mods[1]), lp)
    k_a, v_a = hd['gqa_k'], hd['gqa_v']
    o_a = sweep_query_blocks(lambda qb: grouped_attention(qb, k_a, v_a, HEAD_DIM ** -0.5),
                             hd['gqa_q'].reshape(B, S, GQA_KV_HEADS, GQA_GROUP, HEAD_DIM))
    k_n, v_n = hd['na_k'], hd['na_v']
    o_n = sweep_query_blocks(lambda qb: grouped_attention(qb, k_n, v_n, HEAD_DIM ** -0.5), hd['na_q'][:, :, :, None])
    o_d = differential_mixer(hd['diff_q'], hd['diff_k'], hd['diff_v'], lp)
    zero_state = jnp.zeros((B, HG_HEADS, HG_DK, HG_DV), x.dtype)
    o_h, st = hgrn_mixer(hd, lp, zero_state, zero_state)
    branches = jnp.stack([o_a.reshape(B, S, BRANCH_W), o_n.reshape(B, S, BRANCH_W), o_d, o_h], axis=2)
    y = post_mixer(x, branches, hd['gates'], mods, lp)
    return y, (k_a, v_a, k_n, v_n, hd['diff_k'], hd['diff_v'], st)


def latent_layer(x, mods, lp, ctx):
    ctx_gk, ctx_gv, ctx_nk, ctx_nv, ctx_dk, ctx_dv, ctx_state = ctx
    B, N, _ = x.shape
    hd = mixer_heads(modulate(x, mods[0], mods[1]), lp)
    k_a = jnp.concatenate([ctx_gk, axial_rope(hd['gqa_k'])], axis=1)
    v_a = jnp.concatenate([ctx_gv, hd['gqa_v']], axis=1)
    q_a = axial_rope(hd['gqa_q']).reshape(B, N, GQA_KV_HEADS, GQA_GROUP, HEAD_DIM)
    o_a = sweep_query_blocks(lambda qb: grouped_attention(qb, k_a, v_a, HEAD_DIM ** -0.5), q_a)
    o_n = neighborhood_attention(hd['na_q'], hd['na_k'], hd['na_v'], ctx_nk, ctx_nv, lp['na_rpb'], HEAD_DIM ** -0.5)
    k_d = jnp.concatenate([ctx_dk, axial_rope(hd['diff_k'])], axis=1)
    v_d = jnp.concatenate([ctx_dv, hd['diff_v']], axis=1)
    o_d = differential_mixer(axial_rope(hd['diff_q']), k_d, v_d, lp)
    o_h, _ = hgrn_mixer(hd, lp, ctx_state[:, 0], ctx_state[:, 1])
    branches = jnp.stack([o_a.reshape(B, N, BRANCH_W), o_n.reshape(B, N, BRANCH_W), o_d, o_h], axis=2)
    return post_mixer(x, branches, hd['gates'], mods, lp)


def setup_inputs(seed: int = 0) -> dict:
    key = jax.random.key(seed)
    ks = jax.random.split(key, 34)

    def nrm(i, shape, s=1.0):
        return s * jax.random.normal(ks[i], shape, jnp.float32)

    D = D_MODEL
    return {
        'x_prompt': nrm(0, (BATCH, SEQ, D)),
        'x_sample': nrm(1, (DEC_BATCH, DEC_SEQ, D)),
        'cache_gqa_k': nrm(2, (DEC_BATCH, DEPTH, PAST_LEN, GQA_KV_HEADS, HEAD_DIM)),
        'cache_gqa_v': nrm(3, (DEC_BATCH, DEPTH, PAST_LEN, GQA_KV_HEADS, HEAD_DIM)),
        'cache_na_k': nrm(4, (DEC_BATCH, DEPTH, PAST_LEN, NA_HEADS, HEAD_DIM)),
        'cache_na_v': nrm(5, (DEC_BATCH, DEPTH, PAST_LEN, NA_HEADS, HEAD_DIM)),
        'cache_diff_k': nrm(6, (DEC_BATCH, DEPTH, PAST_LEN, DIFF_HEADS, 2, DIFF_QK_DIM)),
        'cache_diff_v': nrm(7, (DEC_BATCH, DEPTH, PAST_LEN, DIFF_HEADS, DIFF_V_DIM)),
        'state_hgrn': nrm(8, (DEC_BATCH, DEPTH, 2, HG_HEADS, HG_DK, HG_DV), 0.5),
        'c': nrm(9, (DEC_BATCH, D)),
        'c_ctx': nrm(10, (D,)),
        'w_ada': nrm(11, (DEPTH, D, 6 * D), 0.5 * D ** -0.5),
        'b_ada': nrm(12, (DEPTH, 6 * D), 0.02),
        'w_in': nrm(13, (DEPTH, D, D_IN), D ** -0.5),
        'gqa_q_norm': 1.0 + nrm(14, (DEPTH, HEAD_DIM), 0.1),
        'gqa_k_norm': 1.0 + nrm(15, (DEPTH, HEAD_DIM), 0.1),
        'na_rpb': nrm(16, (DEPTH, NA_HEADS, 2 * NA_KH - 1, 2 * NA_KW - 1), 0.1),
        'diff_lambda': nrm(17, (DEPTH, 4, DIFF_QK_DIM), 0.1),
        'diff_subln': 1.0 + nrm(18, (DEPTH, DIFF_V_DIM), 0.1),
        'hgrn_lb': nrm(19, (2, DEPTH, HG_HEADS * HG_DK)),
        'hgrn_norm': 1.0 + nrm(20, (DEPTH, HG_DV), 0.1),
        'w_branch': nrm(21, (DEPTH, N_BRANCH, BRANCH_W, D), BRANCH_W ** -0.5),
        'w_out': nrm(22, (DEPTH, D, D), D ** -0.5 * DEEPNORM_BETA),
        'ln_mix_g': 1.0 + nrm(23, (DEPTH, D), 0.1),
        'ln_mix_b': nrm(24, (DEPTH, D), 0.02),
        'ln_ffn_g': 1.0 + nrm(25, (DEPTH, D), 0.1),
        'ln_ffn_b': nrm(26, (DEPTH, D), 0.02),
        'w_group': nrm(27, (DEPTH, D, N_GROUPS), D ** -0.5),
        'b_group': nrm(28, (DEPTH, N_GROUPS), 0.01),
        'w_router': nrm(29, (DEPTH, D, N_GROUPS, EXPERTS_PER_GROUP), D ** -0.5),
        'b_router': nrm(30, (DEPTH, N_GROUPS, EXPERTS_PER_GROUP), 0.01),
        'w_gate': nrm(31, (DEPTH, N_GROUPS, EXPERTS_PER_GROUP, D, D_EXPERT), D ** -0.5),
        'w_up': nrm(32, (DEPTH, N_GROUPS, EXPERTS_PER_GROUP, D, D_EXPERT), D ** -0.5),
        'w_down': nrm(33, (DEPTH, N_GROUPS, EXPERTS_PER_GROUP, D_EXPERT, D), D_EXPERT ** -0.5 * DEEPNORM_BETA),
    }


def reference(x_prompt, x_sample, cache_gqa_k, cache_gqa_v, cache_na_k, cache_na_v, cache_diff_k, cache_diff_v,
              state_hgrn, c, c_ctx, w_ada, b_ada, w_in, gqa_q_norm, gqa_k_norm, na_rpb, diff_lambda, diff_subln,
              hgrn_lb, hgrn_norm, w_branch, w_out, ln_mix_g, ln_mix_b, ln_ffn_g, ln_ffn_b, w_group, b_group,
              w_router, b_router, w_gate, w_up, w_down):
    lb = jax.nn.softmax(hgrn_lb.astype(F32), axis=1)
    lb = jnp.cumsum(lb, axis=1) - lb[:, :1]
    layers = []
    for l in range(DEPTH):
        lam_init = 0.8 - 0.6 * math.exp(-0.3 * l)
        lp_lam = diff_lambda[l].astype(F32)
        lam = jnp.exp(jnp.sum(lp_lam[0] * lp_lam[1])) - jnp.exp(jnp.sum(lp_lam[2] * lp_lam[3])) + lam_init
        layers.append({
            'w_ada': w_ada[l], 'b_ada': b_ada[l], 'w_in': w_in[l],
            'gqa_qn': gqa_q_norm[l], 'gqa_kn': gqa_k_norm[l], 'na_rpb': na_rpb[l],
            'diff_lam': lam, 'diff_lam_init': lam_init, 'diff_subln': diff_subln[l],
            'hg_lb_f': lb[0, l], 'hg_lb_b': lb[1, l], 'hg_norm': hgrn_norm[l],
            'w_branch': w_branch[l], 'w_out': w_out[l],
            'ln_mix_g': ln_mix_g[l], 'ln_mix_b': ln_mix_b[l], 'ln_ffn_g': ln_ffn_g[l], 'ln_ffn_b': ln_ffn_b[l],
            'w_group': w_group[l], 'b_group': b_group[l], 'w_router': w_router[l], 'b_router': b_router[l],
            'w_gate': w_gate[l], 'w_up': w_up[l], 'w_down': w_down[l],
        })

    xp = x_prompt
    ctx_out = []
    for l in range(DEPTH):
        lp = layers[l]
        mods = jnp.split(jax.nn.silu(c_ctx) @ lp['w_ada'] + lp['b_ada'], 6, axis=-1)
        xp, ctx_l = context_layer(xp, mods, lp)
        ctx_out.append(ctx_l)
    stack = lambda i: jnp.stack([t[i] for t in ctx_out], axis=1)

    xs = x_sample
    for l in range(DEPTH):
        lp = layers[l]
        mods = jnp.split((jax.nn.silu(c) @ lp['w_ada'] + lp['b_ada'])[:, None, :], 6, axis=-1)
        ctx = (cache_gqa_k[:, l], cache_gqa_v[:, l], cache_na_k[:, l], cache_na_v[:, l],
               cache_diff_k[:, l], cache_diff_v[:, l], state_hgrn[:, l])
        xs = latent_layer(xs, mods, lp, ctx)

    return (xp, xs, stack(0), stack(1), stack(2), stack(3), stack(4), stack(5), stack(6))
```

```python
import functools
import math

import jax
import jax.numpy as jnp
import numpy as np
from jax import lax
from jax.experimental import pallas as pl
from jax.experimental.pallas import tpu as pltpu

F32 = jnp.float32
BF16 = jnp.bfloat16

D_MODEL = 1024
GRID_W = 64
HEAD_DIM = 64
GQA_Q_HEADS = 4
GQA_KV_HEADS = 2
NA_HEADS = 4
NA_KH = 8
NA_KW = 16
DIFF_HEADS = 4
DIFF_QK_DIM = 32
DIFF_V_DIM = 64
HG_HEADS = 4
HG_DK = 64
HG_DV = 64
BRANCH_W = 256
N_BRANCH = 4
N_GROUPS = 4
EXPERTS_PER_GROUP = 4
N_EXPERTS = N_GROUPS * EXPERTS_PER_GROUP
D_EXPERT = 512
ROPE_THETA = 10000.0
LN_EPS = 1e-5
RMS_EPS = 1e-6
F_FLOOR = 1e-30
NEG_BIG = -1e30

C_GQ, C_GK, C_GV = 0, 256, 384
C_NQ, C_NK, C_NV = 512, 768, 1024
C_DQ, C_DK, C_DV = 1280, 1536, 1792
C_HG = 2048
D_HEADS_IN = 3328
HG_COLS = 1280

VMEM_LIMIT = 56 * 1024 * 1024


def _cparams(sem):
    return pltpu.CompilerParams(dimension_semantics=sem, vmem_limit_bytes=VMEM_LIMIT)


def _dot(a, b):
    return jnp.dot(a, b, preferred_element_type=F32)


def _dot_nt(a, b):
    return lax.dot_general(a, b, (((1,), (1,)), ((), ())), preferred_element_type=F32)


def _split_dot(x, j):
    hi = x.astype(BF16)
    lo = (x - hi.astype(F32)).astype(BF16)
    return _dot(hi, j) + _dot(lo, j)


def _sigmoid(z):
    return 1.0 / (1.0 + jnp.exp(-z))


def _ln_plain(x):
    mu = jnp.mean(x, axis=-1, keepdims=True)
    xc = x - mu
    var = jnp.mean(xc * xc, axis=-1, keepdims=True)
    return xc * lax.rsqrt(var + LN_EPS)


def _block_ones(n, blk):
    idx = np.arange(n) // blk
    return jnp.asarray((idx[:, None] == idx[None, :]).astype(np.float32), dtype=BF16)


def _ada_kernel(c_ref, w_ref, b_ref, o_ref):
    c = c_ref[...]
    s = (c * _sigmoid(c)).astype(BF16)
    o_ref[0] = _dot(s, w_ref[0].astype(BF16)) + b_ref[0]


def _ada_mods(cond, w_ada, b_ada):
    depth, d, n = w_ada.shape
    rows = cond.shape[0]
    tn = 1536
    return pl.pallas_call(
        _ada_kernel,
        grid=(depth, n // tn),
        in_specs=[
            pl.BlockSpec((rows, d), lambda l, j: (0, 0)),
            pl.BlockSpec((1, d, tn), lambda l, j: (l, 0, j)),
            pl.BlockSpec((1, 1, tn), lambda l, j: (l, 0, j)),
        ],
        out_specs=pl.BlockSpec((1, rows, tn), lambda l, j: (l, 0, j)),
        out_shape=jax.ShapeDtypeStruct((depth, rows, n), F32),
        compiler_params=_cparams(("parallel", "parallel")),
        name="ada_mods",
    )(cond, w_ada, b_ada.reshape(depth, 1, n))


def _rope_tables(t_len, d, reps):
    quarter = d // 4
    t = np.arange(t_len)
    inv = ROPE_THETA ** (-np.arange(quarter, dtype=np.float64) / quarter)
    ang_r = (t // GRID_W).astype(np.float64)[:, None] * inv
    ang_c = (t % GRID_W).astype(np.float64)[:, None] * inv
    ang = np.concatenate([ang_r, ang_r, ang_c, ang_c], axis=-1)
    cos, sin = np.cos(ang), np.sin(ang)
    first = (np.arange(d) % (2 * quarter)) < quarter
    sin_a = np.where(first, -sin, 0.0)
    sin_b = np.where(first, 0.0, sin)
    tile = lambda a: jnp.asarray(np.tile(a, (1, reps)), dtype=F32)
    return tile(cos), tile(sin_a), tile(sin_b)


def _rope(x, cos, sin_a, sin_b, quarter):
    w = x.shape[-1]
    return x * cos + pltpu.roll(x, w - quarter, 1) * sin_a + pltpu.roll(x, quarter, 1) * sin_b


def _seg_rms(x, j):
    ss = _split_dot(x * x, j)
    return x * lax.rsqrt(ss * (1.0 / HEAD_DIM) + RMS_EPS)


def _inproj_kernel(*refs, rope):
    if rope:
        (x_ref, sh_ref, sc_ref, w_ref, qn_ref, kn_ref, j_ref,
         c64_ref, a64_ref, b64_ref, c32_ref, a32_ref, b32_ref,
         qa_ref, ka_ref, va_ref, qnb_ref, knb_ref, vnb_ref, qd_ref, kd_ref, vd_ref, hg_ref) = refs
    else:
        (x_ref, sh_ref, sc_ref, w_ref, qn_ref, kn_ref, j_ref,
         qa_ref, ka_ref, va_ref, qnb_ref, knb_ref, vnb_ref, qd_ref, kd_ref, vd_ref, hg_ref) = refs
    h = (_ln_plain(x_ref[0]) * (1.0 + sc_ref[0]) + sh_ref[0]).astype(BF16)
    proj = lambda c0, n: _dot(h, w_ref[:, c0:c0 + n])

    gq = _seg_rms(proj(C_GQ, 256), j_ref[...]) * qn_ref[...]
    gk = _seg_rms(proj(C_GK, 128), j_ref[0:128, 0:128]) * kn_ref[...]
    dq = proj(C_DQ, 256)
    dk = proj(C_DK, 256)
    if rope:
        c64, a64, b64 = c64_ref[...], a64_ref[...], b64_ref[...]
        c32, a32, b32 = c32_ref[...], a32_ref[...], b32_ref[...]
        gq = _rope(gq, c64, a64, b64, HEAD_DIM // 4)
        gk = _rope(gk, c64[:, 0:128], a64[:, 0:128], b64[:, 0:128], HEAD_DIM // 4)
        dq = _rope(dq, c32, a32, b32, DIFF_QK_DIM // 4)
        dk = _rope(dk, c32, a32, b32, DIFF_QK_DIM // 4)
    qa_ref[0] = (gq * HEAD_DIM ** -0.5).astype(qa_ref.dtype)
    ka_ref[0] = gk.astype(ka_ref.dtype)
    va_ref[0] = proj(C_GV, 128).astype(va_ref.dtype)
    qnb_ref[0] = (proj(C_NQ, 256) * HEAD_DIM ** -0.5).astype(qnb_ref.dtype)
    knb_ref[0] = proj(C_NK, 256).astype(knb_ref.dtype)
    vnb_ref[0] = proj(C_NV, 256).astype(vnb_ref.dtype)
    qd_ref[0] = (dq * DIFF_QK_DIM ** -0.5).astype(qd_ref.dtype)
    kd_ref[0] = dk.astype(kd_ref.dtype)
    vd_ref[0] = proj(C_DV, 256).astype(vd_ref.dtype)
    for s in range(HG_COLS // 256):
        hg_ref[0, :, s * 256:(s + 1) * 256] = proj(C_HG + s * 256, 256)


def _inproj(x, shift, scale, w_heads, qn, kn, rope, kv_dtype, tm=256):
    bsz, t_len, d = x.shape
    nb = shift.shape[0]
    mod_map = (lambda b, i: (b, 0, 0)) if nb > 1 else (lambda b, i: (0, 0, 0))
    tok = lambda n: pl.BlockSpec((1, tm, n), lambda b, i: (b, i, 0))
    const = lambda shape: pl.BlockSpec(shape, lambda b, i: (0,) * len(shape))
    in_specs = [tok(d), pl.BlockSpec((1, 1, d), mod_map), pl.BlockSpec((1, 1, d), mod_map),
                const((d, D_HEADS_IN)), const((1, 256)), const((1, 128)), const((256, 256))]
    args = [x, shift, scale, w_heads, jnp.tile(qn, GQA_Q_HEADS)[None], jnp.tile(kn, GQA_KV_HEADS)[None],
            _block_ones(256, HEAD_DIM)]
    if rope:
        in_specs += [pl.BlockSpec((tm, 256), lambda b, i: (i, 0))] * 6
        args += list(_rope_tables(t_len, HEAD_DIM, 4)) + list(_rope_tables(t_len, DIFF_QK_DIM, 8))
    widths = (256, 128, 128, 256, 256, 256, 256, 256, 256)
    dtypes = (BF16, kv_dtype, kv_dtype, BF16, kv_dtype, kv_dtype, BF16, kv_dtype, kv_dtype)
    out_shape = [jax.ShapeDtypeStruct((bsz, t_len, n), dt) for n, dt in zip(widths, dtypes)]
    out_shape.append(jax.ShapeDtypeStruct((bsz, t_len, HG_COLS), F32))
    out_specs = [tok(n) for n in widths] + [tok(HG_COLS)]
    return pl.pallas_call(
        functools.partial(_inproj_kernel, rope=rope),
        grid=(bsz, t_len // tm),
        in_specs=in_specs,
        out_specs=out_specs,
        out_shape=out_shape,
        compiler_params=_cparams(("parallel", "parallel")),
        name="in_proj_rope" if rope else "in_proj",
    )(*args)


def _softmax_pv(s, v):
    m = jnp.max(s, axis=-1, keepdims=True)
    p = jnp.exp(s - m)
    l = jnp.sum(p, axis=-1, keepdims=True)
    return _dot(p.astype(BF16), v) * (1.0 / l)


def _attn_kernel(q_ref, kt_ref, v_ref, o_ref, *, hq, group, d):
    q = q_ref[0]
    outs = []
    for h in range(hq):
        s = _dot(q[:, h * d:(h + 1) * d], kt_ref[0, h // group])
        outs.append(_softmax_pv(s, v_ref[0, h // group]))
    o_ref[0] = jnp.concatenate(outs, axis=-1).astype(o_ref.dtype)


def _attention(q, kt, v, hq, tq):
    bsz, t_len, qw = q.shape
    _, hkv, d, l_len = kt.shape
    dv = v.shape[-1]
    return pl.pallas_call(
        functools.partial(_attn_kernel, hq=hq, group=hq // hkv, d=d),
        grid=(bsz, t_len // tq),
        in_specs=[
            pl.BlockSpec((1, tq, qw), lambda b, i: (b, i, 0)),
            pl.BlockSpec((1, hkv, d, l_len), lambda b, i: (b, 0, 0, 0)),
            pl.BlockSpec((1, hkv, l_len, dv), lambda b, i: (b, 0, 0, 0)),
        ],
        out_specs=pl.BlockSpec((1, tq, hq * dv), lambda b, i: (b, i, 0)),
        out_shape=jax.ShapeDtypeStruct((bsz, t_len, hq * dv), BF16),
        compiler_params=_cparams(("parallel", "parallel")),
        name="attention",
    )(q, kt, v)


def _diff_attn_kernel(q_ref, kt_ref, v_ref, par_ref, o_ref):
    q = q_ref[0]
    lam = par_ref[0:1, :]
    gain = par_ref[1:2, :]
    dq = DIFF_QK_DIM
    outs = []
    for h in range(DIFF_HEADS):
        v = v_ref[0, h]
        o1 = _softmax_pv(_dot(q[:, 2 * h * dq:(2 * h + 1) * dq], kt_ref[0, 2 * h]), v)
        o2 = _softmax_pv(_dot(q[:, (2 * h + 1) * dq:(2 * h + 2) * dq], kt_ref[0, 2 * h + 1]), v)
        o = o1 - lam * o2
        ms = jnp.mean(o * o, axis=-1, keepdims=True)
        outs.append(o * lax.rsqrt(ms + RMS_EPS) * gain)
    o_ref[0] = jnp.concatenate(outs, axis=-1).astype(o_ref.dtype)


def _diff_attention(q, kt, v, par, tq):
    bsz, t_len, qw = q.shape
    l_len = kt.shape[-1]
    return pl.pallas_call(
        _diff_attn_kernel,
        grid=(bsz, t_len // tq),
        in_specs=[
            pl.BlockSpec((1, tq, qw), lambda b, i: (b, i, 0)),
            pl.BlockSpec((1, 2 * DIFF_HEADS, DIFF_QK_DIM, l_len), lambda b, i: (b, 0, 0, 0)),
            pl.BlockSpec((1, DIFF_HEADS, l_len, DIFF_V_DIM), lambda b, i: (b, 0, 0, 0)),
            pl.BlockSpec((8, DIFF_V_DIM), lambda b, i: (0, 0)),
        ],
        out_specs=pl.BlockSpec((1, tq, DIFF_HEADS * DIFF_V_DIM), lambda b, i: (b, i, 0)),
        out_shape=jax.ShapeDtypeStruct((bsz, t_len, DIFF_HEADS * DIFF_V_DIM), BF16),
        compiler_params=_cparams(("parallel", "parallel")),
        name="diff_attention",
    )(q, kt, v, par)


def _na_bias_table(rpb):
    shift = np.arange(NA_KH)[:, None]
    win_row = np.arange(NA_KH)[None, :]
    row_off = win_row - shift + (NA_KH - 1)
    col = np.arange(GRID_W)
    col_start = np.clip(col - NA_KW // 2, 0, GRID_W - NA_KW)
    key_col = np.arange(GRID_W)[None, :]
    valid = (key_col >= col_start[:, None]) & (key_col < col_start[:, None] + NA_KW)
    col_off = np.clip(key_col - col[:, None] + (NA_KW - 1), 0, 2 * NA_KW - 2)
    tbl = rpb[:, row_off[:, :, None, None], col_off[None, None, :, :]]
    tbl = jnp.where(valid[None, None, None], tbl.astype(F32), NEG_BIG)
    tbl = jnp.transpose(tbl, (1, 0, 3, 2, 4))
    return tbl.reshape(NA_KH, rpb.shape[0], GRID_W, NA_KH * GRID_W)


def _na_kernel(q_ref, k_ref, v_ref, kc_ref, vc_ref, bias_ref, o_ref, *, rows_per_step, rows):
    i = pl.program_id(1)
    n_win = NA_KH * GRID_W
    for rr in range(rows_per_step):
        r = i * rows_per_step + rr
        r_start = jnp.clip(r - NA_KH // 2, 0, rows - NA_KH)
        shift = r - r_start
        start = pl.multiple_of(r_start * GRID_W, GRID_W)
        q = q_ref[0, rr * GRID_W:(rr + 1) * GRID_W, :]
        outs = []
        for h in range(NA_HEADS):
            qh = q[:, h * HEAD_DIM:(h + 1) * HEAD_DIM]
            s_loc = _dot_nt(qh, k_ref[0, h, pl.ds(start, n_win), :]) + bias_ref[shift, h]
            s_ctx = _dot_nt(qh, kc_ref[0, h])
            m = jnp.maximum(jnp.max(s_loc, axis=-1, keepdims=True), jnp.max(s_ctx, axis=-1, keepdims=True))
            p_loc = jnp.exp(s_loc - m)
            p_ctx = jnp.exp(s_ctx - m)
            l = jnp.sum(p_loc, axis=-1, keepdims=True) + jnp.sum(p_ctx, axis=-1, keepdims=True)
            o = _dot(p_loc.astype(BF16), v_ref[0, h, pl.ds(start, n_win), :]) + _dot(p_ctx.astype(BF16), vc_ref[0, h])
            outs.append(o * (1.0 / l))
        o_ref[0, rr * GRID_W:(rr + 1) * GRID_W, :] = jnp.concatenate(outs, axis=-1).astype(o_ref.dtype)


def _na_attention(q, k, v, kc, vc, bias, rows_per_step=8):
    bsz, n, qw = q.shape
    rows = n // GRID_W
    assert rows >= NA_KH and rows % rows_per_step == 0
    l_ctx = kc.shape[2]
    tq = rows_per_step * GRID_W
    full = lambda shape: pl.BlockSpec((1,) + shape, lambda b, i: (b,) + (0,) * len(shape))
    return pl.pallas_call(
        functools.partial(_na_kernel, rows_per_step=rows_per_step, rows=rows),
        grid=(bsz, rows // rows_per_step),
        in_specs=[
            pl.BlockSpec((1, tq, qw), lambda b, i: (b, i, 0)),
            full((NA_HEADS, n, HEAD_DIM)), full((NA_HEADS, n, HEAD_DIM)),
            full((NA_HEADS, l_ctx, HEAD_DIM)), full((NA_HEADS, l_ctx, HEAD_DIM)),
            pl.BlockSpec(bias.shape, lambda b, i: (0, 0, 0, 0)),
        ],
        out_specs=pl.BlockSpec((1, tq, qw), lambda b, i: (b, i, 0)),
        out_shape=jax.ShapeDtypeStruct((bsz, n, qw), BF16),
        compiler_params=_cparams(("parallel", "parallel")),
        name="na_attention",
    )(q, k, v, kc, vc, bias)


HG_SUB = 16
HG_W = HG_HEADS * HG_DK


def _hgrn_direction(hg_ref, vt_ref, lb, tri, jv, bd_mask, st_ref, z_col, reverse, tc):
    c = HG_SUB
    hq = hg_ref[0, :, 0:HG_W]
    q = hq * _sigmoid(hq)
    z = hg_ref[0, :, z_col:z_col + HG_W]
    v = hg_ref[0, :, 3 * HG_W:4 * HG_W]
    f = lb + (1.0 - lb) * _sigmoid(z)
    log_f = jnp.log(jnp.maximum(f, F_FLOOR))
    kk = (1.0 - lb) * _sigmoid(-z)
    b = jnp.dot(tri, log_f, preferred_element_type=F32, precision=lax.Precision.HIGHEST)
    vt = vt_ref[0]
    row = lax.broadcasted_iota(jnp.int32, (c, HG_W), 0)
    order = range(tc // c - 1, -1, -1) if reverse else range(tc // c)
    outs = [None] * (tc // c)
    for j in order:
        sl = slice(j * c, (j + 1) * c)
        bj, qj, kj, vj = b[sl], q[sl], kk[sl], v[sl]
        st = st_ref[...]
        o = _dot_nt((qj * jnp.exp(bj)).astype(BF16), st.astype(BF16))
        prods = []
        for s in range(c):
            keep = (row <= s) if reverse else (row >= s)
            decay = jnp.where(keep, jnp.exp(bj - bj[s:s + 1]), 0.0)
            prods.append(qj * decay * kj[s:s + 1])
        scores = _split_dot(jnp.concatenate(prods, axis=0), jv)
        for s in range(c):
            o = o + scores[s * c:(s + 1) * c] * vj[s:s + 1]
        b_end = bj[0:1] if reverse else bj[c - 1:c]
        kd = (kj * jnp.exp(b_end - bj)).astype(BF16)
        upd = _dot(vt[:, sl].astype(BF16), kd)
        st_ref[...] = jnp.exp(b_end) * st + jnp.where(bd_mask, upd, 0.0)
        outs[j] = o
    return jnp.concatenate(outs, axis=0)


def _hgrn_kernel(hg_ref, vt_ref, lb_ref, gn_ref, s0_ref, tri_ref, jv_ref, o_ref, sfin_ref, st_scr, of_scr, *, tc, nc):
    p = pl.program_id(1)
    i = pl.program_id(2)
    r_i = lax.broadcasted_iota(jnp.int32, (HG_W, HG_W), 0) // HG_DV
    c_i = lax.broadcasted_iota(jnp.int32, (HG_W, HG_W), 1) // HG_DK
    bd_mask = r_i == c_i
    jv = jv_ref[...]

    @pl.when(i == 0)
    def _():
        st_scr[...] = s0_ref[0, p]

    @pl.when(p == 0)
    def _():
        o = _hgrn_direction(hg_ref, vt_ref, lb_ref[0:1, :], tri_ref[0], jv, bd_mask, st_scr, HG_W, False, tc)
        of_scr[pl.ds(pl.multiple_of(i * tc, tc), tc), :] = o

    @pl.when(p == 1)
    def _():
        o_b = _hgrn_direction(hg_ref, vt_ref, lb_ref[1:2, :], tri_ref[1], jv, bd_mask, st_scr, 2 * HG_W, True, tc)
        o = of_scr[pl.ds(pl.multiple_of((nc - 1 - i) * tc, tc), tc), :] + o_b
        ms = _split_dot(o * o, jv) * (1.0 / HG_DV)
        g = hg_ref[0, :, 4 * HG_W:5 * HG_W]
        o_ref[0] = (o * lax.rsqrt(ms + RMS_EPS) * gn_ref[...] * (g * _sigmoid(g))).astype(o_ref.dtype)

    @pl.when(i == nc - 1)
    def _():
        sfin_ref[0, p] = st_scr[...]


def _hgrn(hg, lb2, gn, s0t, tc=256):
    bsz, t_len, _ = hg.shape
    nc = t_len // tc
    vt = jnp.swapaxes(hg[:, :, 3 * HG_W:4 * HG_W], 1, 2)
    blk = np.arange(tc) // HG_SUB
    same = blk[:, None] == blk[None, :]
    lower = np.arange(tc)[:, None] >= np.arange(tc)[None, :]
    tri = jnp.asarray(np.stack([same & lower, same & ~lower | same & np.eye(tc, dtype=bool)]).astype(np.float32))
    chunk = lambda b, p, i: (b, i + p * (nc - 1 - 2 * i), 0)
    o, sfin = pl.pallas_call(
        functools.partial(_hgrn_kernel, tc=tc, nc=nc),
        grid=(bsz, 2, nc),
        in_specs=[
            pl.BlockSpec((1, tc, HG_COLS), chunk),
            pl.BlockSpec((1, HG_W, tc), lambda b, p, i: (b, 0, i + p * (nc - 1 - 2 * i))),
            pl.BlockSpec((2, HG_W), lambda b, p, i: (0, 0)),
            pl.BlockSpec((1, HG_W), lambda b, p, i: (0, 0)),
            pl.BlockSpec((1, 2, HG_W, HG_W), lambda b, p, i: (b, 0, 0, 0)),
            pl.BlockSpec((2, tc, tc), lambda b, p, i: (0, 0, 0)),
            pl.BlockSpec((HG_W, HG_W), lambda b, p, i: (0, 0)),
        ],
        out_specs=[
            pl.BlockSpec((1, tc, HG_W), lambda b, p, i: (b, nc - 1 - i * p, 0)),
            pl.BlockSpec((1, 2, HG_W, HG_W), lambda b, p, i: (b, 0, 0, 0)),
        ],
        out_shape=[jax.ShapeDtypeStruct((bsz, t_len, HG_W), BF16),
                   jax.ShapeDtypeStruct((bsz, 2, HG_W, HG_W), F32)],
        scratch_shapes=[pltpu.VMEM((HG_W, HG_W), F32), pltpu.VMEM((t_len, HG_W), F32)],
        compiler_params=_cparams(("parallel", "arbitrary", "arbitrary")),
        name="hgrn",
    )(hg, vt, lb2, jnp.tile(gn, HG_HEADS)[None], s0t, tri, _block_ones(HG_W, HG_DV))
    return o, sfin


def _state_to_blockdiag_t(s0):
    bsz = s0.shape[0]
    s0t = jnp.swapaxes(s0, -1, -2)
    eye = jnp.eye(HG_HEADS, dtype=s0.dtype)
    return jnp.einsum("bdhvk,hg->bdhvgk", s0t, eye).reshape(bsz, 2, HG_W, HG_W)


def _blockdiag_t_to_state(st):
    bsz = st.shape[0]
    s6 = st.reshape(bsz, 2, HG_HEADS, HG_DV, HG_HEADS, HG_DK)
    diag = jnp.stack([s6[:, :, h, :, h, :] for h in range(HG_HEADS)], axis=2)
    return jnp.swapaxes(diag, -1, -2)


def _post_kernel(x_ref, oa_ref, on_ref, od_ref, oh_ref, sh_ref, sc_ref, g1_ref, wg_ref, wb_ref, wo_ref,
                 lg_ref, lb_ref, o_ref, *, alpha):
    x = x_ref[0]
    h = (_ln_plain(x) * (1.0 + sc_ref[0]) + sh_ref[0]).astype(BF16)
    mixed = None
    for n, br_ref in enumerate((oa_ref, on_ref, od_ref, oh_ref)):
        gate = _sigmoid(_dot(h, wg_ref[:, n * D_MODEL:(n + 1) * D_MODEL]))
        term = gate * _dot(br_ref[0], wb_ref[n])
        mixed = term if mixed is None else mixed + term
    y = _dot(mixed.astype(BF16), wo_ref[...])
    o_ref[0] = _ln_plain(alpha * x + g1_ref[0] * y) * lg_ref[...] + lb_ref[...]


def _post_mixer(x, branches, shift, scale, gate, w_gates, w_branch, w_out, ln_g, ln_b, alpha, tm=256):
    bsz, t_len, d = x.shape
    nb = shift.shape[0]
    mod_map = (lambda b, i: (b, 0, 0)) if nb > 1 else (lambda b, i: (0, 0, 0))
    tok = lambda n: pl.BlockSpec((1, tm, n), lambda b, i: (b, i, 0))
    const = lambda shape: pl.BlockSpec(shape, lambda b, i: (0,) * len(shape))
    mod = pl.BlockSpec((1, 1, d), mod_map)
    return pl.pallas_call(
        functools.partial(_post_kernel, alpha=alpha),
        grid=(bsz, t_len // tm),
        in_specs=[tok(d)] + [tok(BRANCH_W)] * 4 + [mod, mod, mod,
                  const((d, N_BRANCH * d)), const((N_BRANCH, BRANCH_W, d)), const((d, d)),
                  const((1, d)), const((1, d))],
        out_specs=tok(d),
        out_shape=jax.ShapeDtypeStruct((bsz, t_len, d), F32),
        compiler_params=_cparams(("parallel", "parallel")),
        name="post_mixer",
    )(x, *branches, shift, scale, gate, w_gates, w_branch, w_out, ln_g[None], ln_b[None])


ROUTER_LANES = 128


def _router_weights(logits):
    lane = lax.broadcasted_iota(jnp.int32, logits.shape, 1)
    first_at = lambda mask: jnp.min(jnp.where(mask, lane, ROUTER_LANES), axis=-1, keepdims=True)
    is_g = lane < N_GROUPS
    gl = jnp.where(is_g, logits, NEG_BIG)
    g_max = jnp.max(gl, axis=-1, keepdims=True)
    g_idx = first_at(is_g & (gl == g_max))
    g_prob = 1.0 / jnp.sum(jnp.where(is_g, jnp.exp(gl - g_max), 0.0), axis=-1, keepdims=True)
    lo = N_GROUPS + g_idx * EXPERTS_PER_GROUP
    in_grp = (lane >= lo) & (lane < lo + EXPERTS_PER_GROUP)
    el = jnp.where(in_grp, logits, NEG_BIG)
    v1 = jnp.max(el, axis=-1, keepdims=True)
    i1 = first_at(in_grp & (el == v1))
    rest = in_grp & (lane != i1)
    el2 = jnp.where(rest, logits, NEG_BIG)
    v2 = jnp.max(el2, axis=-1, keepdims=True)
    i2 = first_at(rest & (el2 == v2))
    e2 = jnp.exp(v2 - v1)
    w1 = g_prob / (1.0 + e2)
    w2 = g_prob * e2 / (1.0 + e2)
    return jnp.where(lane == i1, w1, jnp.where(lane == i2, w2, 0.0))


def _moe_kernel(x_ref, sh_ref, sc_ref, g2_ref, wr_ref, br_ref, wg_ref, wu_ref, wd_ref, lg_ref, lb_ref, o_ref,
                xm_scr, cw_scr, acc_scr, *, alpha):
    e = pl.program_id(2)

    @pl.when(e == 0)
    def _():
        xm = _ln_plain(x_ref[0]) * (1.0 + sc_ref[0]) + sh_ref[0]
        xm_scr[...] = xm.astype(BF16)
        logits = jnp.dot(xm, wr_ref[...], preferred_element_type=F32, precision=lax.Precision.HIGHEST) + br_ref[...]
        cw_scr[...] = _router_weights(logits)
        acc_scr[...] = jnp.zeros_like(acc_scr)

    xm = xm_scr[...]
    gate = _dot(xm, wg_ref[0])
    hid = gate * _sigmoid(gate) * _dot(xm, wu_ref[0])
    lane = lax.broadcasted_iota(jnp.int32, cw_scr.shape, 1)
    w_e = jnp.sum(jnp.where(lane == N_GROUPS + e, cw_scr[...], 0.0), axis=-1, keepdims=True)
    acc_scr[...] += _dot((hid * w_e).astype(BF16), wd_ref[0])

    @pl.when(e == N_EXPERTS - 1)
    def _():
        o_ref[0] = _ln_plain(alpha * x_ref[0] + g2_ref[0] * acc_scr[...]) * lg_ref[...] + lb_ref[...]


def _moe(x, shift, scale, gate, w_route, b_route, w_gate, w_up, w_down, ln_g, ln_b, alpha, tm=512):
    bsz, t_len, d = x.shape
    tm = min(tm, t_len)
    nb = shift.shape[0]
    mod_map = (lambda b, i, e: (b, 0, 0)) if nb > 1 else (lambda b, i, e: (0, 0, 0))
    tok = pl.BlockSpec((1, tm, d), lambda b, i, e: (b, i, 0))
    const = lambda shape: pl.BlockSpec(shape, lambda b, i, e: (0,) * len(shape))
    mod = pl.BlockSpec((1, 1, d), mod_map)
    expert = lambda shape: pl.BlockSpec((1,) + shape, lambda b, i, e: (e, 0, 0))
    return pl.pallas_call(
        functools.partial(_moe_kernel, alpha=alpha),
        grid=(bsz, t_len // tm, N_EXPERTS),
        in_specs=[tok, mod, mod, mod, const((d, ROUTER_LANES)), const((1, ROUTER_LANES)),
                  expert((d, D_EXPERT)), expert((d, D_EXPERT)), expert((D_EXPERT, d)),
                  const((1, d)), const((1, d))],
        out_specs=tok,
        out_shape=jax.ShapeDtypeStruct((bsz, t_len, d), F32),
        scratch_shapes=[pltpu.VMEM((tm, d), BF16), pltpu.VMEM((tm, ROUTER_LANES), F32), pltpu.VMEM((tm, d), F32)],
        compiler_params=_cparams(("parallel", "parallel", "arbitrary")),
        name="moe",
    )(x, shift, scale, gate, w_route, b_route, w_gate, w_up, w_down, ln_g[None], ln_b[None])


def _heads_t(a, heads, d):
    bsz, l_len, _ = a.shape
    return jnp.transpose(a.reshape(bsz, l_len, heads, d), (0, 2, 3, 1))


def _heads_rows(a, heads, d):
    bsz, l_len, _ = a.shape
    return jnp.transpose(a.reshape(bsz, l_len, heads, d), (0, 2, 1, 3))


def _layer(x, mods, lp, ctx, alpha):
    bsz, t_len, _ = x.shape
    latent = ctx is not None
    shift1, scale1, gate1, shift2, scale2, gate2 = mods
    kv_dtype = BF16 if latent else F32
    qa, ka, va, qn, kn, vn, qd, kd, vd, hg = _inproj(
        x, shift1, scale1, lp["w_heads"], lp["gqa_qn"], lp["gqa_kn"], latent, kv_dtype)
    flat = lambda a: a.reshape(a.shape[0], a.shape[1], -1).astype(BF16)
    if latent:
        c_gk, c_gv, c_nk, c_nv, c_dk, c_dv, c_state = ctx
        ka_all = jnp.concatenate([flat(c_gk), ka], axis=1)
        va_all = jnp.concatenate([flat(c_gv), va], axis=1)
        kd_all = jnp.concatenate([flat(c_dk), kd], axis=1)
        vd_all = jnp.concatenate([flat(c_dv), vd], axis=1)
        s0t = _state_to_blockdiag_t(c_state.astype(F32))
        tq = 128
    else:
        ka_all, va_all, kd_all, vd_all = flat(ka), flat(va), flat(kd), flat(vd)
        s0t = jnp.zeros((bsz, 2, HG_W, HG_W), F32)
        tq = t_len
    o_a = _attention(qa, _heads_t(ka_all, GQA_KV_HEADS, HEAD_DIM), _heads_rows(va_all, GQA_KV_HEADS, HEAD_DIM),
                     GQA_Q_HEADS, tq)
    if latent:
        o_n = _na_attention(qn, _heads_rows(kn, NA_HEADS, HEAD_DIM), _heads_rows(vn, NA_HEADS, HEAD_DIM),
                            _heads_rows(flat(c_nk), NA_HEADS, HEAD_DIM), _heads_rows(flat(c_nv), NA_HEADS, HEAD_DIM),
                            lp["na_bias"])
    else:
        o_n = _attention(qn, _heads_t(flat(kn), NA_HEADS, HEAD_DIM), _heads_rows(flat(vn), NA_HEADS, HEAD_DIM),
                         NA_HEADS, tq)
    o_d = _diff_attention(qd, _heads_t(kd_all, 2 * DIFF_HEADS, DIFF_QK_DIM), _heads_rows(vd_all, DIFF_HEADS, DIFF_V_DIM),
                          lp["diff_par"], tq)
    o_h, sfin = _hgrn(hg, lp["hg_lb"], lp["hg_norm"], s0t)
    x1 = _post_mixer(x, (o_a, o_n, o_d, o_h), shift1, scale1, gate1, lp["w_gates"], lp["w_branch"], lp["w_out"],
                     lp["ln_mix_g"], lp["ln_mix_b"], alpha)
    y = _moe(x1, shift2, scale2, gate2, lp["w_route"], lp["b_route"], lp["w_gate"], lp["w_up"], lp["w_down"],
             lp["ln_ffn_g"], lp["ln_ffn_b"], alpha)
    caches = None
    if not latent:
        caches = (ka.reshape(bsz, t_len, GQA_KV_HEADS, HEAD_DIM), va.reshape(bsz, t_len, GQA_KV_HEADS, HEAD_DIM),
                  kn.reshape(bsz, t_len, NA_HEADS, HEAD_DIM), vn.reshape(bsz, t_len, NA_HEADS, HEAD_DIM),
                  kd.reshape(bsz, t_len, DIFF_HEADS, 2, DIFF_QK_DIM), vd.reshape(bsz, t_len, DIFF_HEADS, DIFF_V_DIM),
                  _blockdiag_t_to_state(sfin))
    return y, caches


def kernel(x_prompt, x_sample, cache_gqa_k, cache_gqa_v, cache_na_k, cache_na_v, cache_diff_k, cache_diff_v, state_hgrn, c, c_ctx, w_ada, b_ada, w_in, gqa_q_norm, gqa_k_norm, na_rpb, diff_lambda, diff_subln, hgrn_lb, hgrn_norm, w_branch, w_out, ln_mix_g, ln_mix_b, ln_ffn_g, ln_ffn_b, w_group, b_group, w_router, b_router, w_gate, w_up, w_down):
    depth = w_in.shape[0]
    d = D_MODEL
    alpha = (2 * depth) ** 0.25
    dec_b = c.shape[0]

    lb = jax.nn.softmax(hgrn_lb.astype(F32), axis=1)
    lb = jnp.cumsum(lb, axis=1) - lb[:, :1]

    cond_rows = -(-(dec_b + 1) // 8) * 8
    cond = jnp.zeros((cond_rows, d), F32).at[:dec_b].set(c).at[dec_b].set(c_ctx)
    mods_all = _ada_mods(cond, w_ada, b_ada)

    layers = []
    for l in range(depth):
        lam_init = 0.8 - 0.6 * math.exp(-0.3 * l)
        lp_lam = diff_lambda[l].astype(F32)
        lam = jnp.exp(jnp.sum(lp_lam[0] * lp_lam[1])) - jnp.exp(jnp.sum(lp_lam[2] * lp_lam[3])) + lam_init
        diff_par = jnp.zeros((8, DIFF_V_DIM), F32).at[0].set(lam).at[1].set(diff_subln[l] * (1.0 - lam_init))
        w_route = jnp.zeros((d, ROUTER_LANES), F32)
        w_route = w_route.at[:, :N_GROUPS].set(w_group[l]).at[:, N_GROUPS:N_GROUPS + N_EXPERTS].set(
            w_router[l].reshape(d, N_EXPERTS))
        b_route = jnp.zeros((1, ROUTER_LANES), F32)
        b_route = b_route.at[0, :N_GROUPS].set(b_group[l]).at[0, N_GROUPS:N_GROUPS + N_EXPERTS].set(
            b_router[l].reshape(N_EXPERTS))
        layers.append({
            "w_heads": w_in[l, :, :D_HEADS_IN].astype(BF16),
            "w_gates": w_in[l, :, D_HEADS_IN:].astype(BF16),
            "gqa_qn": gqa_q_norm[l], "gqa_kn": gqa_k_norm[l],
            "na_bias": _na_bias_table(na_rpb[l]),
            "diff_par": diff_par,
            "hg_lb": lb[:, l], "hg_norm": hgrn_norm[l],
            "w_branch": w_branch[l].astype(BF16), "w_out": w_out[l].astype(BF16),
            "ln_mix_g": ln_mix_g[l], "ln_mix_b": ln_mix_b[l], "ln_ffn_g": ln_ffn_g[l], "ln_ffn_b": ln_ffn_b[l],
            "w_route": w_route, "b_route": b_route,
            "w_gate": w_gate[l].reshape(N_EXPERTS, d, D_EXPERT).astype(BF16),
            "w_up": w_up[l].reshape(N_EXPERTS, d, D_EXPERT).astype(BF16),
            "w_down": w_down[l].reshape(N_EXPERTS, D_EXPERT, d).astype(BF16),
        })

    def split_mods(rows):
        return tuple(rows[:, None, k * d:(k + 1) * d] for k in range(6))

    xp = x_prompt
    ctx_out = []
    for l in range(depth):
        xp, caches = _layer(xp, split_mods(mods_all[l, dec_b:dec_b + 1]), layers[l], None, alpha)
        ctx_out.append(caches)
    stack = lambda i: jnp.stack([t[i] for t in ctx_out], axis=1)

    xs = x_sample
    for l in range(depth):
        ctx = (cache_gqa_k[:, l], cache_gqa_v[:, l], cache_na_k[:, l], cache_na_v[:, l],
               cache_diff_k[:, l], cache_diff_v[:, l], state_hgrn[:, l])
        xs, _ = _layer(xs, split_mods(mods_all[l, :dec_b]), layers[l], ctx, alpha)

    return (xp, xs, stack(0), stack(1), stack(2), stack(3), stack(4), stack(5), stack(6))
```

```python
import functools
import math

import jax
import jax.numpy as jnp
import numpy as np
from jax import lax
from jax.experimental import pallas as pl
from jax.experimental.pallas import tpu as pltpu

F32 = jnp.float32
BF16 = jnp.bfloat16

D_MODEL = 1024
GRID_W = 64
HEAD_DIM = 64
GQA_Q_HEADS = 4
GQA_KV_HEADS = 2
NA_HEADS = 4
NA_KH = 8
NA_KW = 16
DIFF_HEADS = 4
DIFF_QK_DIM = 32
DIFF_V_DIM = 64
HG_HEADS = 4
HG_DK = 64
HG_DV = 64
BRANCH_W = 256
N_BRANCH = 4
N_GROUPS = 4
EXPERTS_PER_GROUP = 4
N_EXPERTS = N_GROUPS * EXPERTS_PER_GROUP
D_EXPERT = 512
ROPE_THETA = 10000.0
LN_EPS = 1e-5
RMS_EPS = 1e-6
F_FLOOR = 1e-30
NEG_BIG = -1e30
LOG2E = math.log2(math.e)

C_GQ, C_GK, C_GV = 0, 256, 384
C_NQ, C_NK, C_NV = 512, 768, 1024
C_DQ, C_DK, C_DV = 1280, 1536, 1792
C_HG = 2048
D_HEADS_IN = 3328
HG_COLS = 1280

VMEM_LIMIT = 56 * 1024 * 1024


def _cparams(sem):
    return pltpu.CompilerParams(dimension_semantics=sem, vmem_limit_bytes=VMEM_LIMIT)


def _dot(a, b):
    return jnp.dot(a, b, preferred_element_type=F32)


def _dot_nt(a, b):
    return lax.dot_general(a, b, (((1,), (1,)), ((), ())), preferred_element_type=F32)


def _split_dot(x, j):
    hi = x.astype(BF16)
    lo = (x - hi.astype(F32)).astype(BF16)
    return _dot(hi, j) + _dot(lo, j)


def _sigmoid(z):
    return 1.0 / (1.0 + jnp.exp(-z))


def _ln_plain(x):
    mu = jnp.mean(x, axis=-1, keepdims=True)
    xc = x - mu
    var = jnp.mean(xc * xc, axis=-1, keepdims=True)
    return xc * lax.rsqrt(var + LN_EPS)


def _block_ones(n, blk):
    idx = np.arange(n) // blk
    return jnp.asarray((idx[:, None] == idx[None, :]).astype(np.float32), dtype=BF16)


def _ada_kernel(c_ref, w_ref, b_ref, o_ref):
    c = c_ref[...]
    s = (c * _sigmoid(c)).astype(BF16)
    o_ref[0] = _dot(s, w_ref[0].astype(BF16)) + b_ref[0]


def _ada_mods(cond, w_ada, b_ada):
    depth, d, n = w_ada.shape
    rows = cond.shape[0]
    tn = 1536
    return pl.pallas_call(
        _ada_kernel,
        grid=(depth, n // tn),
        in_specs=[
            pl.BlockSpec((rows, d), lambda l, j: (0, 0)),
            pl.BlockSpec((1, d, tn), lambda l, j: (l, 0, j)),
            pl.BlockSpec((1, 1, tn), lambda l, j: (l, 0, j)),
        ],
        out_specs=pl.BlockSpec((1, rows, tn), lambda l, j: (l, 0, j)),
        out_shape=jax.ShapeDtypeStruct((depth, rows, n), F32),
        compiler_params=_cparams(("parallel", "parallel")),
        name="ada_mods",
    )(cond, w_ada, b_ada.reshape(depth, 1, n))


def _rope_tables(t_len, d, reps):
    quarter = d // 4
    t = np.arange(t_len)
    inv = ROPE_THETA ** (-np.arange(quarter, dtype=np.float64) / quarter)
    ang_r = (t // GRID_W).astype(np.float64)[:, None] * inv
    ang_c = (t % GRID_W).astype(np.float64)[:, None] * inv
    ang = np.concatenate([ang_r, ang_r, ang_c, ang_c], axis=-1)
    cos, sin = np.cos(ang), np.sin(ang)
    first = (np.arange(d) % (2 * quarter)) < quarter
    sin_a = np.where(first, -sin, 0.0)
    sin_b = np.where(first, 0.0, sin)
    tile = lambda a: jnp.asarray(np.tile(a, (1, reps)), dtype=F32)
    return tile(cos), tile(sin_a), tile(sin_b)


def _rope(x, cos, sin_a, sin_b, quarter):
    w = x.shape[-1]
    return x * cos + pltpu.roll(x, w - quarter, 1) * sin_a + pltpu.roll(x, quarter, 1) * sin_b


def _seg_rms(x, j):
    ss = _split_dot(x * x, j)
    return x * lax.rsqrt(ss * (1.0 / HEAD_DIM) + RMS_EPS)


def _inproj_kernel(*refs, rope):
    if rope:
        (x_ref, sh_ref, sc_ref, w_ref, qn_ref, kn_ref, j_ref,
         c64_ref, a64_ref, b64_ref, c32_ref, a32_ref, b32_ref,
         qa_ref, ka_ref, va_ref, qnb_ref, knb_ref, vnb_ref, qd_ref, kd_ref, vd_ref, hg_ref) = refs
    else:
        (x_ref, sh_ref, sc_ref, w_ref, qn_ref, kn_ref, j_ref,
         qa_ref, ka_ref, va_ref, qnb_ref, knb_ref, vnb_ref, qd_ref, kd_ref, vd_ref, hg_ref) = refs
    h = (_ln_plain(x_ref[0]) * (1.0 + sc_ref[0]) + sh_ref[0]).astype(BF16)
    proj = lambda c0, n: _dot(h, w_ref[:, c0:c0 + n])

    gq = _seg_rms(proj(C_GQ, 256), j_ref[...]) * qn_ref[...]
    gk = _seg_rms(proj(C_GK, 128), j_ref[0:128, 0:128]) * kn_ref[...]
    dq = proj(C_DQ, 256)
    dk = proj(C_DK, 256)
    if rope:
        c64, a64, b64 = c64_ref[...], a64_ref[...], b64_ref[...]
        c32, a32, b32 = c32_ref[...], a32_ref[...], b32_ref[...]
        gq = _rope(gq, c64, a64, b64, HEAD_DIM // 4)
        gk = _rope(gk, c64[:, 0:128], a64[:, 0:128], b64[:, 0:128], HEAD_DIM // 4)
        dq = _rope(dq, c32, a32, b32, DIFF_QK_DIM // 4)
        dk = _rope(dk, c32, a32, b32, DIFF_QK_DIM // 4)
    qa_ref[0] = (gq * (LOG2E * HEAD_DIM ** -0.5)).astype(qa_ref.dtype)
    ka_ref[0] = gk.astype(ka_ref.dtype)
    va_ref[0] = proj(C_GV, 128).astype(va_ref.dtype)
    qnb_ref[0] = (proj(C_NQ, 256) * (LOG2E * HEAD_DIM ** -0.5)).astype(qnb_ref.dtype)
    knb_ref[0] = proj(C_NK, 256).astype(knb_ref.dtype)
    vnb_ref[0] = proj(C_NV, 256).astype(vnb_ref.dtype)
    qd_ref[0] = (dq * (LOG2E * DIFF_QK_DIM ** -0.5)).astype(qd_ref.dtype)
    kd_ref[0] = dk.astype(kd_ref.dtype)
    vd_ref[0] = proj(C_DV, 256).astype(vd_ref.dtype)
    for s in range(HG_COLS // 256):
        hg_ref[0, :, s * 256:(s + 1) * 256] = proj(C_HG + s * 256, 256)


def _inproj(x, shift, scale, w_heads, qn, kn, rope, kv_dtype, tm=256):
    bsz, t_len, d = x.shape
    nb = shift.shape[0]
    mod_map = (lambda b, i: (b, 0, 0)) if nb > 1 else (lambda b, i: (0, 0, 0))
    tok = lambda n: pl.BlockSpec((1, tm, n), lambda b, i: (b, i, 0))
    const = lambda shape: pl.BlockSpec(shape, lambda b, i: (0,) * len(shape))
    in_specs = [tok(d), pl.BlockSpec((1, 1, d), mod_map), pl.BlockSpec((1, 1, d), mod_map),
                const((d, D_HEADS_IN)), const((1, 256)), const((1, 128)), const((256, 256))]
    args = [x, shift, scale, w_heads, jnp.tile(qn, GQA_Q_HEADS)[None], jnp.tile(kn, GQA_KV_HEADS)[None],
            _block_ones(256, HEAD_DIM)]
    if rope:
        in_specs += [pl.BlockSpec((tm, 256), lambda b, i: (i, 0))] * 6
        args += list(_rope_tables(t_len, HEAD_DIM, 4)) + list(_rope_tables(t_len, DIFF_QK_DIM, 8))
    widths = (256, 128, 128, 256, 256, 256, 256, 256, 256)
    dtypes = (BF16, kv_dtype, kv_dtype, BF16, kv_dtype, kv_dtype, BF16, kv_dtype, kv_dtype)
    out_shape = [jax.ShapeDtypeStruct((bsz, t_len, n), dt) for n, dt in zip(widths, dtypes)]
    out_shape.append(jax.ShapeDtypeStruct((bsz, t_len, HG_COLS), F32))
    out_specs = [tok(n) for n in widths] + [tok(HG_COLS)]
    return pl.pallas_call(
        functools.partial(_inproj_kernel, rope=rope),
        grid=(bsz, t_len // tm),
        in_specs=in_specs,
        out_specs=out_specs,
        out_shape=out_shape,
        compiler_params=_cparams(("parallel", "parallel")),
        name="in_proj_rope" if rope else "in_proj",
    )(*args)


def _softmax_p(s):
    return jnp.exp2(s - jnp.max(s, axis=-1, keepdims=True)).astype(BF16)


def _pv_norm(p, v_ext):
    dv = v_ext.shape[-1] // 2
    o = _dot(p, v_ext)
    return o[:, :dv] * (1.0 / o[:, dv:])


def _attn_kernel(q_ref, kt_ref, v_ref, o_ref, *, hq, group, d):
    q = q_ref[0]
    tq = q.shape[0]
    outs = []
    for g in range(hq // group):
        qs = jnp.concatenate([q[:, h * d:(h + 1) * d] for h in range(g * group, (g + 1) * group)], axis=0)
        o = _pv_norm(_softmax_p(_dot(qs, kt_ref[0, g])), v_ref[0, g])
        outs += [o[j * tq:(j + 1) * tq] for j in range(group)]
    o_ref[0] = jnp.concatenate(outs, axis=-1).astype(o_ref.dtype)


def _attention(q, kt, v_ext, hq, tq):
    bsz, t_len, qw = q.shape
    _, hkv, d, l_len = kt.shape
    dv = v_ext.shape[-1] // 2
    return pl.pallas_call(
        functools.partial(_attn_kernel, hq=hq, group=hq // hkv, d=d),
        grid=(bsz, t_len // tq),
        in_specs=[
            pl.BlockSpec((1, tq, qw), lambda b, i: (b, i, 0)),
            pl.BlockSpec((1, hkv, d, l_len), lambda b, i: (b, 0, 0, 0)),
            pl.BlockSpec((1, hkv, l_len, 2 * dv), lambda b, i: (b, 0, 0, 0)),
        ],
        out_specs=pl.BlockSpec((1, tq, hq * dv), lambda b, i: (b, i, 0)),
        out_shape=jax.ShapeDtypeStruct((bsz, t_len, hq * dv), BF16),
        compiler_params=_cparams(("parallel", "parallel")),
        name="attention",
    )(q, kt, v_ext)


def _diff_attn_kernel(q_ref, kt_ref, v_ref, par_ref, o_ref):
    q = q_ref[0]
    tq = q.shape[0]
    lam = par_ref[0:1, :]
    gain = par_ref[1:2, :]
    dq = DIFF_QK_DIM
    outs = []
    for h in range(DIFF_HEADS):
        p1 = _softmax_p(_dot(q[:, 2 * h * dq:(2 * h + 1) * dq], kt_ref[0, 2 * h]))
        p2 = _softmax_p(_dot(q[:, (2 * h + 1) * dq:(2 * h + 2) * dq], kt_ref[0, 2 * h + 1]))
        o12 = _pv_norm(jnp.concatenate([p1, p2], axis=0), v_ref[0, h])
        o = o12[:tq] - lam * o12[tq:]
        ms = jnp.mean(o * o, axis=-1, keepdims=True)
        outs.append(o * lax.rsqrt(ms + RMS_EPS) * gain)
    o_ref[0] = jnp.concatenate(outs, axis=-1).astype(o_ref.dtype)


def _diff_attention(q, kt, v_ext, par, tq):
    bsz, t_len, qw = q.shape
    l_len = kt.shape[-1]
    return pl.pallas_call(
        _diff_attn_kernel,
        grid=(bsz, t_len // tq),
        in_specs=[
            pl.BlockSpec((1, tq, qw), lambda b, i: (b, i, 0)),
            pl.BlockSpec((1, 2 * DIFF_HEADS, DIFF_QK_DIM, l_len), lambda b, i: (b, 0, 0, 0)),
            pl.BlockSpec((1, DIFF_HEADS, l_len, 2 * DIFF_V_DIM), lambda b, i: (b, 0, 0, 0)),
            pl.BlockSpec((8, DIFF_V_DIM), lambda b, i: (0, 0)),
        ],
        out_specs=pl.BlockSpec((1, tq, DIFF_HEADS * DIFF_V_DIM), lambda b, i: (b, i, 0)),
        out_shape=jax.ShapeDtypeStruct((bsz, t_len, DIFF_HEADS * DIFF_V_DIM), BF16),
        compiler_params=_cparams(("parallel", "parallel")),
        name="diff_attention",
    )(q, kt, v_ext, par)


def _na_bias_table(rpb):
    shift = np.arange(NA_KH)[:, None]
    win_row = np.arange(NA_KH)[None, :]
    row_off = win_row - shift + (NA_KH - 1)
    col = np.arange(GRID_W)
    col_start = np.clip(col - NA_KW // 2, 0, GRID_W - NA_KW)
    key_col = np.arange(GRID_W)[None, :]
    valid = (key_col >= col_start[:, None]) & (key_col < col_start[:, None] + NA_KW)
    col_off = key_col - col[:, None] + (NA_KW - 1)
    row_sel = (row_off[:, :, None] == np.arange(2 * NA_KH - 1)).astype(np.float32)
    col_sel = ((col_off[:, :, None] == np.arange(2 * NA_KW - 1)) & valid[:, :, None]).astype(np.float32)
    tbl = jnp.einsum("sia,hab,cxb->shcix", row_sel, rpb.astype(F32) * LOG2E, col_sel,
                     precision=lax.Precision.HIGHEST)
    tbl = jnp.where(valid[None, None, :, None, :], tbl, NEG_BIG)
    return tbl.reshape(NA_KH, rpb.shape[0], GRID_W, NA_KH * GRID_W)


def _na_kernel(q_ref, k_ref, v_ref, kc_ref, vc_ref, bias_ref, o_ref, *, rows_per_step, rows):
    i = pl.program_id(1)
    n_win = NA_KH * GRID_W
    for rr in range(rows_per_step):
        r = i * rows_per_step + rr
        r_start = jnp.clip(r - NA_KH // 2, 0, rows - NA_KH)
        shift = r - r_start
        start = pl.multiple_of(r_start * GRID_W, GRID_W)
        q = q_ref[0, rr * GRID_W:(rr + 1) * GRID_W, :]
        outs = []
        for h in range(NA_HEADS):
            qh = q[:, h * HEAD_DIM:(h + 1) * HEAD_DIM]
            s_loc = _dot_nt(qh, k_ref[0, h, pl.ds(start, n_win), :]) + bias_ref[shift, h]
            s_ctx = _dot_nt(qh, kc_ref[0, h])
            m = jnp.maximum(jnp.max(s_loc, axis=-1, keepdims=True), jnp.max(s_ctx, axis=-1, keepdims=True))
            p_loc = jnp.exp2(s_loc - m).astype(BF16)
            p_ctx = jnp.exp2(s_ctx - m).astype(BF16)
            o = _dot(p_loc, v_ref[0, h, pl.ds(start, n_win), :]) + _dot(p_ctx, vc_ref[0, h])
            outs.append(o[:, :HEAD_DIM] * (1.0 / o[:, HEAD_DIM:]))
        o_ref[0, rr * GRID_W:(rr + 1) * GRID_W, :] = jnp.concatenate(outs, axis=-1).astype(o_ref.dtype)


def _na_attention(q, k, v_ext, kc, vc_ext, bias, rows_per_step=8):
    bsz, n, qw = q.shape
    rows = n // GRID_W
    assert rows >= NA_KH and rows % rows_per_step == 0
    l_ctx = kc.shape[2]
    tq = rows_per_step * GRID_W
    full = lambda shape: pl.BlockSpec((1,) + shape, lambda b, i: (b,) + (0,) * len(shape))
    return pl.pallas_call(
        functools.partial(_na_kernel, rows_per_step=rows_per_step, rows=rows),
        grid=(bsz, rows // rows_per_step),
        in_specs=[
            pl.BlockSpec((1, tq, qw), lambda b, i: (b, i, 0)),
            full((NA_HEADS, n, HEAD_DIM)), full((NA_HEADS, n, 2 * HEAD_DIM)),
            full((NA_HEADS, l_ctx, HEAD_DIM)), full((NA_HEADS, l_ctx, 2 * HEAD_DIM)),
            pl.BlockSpec(bias.shape, lambda b, i: (0, 0, 0, 0)),
        ],
        out_specs=pl.BlockSpec((1, tq, qw), lambda b, i: (b, i, 0)),
        out_shape=jax.ShapeDtypeStruct((bsz, n, qw), BF16),
        compiler_params=_cparams(("parallel", "parallel")),
        name="na_attention",
    )(q, k, v_ext, kc, vc_ext, bias)


HG_SUB = 16
HG_W = HG_HEADS * HG_DK


def _hgrn_direction(hg_ref, vt_ref, lb, tri, jv, bd_mask, st_ref, z_col, reverse, tc):
    c = HG_SUB
    hq = hg_ref[0, :, 0:HG_W]
    q = hq * _sigmoid(hq)
    z = hg_ref[0, :, z_col:z_col + HG_W]
    v = hg_ref[0, :, 3 * HG_W:4 * HG_W]
    f = lb + (1.0 - lb) * _sigmoid(z)
    log_f = jnp.log(jnp.maximum(f, F_FLOOR))
    kk = (1.0 - lb) * _sigmoid(-z)
    b = jnp.dot(tri, log_f, preferred_element_type=F32, precision=lax.Precision.HIGHEST)
    vt = vt_ref[0]
    row = lax.broadcasted_iota(jnp.int32, (c, HG_W), 0)
    order = range(tc // c - 1, -1, -1) if reverse else range(tc // c)
    outs = [None] * (tc // c)
    for j in order:
        sl = slice(j * c, (j + 1) * c)
        bj, qj, kj, vj = b[sl], q[sl], kk[sl], v[sl]
        st = st_ref[...]
        o = _dot_nt((qj * jnp.exp(bj)).astype(BF16), st.astype(BF16))
        prods = []
        for s in range(c):
            keep = (row <= s) if reverse else (row >= s)
            decay = jnp.where(keep, jnp.exp(bj - bj[s:s + 1]), 0.0)
            prods.append(qj * decay * kj[s:s + 1])
        scores = _split_dot(jnp.concatenate(prods, axis=0), jv)
        for s in range(c):
            o = o + scores[s * c:(s + 1) * c] * vj[s:s + 1]
        b_end = bj[0:1] if reverse else bj[c - 1:c]
        kd = (kj * jnp.exp(b_end - bj)).astype(BF16)
        upd = _dot(vt[:, sl].astype(BF16), kd)
        st_ref[...] = jnp.exp(b_end) * st + jnp.where(bd_mask, upd, 0.0)
        outs[j] = o
    return jnp.concatenate(outs, axis=0)


def _hgrn_kernel(hg_ref, vt_ref, lb_ref, gn_ref, s0_ref, tri_ref, jv_ref, o_ref, sfin_ref, st_scr, of_scr, *, tc, nc):
    p = pl.program_id(1)
    i = pl.program_id(2)
    r_i = lax.broadcasted_iota(jnp.int32, (HG_W, HG_W), 0) // HG_DV
    c_i = lax.broadcasted_iota(jnp.int32, (HG_W, HG_W), 1) // HG_DK
    bd_mask = r_i == c_i
    jv = jv_ref[...]

    @pl.when(i == 0)
    def _():
        st_scr[...] = s0_ref[0, p]

    @pl.when(p == 0)
    def _():
        o = _hgrn_direction(hg_ref, vt_ref, lb_ref[0:1, :], tri_ref[0], jv, bd_mask, st_scr, HG_W, False, tc)
        of_scr[pl.ds(pl.multiple_of(i * tc, tc), tc), :] = o

    @pl.when(p == 1)
    def _():
        o_b = _hgrn_direction(hg_ref, vt_ref, lb_ref[1:2, :], tri_ref[1], jv, bd_mask, st_scr, 2 * HG_W, True, tc)
        o = of_scr[pl.ds(pl.multiple_of((nc - 1 - i) * tc, tc), tc), :] + o_b
        ms = _split_dot(o * o, jv) * (1.0 / HG_DV)
        g = hg_ref[0, :, 4 * HG_W:5 * HG_W]
        o_ref[0] = (o * lax.rsqrt(ms + RMS_EPS) * gn_ref[...] * (g * _sigmoid(g))).astype(o_ref.dtype)

    @pl.when(i == nc - 1)
    def _():
        sfin_ref[0, p] = st_scr[...]


def _hgrn(hg, lb2, gn, s0t, tc=256):
    bsz, t_len, _ = hg.shape
    nc = t_len // tc
    vt = jnp.swapaxes(hg[:, :, 3 * HG_W:4 * HG_W], 1, 2)
    blk = np.arange(tc) // HG_SUB
    same = blk[:, None] == blk[None, :]
    lower = np.arange(tc)[:, None] >= np.arange(tc)[None, :]
    tri = jnp.asarray(np.stack([same & lower, same & ~lower | same & np.eye(tc, dtype=bool)]).astype(np.float32))
    chunk = lambda b, p, i: (b, i + p * (nc - 1 - 2 * i), 0)
    o, sfin = pl.pallas_call(
        functools.partial(_hgrn_kernel, tc=tc, nc=nc),
        grid=(bsz, 2, nc),
        in_specs=[
            pl.BlockSpec((1, tc, HG_COLS), chunk),
            pl.BlockSpec((1, HG_W, tc), lambda b, p, i: (b, 0, i + p * (nc - 1 - 2 * i))),
            pl.BlockSpec((2, HG_W), lambda b, p, i: (0, 0)),
            pl.BlockSpec((1, HG_W), lambda b, p, i: (0, 0)),
            pl.BlockSpec((1, 2, HG_W, HG_W), lambda b, p, i: (b, 0, 0, 0)),
            pl.BlockSpec((2, tc, tc), lambda b, p, i: (0, 0, 0)),
            pl.BlockSpec((HG_W, HG_W), lambda b, p, i: (0, 0)),
        ],
        out_specs=[
            pl.BlockSpec((1, tc, HG_W), lambda b, p, i: (b, nc - 1 - i * p, 0)),
            pl.BlockSpec((1, 2, HG_W, HG_W), lambda b, p, i: (b, 0, 0, 0)),
        ],
        out_shape=[jax.ShapeDtypeStruct((bsz, t_len, HG_W), BF16),
                   jax.ShapeDtypeStruct((bsz, 2, HG_W, HG_W), F32)],
        scratch_shapes=[pltpu.VMEM((HG_W, HG_W), F32), pltpu.VMEM((t_len, HG_W), F32)],
        compiler_params=_cparams(("parallel", "arbitrary", "arbitrary")),
        name="hgrn",
    )(hg, vt, lb2, jnp.tile(gn, HG_HEADS)[None], s0t, tri, _block_ones(HG_W, HG_DV))
    return o, sfin


def _state_to_blockdiag_t(s0):
    bsz = s0.shape[0]
    s0t = jnp.swapaxes(s0, -1, -2)
    eye = jnp.eye(HG_HEADS, dtype=s0.dtype)
    return jnp.einsum("bdhvk,hg->bdhvgk", s0t, eye).reshape(bsz, 2, HG_W, HG_W)


def _blockdiag_t_to_state(st):
    bsz = st.shape[0]
    s6 = st.reshape(bsz, 2, HG_HEADS, HG_DV, HG_HEADS, HG_DK)
    diag = jnp.stack([s6[:, :, h, :, h, :] for h in range(HG_HEADS)], axis=2)
    return jnp.swapaxes(diag, -1, -2)


def _post_kernel(x_ref, oa_ref, on_ref, od_ref, oh_ref, sh_ref, sc_ref, g1_ref, wg_ref, wb_ref, wo_ref,
                 lg_ref, lb_ref, o_ref, *, alpha):
    x = x_ref[0]
    h = (_ln_plain(x) * (1.0 + sc_ref[0]) + sh_ref[0]).astype(BF16)
    mixed = None
    for n, br_ref in enumerate((oa_ref, on_ref, od_ref, oh_ref)):
        gate = _sigmoid(_dot(h, wg_ref[:, n * D_MODEL:(n + 1) * D_MODEL]))
        term = gate * _dot(br_ref[0], wb_ref[n])
        mixed = term if mixed is None else mixed + term
    y = _dot(mixed.astype(BF16), wo_ref[...])
    o_ref[0] = _ln_plain(alpha * x + g1_ref[0] * y) * lg_ref[...] + lb_ref[...]


def _post_mixer(x, branches, shift, scale, gate, w_gates, w_branch, w_out, ln_g, ln_b, alpha, tm=256):
    bsz, t_len, d = x.shape
    nb = shift.shape[0]
    mod_map = (lambda b, i: (b, 0, 0)) if nb > 1 else (lambda b, i: (0, 0, 0))
    tok = lambda n: pl.BlockSpec((1, tm, n), lambda b, i: (b, i, 0))
    const = lambda shape: pl.BlockSpec(shape, lambda b, i: (0,) * len(shape))
    mod = pl.BlockSpec((1, 1, d), mod_map)
    return pl.pallas_call(
        functools.partial(_post_kernel, alpha=alpha),
        grid=(bsz, t_len // tm),
        in_specs=[tok(d)] + [tok(BRANCH_W)] * 4 + [mod, mod, mod,
                  const((d, N_BRANCH * d)), const((N_BRANCH, BRANCH_W, d)), const((d, d)),
                  const((1, d)), const((1, d))],
        out_specs=tok(d),
        out_shape=jax.ShapeDtypeStruct((bsz, t_len, d), F32),
        compiler_params=_cparams(("parallel", "parallel")),
        name="post_mixer",
    )(x, *branches, shift, scale, gate, w_gates, w_branch, w_out, ln_g[None], ln_b[None])


ROUTER_LANES = 128
MOE_BT = 1024
MOE_ROWS = 128
MOE_STEPS = N_EXPERTS // 2


def _router_weights(logits):
    lane = lax.broadcasted_iota(jnp.int32, logits.shape, 1)
    first_at = lambda mask: jnp.min(jnp.where(mask, lane, ROUTER_LANES), axis=-1, keepdims=True)
    is_g = lane < N_GROUPS
    gl = jnp.where(is_g, logits, NEG_BIG)
    g_max = jnp.max(gl, axis=-1, keepdims=True)
    g_idx = first_at(is_g & (gl == g_max))
    g_prob = 1.0 / jnp.sum(jnp.where(is_g, jnp.exp(gl - g_max), 0.0), axis=-1, keepdims=True)
    lo = N_GROUPS + g_idx * EXPERTS_PER_GROUP
    in_grp = (lane >= lo) & (lane < lo + EXPERTS_PER_GROUP)
    el = jnp.where(in_grp, logits, NEG_BIG)
    v1 = jnp.max(el, axis=-1, keepdims=True)
    i1 = first_at(in_grp & (el == v1))
    rest = in_grp & (lane != i1)
    el2 = jnp.where(rest, logits, NEG_BIG)
    v2 = jnp.max(el2, axis=-1, keepdims=True)
    i2 = first_at(rest & (el2 == v2))
    e2 = jnp.exp(v2 - v1)
    w1 = g_prob / (1.0 + e2)
    w2 = g_prob * e2 / (1.0 + e2)
    return g_idx, jnp.where(lane == i1, w1, jnp.where(lane == i2, w2, 0.0))


def _moe_kernel(x_ref, sh_ref, sc_ref, g2_ref, wr_ref, br_ref, lt_ref, wgu_ref, wd_ref, lg_ref, lb_ref, o_ref,
                xs_scr, pt_scr, cws_scr, ys_scr, seg_ref, *, alpha):
    s = pl.program_id(2)
    bt = xs_scr.shape[0]
    rows = MOE_ROWS

    @pl.when(s == 0)
    def _():
        xm = _ln_plain(x_ref[0]) * (1.0 + sc_ref[0]) + sh_ref[0]
        x_hi = xm.astype(BF16)
        x_lo = (xm - x_hi.astype(F32)).astype(BF16)
        logits = _dot(x_hi, wr_ref[0]) + _dot(x_lo, wr_ref[0]) + _dot(x_hi, wr_ref[1]) + br_ref[...]
        g_idx, cw = _router_weights(logits)
        lane = lax.broadcasted_iota(jnp.int32, (bt, ROUTER_LANES), 1)
        in_g = lane == g_idx
        onehot = jnp.where(in_g, 1.0, 0.0)
        before = _dot(lt_ref[...], onehot.astype(BF16))
        count = jnp.sum(onehot, axis=0, keepdims=True)
        lane1 = lax.broadcasted_iota(jnp.int32, (1, ROUTER_LANES), 1)
        start = jnp.zeros((1, ROUTER_LANES), F32)
        for k in range(N_GROUPS - 1):
            start = start + jnp.where(lane1 > k, count[:, k:k + 1], 0.0)
        dest = jnp.sum(jnp.where(in_g, start + before, 0.0), axis=-1, keepdims=True)
        dest_row = jnp.transpose(jnp.broadcast_to(dest, (bt, ROUTER_LANES)))[0:1, :]
        pt_scr[...] = jnp.where(dest.astype(jnp.int32) == lax.broadcasted_iota(jnp.int32, (bt, bt), 1),
                                1.0, 0.0).astype(BF16)
        perm = jnp.where(dest_row.astype(jnp.int32) == lax.broadcasted_iota(jnp.int32, (bt, bt), 0),
                         1.0, 0.0).astype(BF16)
        xs_scr[...] = _dot(perm, x_hi).astype(BF16)
        c_hi = cw.astype(BF16)
        c_lo = (cw - c_hi.astype(F32)).astype(BF16)
        cws_scr[...] = _dot(perm, c_hi) + _dot(perm, c_lo)
        ys_scr[...] = jnp.zeros_like(ys_scr)
        for g in range(N_GROUPS):
            seg_ref[g] = jnp.sum(jnp.where(lane1 == g, start, 0.0)).astype(jnp.int32)
            seg_ref[N_GROUPS + g] = jnp.sum(jnp.where(lane1 == g, start + count, 0.0)).astype(jnp.int32)

    g = s // (MOE_STEPS // N_GROUPS)
    seg_lo = seg_ref[g]
    seg_hi = seg_ref[N_GROUPS + g]
    t_lo = seg_lo // rows
    t_hi = jnp.where(seg_hi > seg_lo, (seg_hi + rows - 1) // rows, t_lo)
    lane_r = lax.broadcasted_iota(jnp.int32, (rows, ROUTER_LANES), 1)
    f = D_EXPERT

    def tile(t, carry):
        r0 = pl.multiple_of(t * rows, rows)
        gu = _dot(xs_scr[pl.ds(r0, rows), :], wgu_ref[0])
        cwt = cws_scr[pl.ds(r0, rows), :]
        w_a = jnp.sum(jnp.where(lane_r == N_GROUPS + 2 * s, cwt, 0.0), axis=-1, keepdims=True)
        w_b = jnp.sum(jnp.where(lane_r == N_GROUPS + 2 * s + 1, cwt, 0.0), axis=-1, keepdims=True)
        h_a = gu[:, 0:f] * _sigmoid(gu[:, 0:f]) * gu[:, 2 * f:3 * f] * w_a
        h_b = gu[:, f:2 * f] * _sigmoid(gu[:, f:2 * f]) * gu[:, 3 * f:4 * f] * w_b
        hid = jnp.concatenate([h_a, h_b], axis=-1).astype(BF16)
        ys_scr[pl.ds(r0, rows), :] += _dot(hid, wd_ref[0])
        return carry

    lax.fori_loop(t_lo, t_hi, tile, 0)

    @pl.when(s == MOE_STEPS - 1)
    def _():
        ys = ys_scr[...]
        hi = ys.astype(BF16)
        lo = (ys - hi.astype(F32)).astype(BF16)
        pt = pt_scr[...]
        y = _dot(pt, hi) + _dot(pt, lo)
        o_ref[0] = _ln_plain(alpha * x_ref[0] + g2_ref[0] * y) * lg_ref[...] + lb_ref[...]


def _moe(x, shift, scale, gate, w_route, b_route, w_gu, w_dn, ln_g, ln_b, alpha):
    bsz, t_len, d = x.shape
    bt = min(MOE_BT, t_len)
    assert t_len % bt == 0 and bt % MOE_ROWS == 0
    nb = shift.shape[0]
    mod_map = (lambda b, i, e: (b, 0, 0)) if nb > 1 else (lambda b, i, e: (0, 0, 0))
    tok = pl.BlockSpec((1, bt, d), lambda b, i, e: (b, i, 0))
    const = lambda shape: pl.BlockSpec(shape, lambda b, i, e: (0,) * len(shape))
    mod = pl.BlockSpec((1, 1, d), mod_map)
    pair = lambda shape: pl.BlockSpec((1,) + shape, lambda b, i, e: (e, 0, 0))
    lower = jnp.asarray(np.tril(np.ones((bt, bt), np.float32), -1), dtype=BF16)
    return pl.pallas_call(
        functools.partial(_moe_kernel, alpha=alpha),
        grid=(bsz, t_len // bt, MOE_STEPS),
        in_specs=[tok, mod, mod, mod, const((2, d, ROUTER_LANES)), const((1, ROUTER_LANES)), const((bt, bt)),
                  pair((d, 4 * D_EXPERT)), pair((2 * D_EXPERT, d)), const((1, d)), const((1, d))],
        out_specs=tok,
        out_shape=jax.ShapeDtypeStruct((bsz, t_len, d), F32),
        scratch_shapes=[pltpu.VMEM((bt, d), BF16), pltpu.VMEM((bt, bt), BF16), pltpu.VMEM((bt, ROUTER_LANES), F32),
                        pltpu.VMEM((bt, d), F32), pltpu.SMEM((2 * N_GROUPS,), jnp.int32)],
        compiler_params=_cparams(("parallel", "parallel", "arbitrary")),
        name="moe",
    )(x, shift, scale, gate, w_route, b_route, lower, w_gu, w_dn, ln_g[None], ln_b[None])


def _heads_t(a, heads, d):
    bsz, l_len, _ = a.shape
    return jnp.transpose(a.reshape(bsz, l_len, heads, d), (0, 2, 3, 1))


def _heads_rows(a, heads, d):
    bsz, l_len, _ = a.shape
    return jnp.transpose(a.reshape(bsz, l_len, heads, d), (0, 2, 1, 3))


def _heads_rows_ext(a, heads, d):
    r = _heads_rows(a, heads, d)
    return jnp.concatenate([r, jnp.ones_like(r)], axis=-1)


def _layer(x, mods, lp, ctx, alpha):
    bsz, t_len, _ = x.shape
    latent = ctx is not None
    shift1, scale1, gate1, shift2, scale2, gate2 = mods
    kv_dtype = BF16 if latent else F32
    qa, ka, va, qn, kn, vn, qd, kd, vd, hg = _inproj(
        x, shift1, scale1, lp["w_heads"], lp["gqa_qn"], lp["gqa_kn"], latent, kv_dtype)
    flat = lambda a: a.reshape(a.shape[0], a.shape[1], -1).astype(BF16)
    if latent:
        c_gk, c_gv, c_nk, c_nv, c_dk, c_dv, c_state = ctx
        ka_all = jnp.concatenate([flat(c_gk), ka], axis=1)
        va_all = jnp.concatenate([flat(c_gv), va], axis=1)
        kd_all = jnp.concatenate([flat(c_dk), kd], axis=1)
        vd_all = jnp.concatenate([flat(c_dv), vd], axis=1)
        s0t = _state_to_blockdiag_t(c_state.astype(F32))
        tq = 256
    else:
        ka_all, va_all, kd_all, vd_all = flat(ka), flat(va), flat(kd), flat(vd)
        s0t = jnp.zeros((bsz, 2, HG_W, HG_W), F32)
        tq = t_len
    o_a = _attention(qa, _heads_t(ka_all, GQA_KV_HEADS, HEAD_DIM), _heads_rows_ext(va_all, GQA_KV_HEADS, HEAD_DIM),
                     GQA_Q_HEADS, tq)
    if latent:
        o_n = _na_attention(qn, _heads_rows(kn, NA_HEADS, HEAD_DIM), _heads_rows_ext(vn, NA_HEADS, HEAD_DIM),
                            _heads_rows(flat(c_nk), NA_HEADS, HEAD_DIM), _heads_rows_ext(flat(c_nv), NA_HEADS, HEAD_DIM),
                            lp["na_bias"])
    else:
        o_n = _attention(qn, _heads_t(flat(kn), NA_HEADS, HEAD_DIM), _heads_rows_ext(flat(vn), NA_HEADS, HEAD_DIM),
                         NA_HEADS, tq)
    o_d = _diff_attention(qd, _heads_t(kd_all, 2 * DIFF_HEADS, DIFF_QK_DIM),
                          _heads_rows_ext(vd_all, DIFF_HEADS, DIFF_V_DIM), lp["diff_par"], tq)
    o_h, sfin = _hgrn(hg, lp["hg_lb"], lp["hg_norm"], s0t)
    x1 = _post_mixer(x, (o_a, o_n, o_d, o_h), shift1, scale1, gate1, lp["w_gates"], lp["w_branch"], lp["w_out"],
                     lp["ln_mix_g"], lp["ln_mix_b"], alpha)
    x1r = x1 if latent else x1.reshape(1, bsz * t_len, D_MODEL)
    y = _moe(x1r, shift2, scale2, gate2, lp["w_route"], lp["b_route"], lp["w_gu"], lp["w_dn"],
             lp["ln_ffn_g"], lp["ln_ffn_b"], alpha).reshape(bsz, t_len, D_MODEL)
    caches = None
    if not latent:
        caches = (ka.reshape(bsz, t_len, GQA_KV_HEADS, HEAD_DIM), va.reshape(bsz, t_len, GQA_KV_HEADS, HEAD_DIM),
                  kn.reshape(bsz, t_len, NA_HEADS, HEAD_DIM), vn.reshape(bsz, t_len, NA_HEADS, HEAD_DIM),
                  kd.reshape(bsz, t_len, DIFF_HEADS, 2, DIFF_QK_DIM), vd.reshape(bsz, t_len, DIFF_HEADS, DIFF_V_DIM),
                  _blockdiag_t_to_state(sfin))
    return y, caches


def kernel(x_prompt, x_sample, cache_gqa_k, cache_gqa_v, cache_na_k, cache_na_v, cache_diff_k, cache_diff_v, state_hgrn, c, c_ctx, w_ada, b_ada, w_in, gqa_q_norm, gqa_k_norm, na_rpb, diff_lambda, diff_subln, hgrn_lb, hgrn_norm, w_branch, w_out, ln_mix_g, ln_mix_b, ln_ffn_g, ln_ffn_b, w_group, b_group, w_router, b_router, w_gate, w_up, w_down):
    depth = w_in.shape[0]
    d = D_MODEL
    alpha = (2 * depth) ** 0.25
    dec_b = c.shape[0]

    lb = jax.nn.softmax(hgrn_lb.astype(F32), axis=1)
    lb = jnp.cumsum(lb, axis=1) - lb[:, :1]

    cond_rows = -(-(dec_b + 1) // 8) * 8
    cond = jnp.zeros((cond_rows, d), F32).at[:dec_b].set(c).at[dec_b].set(c_ctx)
    mods_all = _ada_mods(cond, w_ada, b_ada)

    def pair_cols(w):
        w = w.reshape(MOE_STEPS, 2, d, D_EXPERT).astype(BF16)
        return jnp.transpose(w, (0, 2, 1, 3)).reshape(MOE_STEPS, d, 2 * D_EXPERT)

    layers = []
    for l in range(depth):
        lam_init = 0.8 - 0.6 * math.exp(-0.3 * l)
        lp_lam = diff_lambda[l].astype(F32)
        lam = jnp.exp(jnp.sum(lp_lam[0] * lp_lam[1])) - jnp.exp(jnp.sum(lp_lam[2] * lp_lam[3])) + lam_init
        diff_par = jnp.zeros((8, DIFF_V_DIM), F32).at[0].set(lam).at[1].set(diff_subln[l] * (1.0 - lam_init))
        w_route = jnp.zeros((d, ROUTER_LANES), F32)
        w_route = w_route.at[:, :N_GROUPS].set(w_group[l]).at[:, N_GROUPS:N_GROUPS + N_EXPERTS].set(
            w_router[l].reshape(d, N_EXPERTS))
        w_route_hi = w_route.astype(BF16)
        b_route = jnp.zeros((1, ROUTER_LANES), F32)
        b_route = b_route.at[0, :N_GROUPS].set(b_group[l]).at[0, N_GROUPS:N_GROUPS + N_EXPERTS].set(
            b_router[l].reshape(N_EXPERTS))
        layers.append({
            "w_heads": w_in[l, :, :D_HEADS_IN].astype(BF16),
            "w_gates": w_in[l, :, D_HEADS_IN:].astype(BF16),
            "gqa_qn": gqa_q_norm[l], "gqa_kn": gqa_k_norm[l],
            "na_bias": _na_bias_table(na_rpb[l]),
            "diff_par": diff_par,
            "hg_lb": lb[:, l], "hg_norm": hgrn_norm[l],
            "w_branch": w_branch[l].astype(BF16), "w_out": w_out[l].astype(BF16),
            "ln_mix_g": ln_mix_g[l], "ln_mix_b": ln_mix_b[l], "ln_ffn_g": ln_ffn_g[l], "ln_ffn_b": ln_ffn_b[l],
            "w_route": jnp.stack([w_route_hi, (w_route - w_route_hi.astype(F32)).astype(BF16)]), "b_route": b_route,
            "w_gu": jnp.concatenate([pair_cols(w_gate[l]), pair_cols(w_up[l])], axis=-1),
            "w_dn": w_down[l].reshape(MOE_STEPS, 2 * D_EXPERT, d).astype(BF16),
        })

    def split_mods(rows):
        return tuple(rows[:, None, k * d:(k + 1) * d] for k in range(6))

    xp = x_prompt
    ctx_out = []
    for l in range(depth):
        xp, caches = _layer(xp, split_mods(mods_all[l, dec_b:dec_b + 1]), layers[l], None, alpha)
        ctx_out.append(caches)
    stack = lambda i: jnp.stack([t[i] for t in ctx_out], axis=1)

    xs = x_sample
    for l in range(depth):
        ctx = (cache_gqa_k[:, l], cache_gqa_v[:, l], cache_na_k[:, l], cache_na_v[:, l],
               cache_diff_k[:, l], cache_diff_v[:, l], state_hgrn[:, l])
        xs, _ = _layer(xs, split_mods(mods_all[l, :dec_b]), layers[l], ctx, alpha)

    return (xp, xs, stack(0), stack(1), stack(2), stack(3), stack(4), stack(5), stack(6))
```

```python
import functools
import math

import jax
import jax.numpy as jnp
import numpy as np
from jax import lax
from jax.experimental import pallas as pl
from jax.experimental.pallas import tpu as pltpu

F32 = jnp.float32
BF16 = jnp.bfloat16

D_MODEL = 1024
GRID_W = 64
HEAD_DIM = 64
GQA_Q_HEADS = 4
GQA_KV_HEADS = 2
NA_HEADS = 4
NA_KH = 8
NA_KW = 16
DIFF_HEADS = 4
DIFF_QK_DIM = 32
DIFF_V_DIM = 64
HG_HEADS = 4
HG_DK = 64
HG_DV = 64
BRANCH_W = 256
N_BRANCH = 4
N_GROUPS = 4
EXPERTS_PER_GROUP = 4
N_EXPERTS = N_GROUPS * EXPERTS_PER_GROUP
D_EXPERT = 512
ROPE_THETA = 10000.0
LN_EPS = 1e-5
RMS_EPS = 1e-6
F_FLOOR = 1e-30
NEG_BIG = -1e30
LOG2E = math.log2(math.e)

C_GQ, C_GK, C_GV = 0, 256, 384
C_NQ, C_NK, C_NV = 512, 768, 1024
C_DQ, C_DK, C_DV = 1280, 1536, 1792
C_HG = 2048
D_HEADS_IN = 3328
HG_COLS = 1280

VMEM_LIMIT = 56 * 1024 * 1024


def _cparams(sem):
    return pltpu.CompilerParams(dimension_semantics=sem, vmem_limit_bytes=VMEM_LIMIT)


def _dot(a, b):
    return jnp.dot(a, b, preferred_element_type=F32)


def _dot_nt(a, b):
    return lax.dot_general(a, b, (((1,), (1,)), ((), ())), preferred_element_type=F32)


def _split_dot(x, j):
    hi = x.astype(BF16)
    lo = (x - hi.astype(F32)).astype(BF16)
    return _dot(hi, j) + _dot(lo, j)


def _sigmoid(z):
    return 1.0 / (1.0 + jnp.exp(-z))


def _ln_plain(x):
    mu = jnp.mean(x, axis=-1, keepdims=True)
    xc = x - mu
    var = jnp.mean(xc * xc, axis=-1, keepdims=True)
    return xc * lax.rsqrt(var + LN_EPS)


def _block_ones(n, blk):
    idx = np.arange(n) // blk
    return jnp.asarray((idx[:, None] == idx[None, :]).astype(np.float32), dtype=BF16)


def _ada_kernel(c_ref, w_ref, b_ref, o_ref):
    c = c_ref[...]
    s = (c * _sigmoid(c)).astype(BF16)
    o_ref[0] = _dot(s, w_ref[0].astype(BF16)) + b_ref[0]


def _ada_mods(cond, w_ada, b_ada):
    depth, d, n = w_ada.shape
    rows = cond.shape[0]
    tn = 1536
    return pl.pallas_call(
        _ada_kernel,
        grid=(depth, n // tn),
        in_specs=[
            pl.BlockSpec((rows, d), lambda l, j: (0, 0)),
            pl.BlockSpec((1, d, tn), lambda l, j: (l, 0, j)),
            pl.BlockSpec((1, 1, tn), lambda l, j: (l, 0, j)),
        ],
        out_specs=pl.BlockSpec((1, rows, tn), lambda l, j: (l, 0, j)),
        out_shape=jax.ShapeDtypeStruct((depth, rows, n), F32),
        compiler_params=_cparams(("parallel", "parallel")),
        name="ada_mods",
    )(cond, w_ada, b_ada.reshape(depth, 1, n))


def _rope_tables(t_len, d, reps):
    quarter = d // 4
    t = np.arange(t_len)
    inv = ROPE_THETA ** (-np.arange(quarter, dtype=np.float64) / quarter)
    ang_r = (t // GRID_W).astype(np.float64)[:, None] * inv
    ang_c = (t % GRID_W).astype(np.float64)[:, None] * inv
    ang = np.concatenate([ang_r, ang_r, ang_c, ang_c], axis=-1)
    cos, sin = np.cos(ang), np.sin(ang)
    first = (np.arange(d) % (2 * quarter)) < quarter
    sin_a = np.where(first, -sin, 0.0)
    sin_b = np.where(first, 0.0, sin)
    tile = lambda a: jnp.asarray(np.tile(a, (1, reps)), dtype=F32)
    return tile(cos), tile(sin_a), tile(sin_b)


def _rope(x, cos, sin_a, sin_b, quarter):
    w = x.shape[-1]
    return x * cos + pltpu.roll(x, w - quarter, 1) * sin_a + pltpu.roll(x, quarter, 1) * sin_b


def _seg_rms(x, j):
    ss = _split_dot(x * x, j)
    return x * lax.rsqrt(ss * (1.0 / HEAD_DIM) + RMS_EPS)


def _inproj_kernel(*refs, rope):
    if rope:
        (x_ref, sh_ref, sc_ref, w_ref, qn_ref, kn_ref, j_ref,
         c64_ref, a64_ref, b64_ref, c32_ref, a32_ref, b32_ref,
         qa_ref, ka_ref, va_ref, qnb_ref, knb_ref, vnb_ref, qd_ref, kd_ref, vd_ref, hg_ref) = refs
    else:
        (x_ref, sh_ref, sc_ref, w_ref, qn_ref, kn_ref, j_ref,
         qa_ref, ka_ref, va_ref, qnb_ref, knb_ref, vnb_ref, qd_ref, kd_ref, vd_ref, hg_ref) = refs
    h = (_ln_plain(x_ref[0]) * (1.0 + sc_ref[0]) + sh_ref[0]).astype(BF16)
    proj = lambda c0, n: _dot(h, w_ref[:, c0:c0 + n])

    gq = _seg_rms(proj(C_GQ, 256), j_ref[...]) * qn_ref[...]
    gk = _seg_rms(proj(C_GK, 128), j_ref[0:128, 0:128]) * kn_ref[...]
    dq = proj(C_DQ, 256)
    dk = proj(C_DK, 256)
    if rope:
        c64, a64, b64 = c64_ref[...], a64_ref[...], b64_ref[...]
        c32, a32, b32 = c32_ref[...], a32_ref[...], b32_ref[...]
        gq = _rope(gq, c64, a64, b64, HEAD_DIM // 4)
        gk = _rope(gk, c64[:, 0:128], a64[:, 0:128], b64[:, 0:128], HEAD_DIM // 4)
        dq = _rope(dq, c32, a32, b32, DIFF_QK_DIM // 4)
        dk = _rope(dk, c32, a32, b32, DIFF_QK_DIM // 4)
    qa_ref[0] = (gq * (LOG2E * HEAD_DIM ** -0.5)).astype(qa_ref.dtype)
    ka_ref[0] = gk.astype(ka_ref.dtype)
    va_ref[0] = proj(C_GV, 128).astype(va_ref.dtype)
    qnb_ref[0] = (proj(C_NQ, 256) * (LOG2E * HEAD_DIM ** -0.5)).astype(qnb_ref.dtype)
    knb_ref[0] = proj(C_NK, 256).astype(knb_ref.dtype)
    vnb_ref[0] = proj(C_NV, 256).astype(vnb_ref.dtype)
    qd_ref[0] = (dq * (LOG2E * DIFF_QK_DIM ** -0.5)).astype(qd_ref.dtype)
    kd_ref[0] = dk.astype(kd_ref.dtype)
    vd_ref[0] = proj(C_DV, 256).astype(vd_ref.dtype)
    for s in range(HG_COLS // 256):
        hg_ref[0, :, s * 256:(s + 1) * 256] = proj(C_HG + s * 256, 256)


def _inproj(x, shift, scale, w_heads, qn, kn, rope, kv_dtype, tm=256):
    bsz, t_len, d = x.shape
    nb = shift.shape[0]
    mod_map = (lambda b, i: (b, 0, 0)) if nb > 1 else (lambda b, i: (0, 0, 0))
    tok = lambda n: pl.BlockSpec((1, tm, n), lambda b, i: (b, i, 0))
    const = lambda shape: pl.BlockSpec(shape, lambda b, i: (0,) * len(shape))
    in_specs = [tok(d), pl.BlockSpec((1, 1, d), mod_map), pl.BlockSpec((1, 1, d), mod_map),
                const((d, D_HEADS_IN)), const((1, 256)), const((1, 128)), const((256, 256))]
    args = [x, shift, scale, w_heads, jnp.tile(qn, GQA_Q_HEADS)[None], jnp.tile(kn, GQA_KV_HEADS)[None],
            _block_ones(256, HEAD_DIM)]
    if rope:
        in_specs += [pl.BlockSpec((tm, 256), lambda b, i: (i, 0))] * 6
        args += list(_rope_tables(t_len, HEAD_DIM, 4)) + list(_rope_tables(t_len, DIFF_QK_DIM, 8))
    widths = (256, 128, 128, 256, 256, 256, 256, 256, 256)
    dtypes = (BF16, kv_dtype, kv_dtype, BF16, kv_dtype, kv_dtype, BF16, kv_dtype, kv_dtype)
    out_shape = [jax.ShapeDtypeStruct((bsz, t_len, n), dt) for n, dt in zip(widths, dtypes)]
    out_shape.append(jax.ShapeDtypeStruct((bsz, t_len, HG_COLS), F32))
    out_specs = [tok(n) for n in widths] + [tok(HG_COLS)]
    return pl.pallas_call(
        functools.partial(_inproj_kernel, rope=rope),
        grid=(bsz, t_len // tm),
        in_specs=in_specs,
        out_specs=out_specs,
        out_shape=out_shape,
        compiler_params=_cparams(("parallel", "parallel")),
        name="in_proj_rope" if rope else "in_proj",
    )(*args)


def _softmax_p(s):
    return jnp.exp2(s - jnp.max(s, axis=-1, keepdims=True)).astype(BF16)


def _pv_norm(p, v_ext):
    dv = v_ext.shape[-1] // 2
    o = _dot(p, v_ext)
    return o[:, :dv] * (1.0 / o[:, dv:])


def _attn_kernel(q_ref, kt_ref, v_ref, o_ref, *, hq, group, d):
    q = q_ref[0]
    tq = q.shape[0]
    n_groups = hq // group

    def scores(g):
        qs = jnp.concatenate([q[:, h * d:(h + 1) * d] for h in range(g * group, (g + 1) * group)], axis=0)
        return _dot(qs, kt_ref[0, g])

    outs = []
    s_next = scores(0)
    for g in range(n_groups):
        s_cur, s_next = s_next, (scores(g + 1) if g + 1 < n_groups else None)
        o = _pv_norm(_softmax_p(s_cur), v_ref[0, g])
        outs += [o[j * tq:(j + 1) * tq] for j in range(group)]
    o_ref[0] = jnp.concatenate(outs, axis=-1).astype(o_ref.dtype)


def _attention(q, kt, v_ext, hq, tq):
    bsz, t_len, qw = q.shape
    _, hkv, d, l_len = kt.shape
    dv = v_ext.shape[-1] // 2
    return pl.pallas_call(
        functools.partial(_attn_kernel, hq=hq, group=hq // hkv, d=d),
        grid=(bsz, t_len // tq),
        in_specs=[
            pl.BlockSpec((1, tq, qw), lambda b, i: (b, i, 0)),
            pl.BlockSpec((1, hkv, d, l_len), lambda b, i: (b, 0, 0, 0)),
            pl.BlockSpec((1, hkv, l_len, 2 * dv), lambda b, i: (b, 0, 0, 0)),
        ],
        out_specs=pl.BlockSpec((1, tq, hq * dv), lambda b, i: (b, i, 0)),
        out_shape=jax.ShapeDtypeStruct((bsz, t_len, hq * dv), BF16),
        compiler_params=_cparams(("parallel", "parallel")),
        name="attention",
    )(q, kt, v_ext)


def _diff_attn_kernel(q_ref, kt_ref, v_ref, par_ref, o_ref):
    q = q_ref[0]
    tq = q.shape[0]
    lam = par_ref[0:1, :]
    gain = par_ref[1:2, :]
    dq = DIFF_QK_DIM
    scores = lambda i: _dot(q[:, i * dq:(i + 1) * dq], kt_ref[0, i])
    outs = []
    s_next = (scores(0), scores(1))
    for h in range(DIFF_HEADS):
        s_cur, s_next = s_next, ((scores(2 * h + 2), scores(2 * h + 3)) if h + 1 < DIFF_HEADS else None)
        o12 = _pv_norm(jnp.concatenate([_softmax_p(s_cur[0]), _softmax_p(s_cur[1])], axis=0), v_ref[0, h])
        o = o12[:tq] - lam * o12[tq:]
        ms = jnp.mean(o * o, axis=-1, keepdims=True)
        outs.append(o * lax.rsqrt(ms + RMS_EPS) * gain)
    o_ref[0] = jnp.concatenate(outs, axis=-1).astype(o_ref.dtype)


def _diff_attention(q, kt, v_ext, par, tq):
    bsz, t_len, qw = q.shape
    l_len = kt.shape[-1]
    return pl.pallas_call(
        _diff_attn_kernel,
        grid=(bsz, t_len // tq),
        in_specs=[
            pl.BlockSpec((1, tq, qw), lambda b, i: (b, i, 0)),
            pl.BlockSpec((1, 2 * DIFF_HEADS, DIFF_QK_DIM, l_len), lambda b, i: (b, 0, 0, 0)),
            pl.BlockSpec((1, DIFF_HEADS, l_len, 2 * DIFF_V_DIM), lambda b, i: (b, 0, 0, 0)),
            pl.BlockSpec((8, DIFF_V_DIM), lambda b, i: (0, 0)),
        ],
        out_specs=pl.BlockSpec((1, tq, DIFF_HEADS * DIFF_V_DIM), lambda b, i: (b, i, 0)),
        out_shape=jax.ShapeDtypeStruct((bsz, t_len, DIFF_HEADS * DIFF_V_DIM), BF16),
        compiler_params=_cparams(("parallel", "parallel")),
        name="diff_attention",
    )(q, kt, v_ext, par)


def _na_bias_table(rpb):
    shift = np.arange(NA_KH)[:, None]
    win_row = np.arange(NA_KH)[None, :]
    row_off = win_row - shift + (NA_KH - 1)
    col = np.arange(GRID_W)
    col_start = np.clip(col - NA_KW // 2, 0, GRID_W - NA_KW)
    key_col = np.arange(GRID_W)[None, :]
    valid = (key_col >= col_start[:, None]) & (key_col < col_start[:, None] + NA_KW)
    col_off = key_col - col[:, None] + (NA_KW - 1)
    row_sel = (row_off[:, :, None] == np.arange(2 * NA_KH - 1)).astype(np.float32)
    col_sel = ((col_off[:, :, None] == np.arange(2 * NA_KW - 1)) & valid[:, :, None]).astype(np.float32)
    tbl = jnp.einsum("sia,hab,cxb->shcix", row_sel, rpb.astype(F32) * LOG2E, col_sel,
                     precision=lax.Precision.HIGHEST)
    tbl = jnp.where(valid[None, None, :, None, :], tbl, NEG_BIG)
    return tbl.reshape(NA_KH, rpb.shape[0], GRID_W, NA_KH * GRID_W)


def _na_kernel(q_ref, k_ref, v_ref, kc_ref, vc_ref, bias_ref, o_ref, *, rows_per_step, rows):
    i = pl.program_id(1)
    n_win = NA_KH * GRID_W
    units = [(rr, h) for rr in range(rows_per_step) for h in range(NA_HEADS)]

    def window(rr):
        r = i * rows_per_step + rr
        r_start = jnp.clip(r - NA_KH // 2, 0, rows - NA_KH)
        return r - r_start, pl.multiple_of(r_start * GRID_W, GRID_W)

    def scores(unit):
        rr, h = unit
        shift, start = window(rr)
        qh = q_ref[0, rr * GRID_W:(rr + 1) * GRID_W, h * HEAD_DIM:(h + 1) * HEAD_DIM]
        s_loc = _dot_nt(qh, k_ref[0, h, pl.ds(start, n_win), :]) + bias_ref[shift, h]
        return s_loc, _dot_nt(qh, kc_ref[0, h])

    def output(unit, s_loc, s_ctx):
        rr, h = unit
        _, start = window(rr)
        m = jnp.maximum(jnp.max(s_loc, axis=-1, keepdims=True), jnp.max(s_ctx, axis=-1, keepdims=True))
        p_loc = jnp.exp2(s_loc - m).astype(BF16)
        p_ctx = jnp.exp2(s_ctx - m).astype(BF16)
        o = _dot(p_loc, v_ref[0, h, pl.ds(start, n_win), :]) + _dot(p_ctx, vc_ref[0, h])
        return o[:, :HEAD_DIM] * (1.0 / o[:, HEAD_DIM:])

    depth = 4
    staged = [scores(u) for u in units[:depth]]
    outs = []
    for n, unit in enumerate(units):
        cur = staged.pop(0)
        if n + depth < len(units):
            staged.append(scores(units[n + depth]))
        outs.append(output(unit, *cur))
        if unit[1] == NA_HEADS - 1:
            rr = unit[0]
            o_ref[0, rr * GRID_W:(rr + 1) * GRID_W, :] = jnp.concatenate(outs, axis=-1).astype(o_ref.dtype)
            outs = []


def _na_attention(q, k, v_ext, kc, vc_ext, bias, rows_per_step=8):
    bsz, n, qw = q.shape
    rows = n // GRID_W
    assert rows >= NA_KH and rows % rows_per_step == 0
    l_ctx = kc.shape[2]
    tq = rows_per_step * GRID_W
    full = lambda shape: pl.BlockSpec((1,) + shape, lambda b, i: (b,) + (0,) * len(shape))
    return pl.pallas_call(
        functools.partial(_na_kernel, rows_per_step=rows_per_step, rows=rows),
        grid=(bsz, rows // rows_per_step),
        in_specs=[
            pl.BlockSpec((1, tq, qw), lambda b, i: (b, i, 0)),
            full((NA_HEADS, n, HEAD_DIM)), full((NA_HEADS, n, 2 * HEAD_DIM)),
            full((NA_HEADS, l_ctx, HEAD_DIM)), full((NA_HEADS, l_ctx, 2 * HEAD_DIM)),
            pl.BlockSpec(bias.shape, lambda b, i: (0, 0, 0, 0)),
        ],
        out_specs=pl.BlockSpec((1, tq, qw), lambda b, i: (b, i, 0)),
        out_shape=jax.ShapeDtypeStruct((bsz, n, qw), BF16),
        compiler_params=_cparams(("parallel", "parallel")),
        name="na_attention",
    )(q, k, v_ext, kc, vc_ext, bias)


HG_SUB = 16
HG_W = HG_HEADS * HG_DK


def _hgrn_direction(hg_ref, vt_ref, lb, tri, jv, bd_mask, st_ref, z_col, reverse, tc):
    c = HG_SUB
    n_sub = tc // c
    hq = hg_ref[0, :, 0:HG_W]
    q = hq * _sigmoid(hq)
    z = hg_ref[0, :, z_col:z_col + HG_W]
    v = hg_ref[0, :, 3 * HG_W:4 * HG_W]
    f = lb + (1.0 - lb) * _sigmoid(z)
    log_f = jnp.log(jnp.maximum(f, F_FLOOR))
    kk = (1.0 - lb) * _sigmoid(-z)
    b = jnp.dot(tri, log_f * LOG2E, preferred_element_type=F32, precision=lax.Precision.HIGHEST)
    st = st_ref[...]
    o_inter = _dot_nt((q * jnp.exp2(b)).astype(BF16), st.astype(BF16))
    b_end = b[0:1] if reverse else b[tc - 1:tc]
    upd = _dot(vt_ref[0].astype(BF16), (kk * jnp.exp2(b_end - b)).astype(BF16))
    st_ref[...] = jnp.exp2(b_end) * st + jnp.where(bd_mask, upd, 0.0)

    half = c // 2
    row = lax.broadcasted_iota(jnp.int32, (half, HG_W), 0)
    lane_head = lax.broadcasted_iota(jnp.int32, (c, HG_W), 1) // HG_DK
    v_bf = v.astype(BF16)
    zeros = jnp.zeros((half, HG_W), F32)
    def reach(s, top):
        if reverse:
            return ("all" if s >= half else "tri") if top else ("none" if s < half else "tri")
        return ("none" if s >= half else "tri") if top else ("all" if s < half else "tri")

    def score_stage(j):
        sl = slice(j * c, (j + 1) * c)
        bj, qj, kj = b[sl], q[sl], kk[sl]
        cj = bj - jnp.log2(kj)
        prods = []
        for s in range(c):
            for top in (True, False):
                kind = reach(s, top)
                if kind == "none":
                    prods.append(zeros)
                    continue
                hs = slice(0, half) if top else slice(half, c)
                decay_k = jnp.exp2(bj[hs] - cj[s:s + 1])
                if kind == "tri":
                    s_in = s % half
                    decay_k = jnp.where((row <= s_in) if reverse else (row >= s_in), decay_k, 0.0)
                prods.append(qj[hs] * decay_k)
        diag = _dot(jnp.concatenate(prods, axis=0).astype(BF16), jv)
        if reverse:
            keys, edge = slice((j + 1) * c, tc), (j + 1) * c
        else:
            keys, edge = slice(0, j * c), j * c - 1
        if keys.stop == keys.start:
            return diag, None, keys
        ref_b = b[edge:edge + 1]
        qd = qj * jnp.exp2(bj - ref_b)
        kd = (kk[keys] * jnp.exp2(ref_b - b[keys])).astype(BF16)
        qm = jnp.concatenate([jnp.where(lane_head == h, qd, 0.0) for h in range(HG_HEADS)], axis=0)
        return diag, _dot_nt(qm.astype(BF16), kd), keys

    def output_stage(j, diag, off, keys):
        sl = slice(j * c, (j + 1) * c)
        vj = v[sl]
        o_top, o_bot = o_inter[j * c:j * c + half], o_inter[j * c + half:(j + 1) * c]
        for s in range(c):
            if reach(s, True) != "none":
                o_top = o_top + diag[s * c:s * c + half] * vj[s:s + 1]
            if reach(s, False) != "none":
                o_bot = o_bot + diag[s * c + half:(s + 1) * c] * vj[s:s + 1]
        o = jnp.concatenate([o_top, o_bot], axis=0)
        if off is not None:
            ov = _dot(off.astype(BF16), v_bf[keys])
            for h in range(HG_HEADS):
                o = o + jnp.where(lane_head == h, ov[h * c:(h + 1) * c], 0.0)
        return o

    outs = []
    staged = score_stage(0)
    for j in range(n_sub):
        cur, staged = staged, (score_stage(j + 1) if j + 1 < n_sub else None)
        outs.append(output_stage(j, *cur))
    return jnp.concatenate(outs, axis=0)


def _hgrn_kernel(hg_ref, vt_ref, lb_ref, gn_ref, s0_ref, tri_ref, jv_ref, o_ref, sfin_ref, st_scr, of_scr, *, tc, nc):
    p = pl.program_id(1)
    i = pl.program_id(2)
    r_i = lax.broadcasted_iota(jnp.int32, (HG_W, HG_W), 0) // HG_DV
    c_i = lax.broadcasted_iota(jnp.int32, (HG_W, HG_W), 1) // HG_DK
    bd_mask = r_i == c_i
    jv = jv_ref[...]

    @pl.when(i == 0)
    def _():
        st_scr[...] = s0_ref[0, p]

    @pl.when(p == 0)
    def _():
        o = _hgrn_direction(hg_ref, vt_ref, lb_ref[0:1, :], tri_ref[0], jv, bd_mask, st_scr, HG_W, False, tc)
        of_scr[pl.ds(pl.multiple_of(i * tc, tc), tc), :] = o

    @pl.when(p == 1)
    def _():
        o_b = _hgrn_direction(hg_ref, vt_ref, lb_ref[1:2, :], tri_ref[1], jv, bd_mask, st_scr, 2 * HG_W, True, tc)
        o = of_scr[pl.ds(pl.multiple_of((nc - 1 - i) * tc, tc), tc), :] + o_b
        ms = _split_dot(o * o, jv) * (1.0 / HG_DV)
        g = hg_ref[0, :, 4 * HG_W:5 * HG_W]
        o_ref[0] = (o * lax.rsqrt(ms + RMS_EPS) * gn_ref[...] * (g * _sigmoid(g))).astype(o_ref.dtype)

    @pl.when(i == nc - 1)
    def _():
        sfin_ref[0, p] = st_scr[...]


def _hgrn(hg, lb2, gn, s0t, tc=256):
    bsz, t_len, _ = hg.shape
    nc = t_len // tc
    vt = jnp.swapaxes(hg[:, :, 3 * HG_W:4 * HG_W], 1, 2)
    lower = np.arange(tc)[:, None] >= np.arange(tc)[None, :]
    tri = jnp.asarray(np.stack([lower, lower.T]).astype(np.float32))
    chunk = lambda b, p, i: (b, i + p * (nc - 1 - 2 * i), 0)
    o, sfin = pl.pallas_call(
        functools.partial(_hgrn_kernel, tc=tc, nc=nc),
        grid=(bsz, 2, nc),
        in_specs=[
            pl.BlockSpec((1, tc, HG_COLS), chunk),
            pl.BlockSpec((1, HG_W, tc), lambda b, p, i: (b, 0, i + p * (nc - 1 - 2 * i))),
            pl.BlockSpec((2, HG_W), lambda b, p, i: (0, 0)),
            pl.BlockSpec((1, HG_W), lambda b, p, i: (0, 0)),
            pl.BlockSpec((1, 2, HG_W, HG_W), lambda b, p, i: (b, 0, 0, 0)),
            pl.BlockSpec((2, tc, tc), lambda b, p, i: (0, 0, 0)),
            pl.BlockSpec((HG_W, HG_W), lambda b, p, i: (0, 0)),
        ],
        out_specs=[
            pl.BlockSpec((1, tc, HG_W), lambda b, p, i: (b, nc - 1 - i * p, 0)),
            pl.BlockSpec((1, 2, HG_W, HG_W), lambda b, p, i: (b, 0, 0, 0)),
        ],
        out_shape=[jax.ShapeDtypeStruct((bsz, t_len, HG_W), BF16),
                   jax.ShapeDtypeStruct((bsz, 2, HG_W, HG_W), F32)],
        scratch_shapes=[pltpu.VMEM((HG_W, HG_W), F32), pltpu.VMEM((t_len, HG_W), F32)],
        compiler_params=_cparams(("parallel", "arbitrary", "arbitrary")),
        name="hgrn",
    )(hg, vt, lb2, jnp.tile(gn, HG_HEADS)[None], s0t, tri, _block_ones(HG_W, HG_DV))
    return o, sfin


def _state_to_blockdiag_t(s0):
    bsz = s0.shape[0]
    s0t = jnp.swapaxes(s0, -1, -2)
    eye = jnp.eye(HG_HEADS, dtype=s0.dtype)
    return jnp.einsum("bdhvk,hg->bdhvgk", s0t, eye).reshape(bsz, 2, HG_W, HG_W)


def _blockdiag_t_to_state(st):
    bsz = st.shape[0]
    s6 = st.reshape(bsz, 2, HG_HEADS, HG_DV, HG_HEADS, HG_DK)
    diag = jnp.stack([s6[:, :, h, :, h, :] for h in range(HG_HEADS)], axis=2)
    return jnp.swapaxes(diag, -1, -2)


def _post_kernel(x_ref, oa_ref, on_ref, od_ref, oh_ref, sh_ref, sc_ref, g1_ref, wg_ref, wb_ref, wo_ref,
                 lg_ref, lb_ref, o_ref, *, alpha):
    x = x_ref[0]
    h = (_ln_plain(x) * (1.0 + sc_ref[0]) + sh_ref[0]).astype(BF16)
    mixed = None
    for n, br_ref in enumerate((oa_ref, on_ref, od_ref, oh_ref)):
        gate = _sigmoid(_dot(h, wg_ref[:, n * D_MODEL:(n + 1) * D_MODEL]))
        term = gate * _dot(br_ref[0], wb_ref[n])
        mixed = term if mixed is None else mixed + term
    y = _dot(mixed.astype(BF16), wo_ref[...])
    o_ref[0] = _ln_plain(alpha * x + g1_ref[0] * y) * lg_ref[...] + lb_ref[...]


def _post_mixer(x, branches, shift, scale, gate, w_gates, w_branch, w_out, ln_g, ln_b, alpha, tm=256):
    bsz, t_len, d = x.shape
    nb = shift.shape[0]
    mod_map = (lambda b, i: (b, 0, 0)) if nb > 1 else (lambda b, i: (0, 0, 0))
    tok = lambda n: pl.BlockSpec((1, tm, n), lambda b, i: (b, i, 0))
    const = lambda shape: pl.BlockSpec(shape, lambda b, i: (0,) * len(shape))
    mod = pl.BlockSpec((1, 1, d), mod_map)
    return pl.pallas_call(
        functools.partial(_post_kernel, alpha=alpha),
        grid=(bsz, t_len // tm),
        in_specs=[tok(d)] + [tok(BRANCH_W)] * 4 + [mod, mod, mod,
                  const((d, N_BRANCH * d)), const((N_BRANCH, BRANCH_W, d)), const((d, d)),
                  const((1, d)), const((1, d))],
        out_specs=tok(d),
        out_shape=jax.ShapeDtypeStruct((bsz, t_len, d), F32),
        compiler_params=_cparams(("parallel", "parallel")),
        name="post_mixer",
    )(x, *branches, shift, scale, gate, w_gates, w_branch, w_out, ln_g[None], ln_b[None])


ROUTER_LANES = 128
MOE_BT = 1024
MOE_ROWS = 128
MOE_STEPS = N_EXPERTS // 2


def _router_weights(logits):
    lane = lax.broadcasted_iota(jnp.int32, logits.shape, 1)
    first_at = lambda mask: jnp.min(jnp.where(mask, lane, ROUTER_LANES), axis=-1, keepdims=True)
    is_g = lane < N_GROUPS
    gl = jnp.where(is_g, logits, NEG_BIG)
    g_max = jnp.max(gl, axis=-1, keepdims=True)
    g_idx = first_at(is_g & (gl == g_max))
    g_prob = 1.0 / jnp.sum(jnp.where(is_g, jnp.exp(gl - g_max), 0.0), axis=-1, keepdims=True)
    lo = N_GROUPS + g_idx * EXPERTS_PER_GROUP
    in_grp = (lane >= lo) & (lane < lo + EXPERTS_PER_GROUP)
    el = jnp.where(in_grp, logits, NEG_BIG)
    v1 = jnp.max(el, axis=-1, keepdims=True)
    i1 = first_at(in_grp & (el == v1))
    rest = in_grp & (lane != i1)
    el2 = jnp.where(rest, logits, NEG_BIG)
    v2 = jnp.max(el2, axis=-1, keepdims=True)
    i2 = first_at(rest & (el2 == v2))
    e2 = jnp.exp(v2 - v1)
    w1 = g_prob / (1.0 + e2)
    w2 = g_prob * e2 / (1.0 + e2)
    return g_idx, jnp.where(lane == i1, w1, jnp.where(lane == i2, w2, 0.0))


def _moe_kernel(x_ref, sh_ref, sc_ref, g2_ref, wr_ref, br_ref, lt_ref, wgu_ref, wd_ref, lg_ref, lb_ref, o_ref,
                xs_scr, pt_scr, cws_scr, ys_scr, seg_ref, *, alpha):
    s = pl.program_id(2)
    bt = xs_scr.shape[0]
    rows = MOE_ROWS

    @pl.when(s == 0)
    def _():
        xm = _ln_plain(x_ref[0]) * (1.0 + sc_ref[0]) + sh_ref[0]
        x_hi = xm.astype(BF16)
        x_lo = (xm - x_hi.astype(F32)).astype(BF16)
        logits = _dot(x_hi, wr_ref[0]) + _dot(x_lo, wr_ref[0]) + _dot(x_hi, wr_ref[1]) + br_ref[...]
        g_idx, cw = _router_weights(logits)
        lane = lax.broadcasted_iota(jnp.int32, (bt, ROUTER_LANES), 1)
        in_g = lane == g_idx
        onehot = jnp.where(in_g, 1.0, 0.0)
        before = _dot(lt_ref[...], onehot.astype(BF16))
        count = jnp.sum(onehot, axis=0, keepdims=True)
        lane1 = lax.broadcasted_iota(jnp.int32, (1, ROUTER_LANES), 1)
        start = jnp.zeros((1, ROUTER_LANES), F32)
        for k in range(N_GROUPS - 1):
            start = start + jnp.where(lane1 > k, count[:, k:k + 1], 0.0)
        dest = jnp.sum(jnp.where(in_g, start + before, 0.0), axis=-1, keepdims=True)
        dest_row = jnp.transpose(jnp.broadcast_to(dest, (bt, ROUTER_LANES)))[0:1, :]
        pt_scr[...] = jnp.where(dest.astype(jnp.int32) == lax.broadcasted_iota(jnp.int32, (bt, bt), 1),
                                1.0, 0.0).astype(BF16)
        perm = jnp.where(dest_row.astype(jnp.int32) == lax.broadcasted_iota(jnp.int32, (bt, bt), 0),
                         1.0, 0.0).astype(BF16)
        xs_scr[...] = _dot(perm, x_hi).astype(BF16)
        c_hi = cw.astype(BF16)
        c_lo = (cw - c_hi.astype(F32)).astype(BF16)
        cws_scr[...] = _dot(perm, c_hi) + _dot(perm, c_lo)
        ys_scr[...] = jnp.zeros_like(ys_scr)
        for g in range(N_GROUPS):
            seg_ref[g] = jnp.sum(jnp.where(lane1 == g, start, 0.0)).astype(jnp.int32)
            seg_ref[N_GROUPS + g] = jnp.sum(jnp.where(lane1 == g, start + count, 0.0)).astype(jnp.int32)

    g = s // (MOE_STEPS // N_GROUPS)
    seg_lo = seg_ref[g]
    seg_hi = seg_ref[N_GROUPS + g]
    t_lo = seg_lo // rows
    t_hi = jnp.where(seg_hi > seg_lo, (seg_hi + rows - 1) // rows, t_lo)
    lane_r = lax.broadcasted_iota(jnp.int32, (rows, ROUTER_LANES), 1)
    f = D_EXPERT

    def tile(t, carry):
        r0 = pl.multiple_of(t * rows, rows)
        gu = _dot(xs_scr[pl.ds(r0, rows), :], wgu_ref[0])
        cwt = cws_scr[pl.ds(r0, rows), :]
        w_a = jnp.sum(jnp.where(lane_r == N_GROUPS + 2 * s, cwt, 0.0), axis=-1, keepdims=True)
        w_b = jnp.sum(jnp.where(lane_r == N_GROUPS + 2 * s + 1, cwt, 0.0), axis=-1, keepdims=True)
        h_a = gu[:, 0:f] * _sigmoid(gu[:, 0:f]) * gu[:, 2 * f:3 * f] * w_a
        h_b = gu[:, f:2 * f] * _sigmoid(gu[:, f:2 * f]) * gu[:, 3 * f:4 * f] * w_b
        hid = jnp.concatenate([h_a, h_b], axis=-1).astype(BF16)
        ys_scr[pl.ds(r0, rows), :] += _dot(hid, wd_ref[0])
        return carry

    lax.fori_loop(t_lo, t_hi, tile, 0)

    @pl.when(s == MOE_STEPS - 1)
    def _():
        ys = ys_scr[...]
        hi = ys.astype(BF16)
        lo = (ys - hi.astype(F32)).astype(BF16)
        pt = pt_scr[...]
        y = _dot(pt, hi) + _dot(pt, lo)
        o_ref[0] = _ln_plain(alpha * x_ref[0] + g2_ref[0] * y) * lg_ref[...] + lb_ref[...]


def _moe(x, shift, scale, gate, w_route, b_route, w_gu, w_dn, ln_g, ln_b, alpha):
    bsz, t_len, d = x.shape
    bt = min(MOE_BT, t_len)
    assert t_len % bt == 0 and bt % MOE_ROWS == 0
    nb = shift.shape[0]
    mod_map = (lambda b, i, e: (b, 0, 0)) if nb > 1 else (lambda b, i, e: (0, 0, 0))
    tok = pl.BlockSpec((1, bt, d), lambda b, i, e: (b, i, 0))
    const = lambda shape: pl.BlockSpec(shape, lambda b, i, e: (0,) * len(shape))
    mod = pl.BlockSpec((1, 1, d), mod_map)
    pair = lambda shape: pl.BlockSpec((1,) + shape, lambda b, i, e: (e, 0, 0))
    lower = jnp.asarray(np.tril(np.ones((bt, bt), np.float32), -1), dtype=BF16)
    return pl.pallas_call(
        functools.partial(_moe_kernel, alpha=alpha),
        grid=(bsz, t_len // bt, MOE_STEPS),
        in_specs=[tok, mod, mod, mod, const((2, d, ROUTER_LANES)), const((1, ROUTER_LANES)), const((bt, bt)),
                  pair((d, 4 * D_EXPERT)), pair((2 * D_EXPERT, d)), const((1, d)), const((1, d))],
        out_specs=tok,
        out_shape=jax.ShapeDtypeStruct((bsz, t_len, d), F32),
        scratch_shapes=[pltpu.VMEM((bt, d), BF16), pltpu.VMEM((bt, bt), BF16), pltpu.VMEM((bt, ROUTER_LANES), F32),
                        pltpu.VMEM((bt, d), F32), pltpu.SMEM((2 * N_GROUPS,), jnp.int32)],
        compiler_params=_cparams(("parallel", "parallel", "arbitrary")),
        name="moe",
    )(x, shift, scale, gate, w_route, b_route, lower, w_gu, w_dn, ln_g[None], ln_b[None])


def _heads_t(a, heads, d):
    bsz, l_len, _ = a.shape
    return jnp.transpose(a.reshape(bsz, l_len, heads, d), (0, 2, 3, 1))


def _heads_rows(a, heads, d):
    bsz, l_len, _ = a.shape
    return jnp.transpose(a.reshape(bsz, l_len, heads, d), (0, 2, 1, 3))


def _heads_rows_ext(a, heads, d):
    r = _heads_rows(a, heads, d)
    return jnp.concatenate([r, jnp.ones_like(r)], axis=-1)


def _layer(x, mods, lp, ctx, alpha):
    bsz, t_len, _ = x.shape
    latent = ctx is not None
    shift1, scale1, gate1, shift2, scale2, gate2 = mods
    kv_dtype = BF16 if latent else F32
    qa, ka, va, qn, kn, vn, qd, kd, vd, hg = _inproj(
        x, shift1, scale1, lp["w_heads"], lp["gqa_qn"], lp["gqa_kn"], latent, kv_dtype)
    flat = lambda a: a.reshape(a.shape[0], a.shape[1], -1).astype(BF16)
    if latent:
        c_gk, c_gv, c_nk, c_nv, c_dk, c_dv, c_state = ctx
        ka_all = jnp.concatenate([flat(c_gk), ka], axis=1)
        va_all = jnp.concatenate([flat(c_gv), va], axis=1)
        kd_all = jnp.concatenate([flat(c_dk), kd], axis=1)
        vd_all = jnp.concatenate([flat(c_dv), vd], axis=1)
        s0t = _state_to_blockdiag_t(c_state.astype(F32))
        tq = 256
    else:
        ka_all, va_all, kd_all, vd_all = flat(ka), flat(va), flat(kd), flat(vd)
        s0t = jnp.zeros((bsz, 2, HG_W, HG_W), F32)
        tq = t_len
    o_a = _attention(qa, _heads_t(ka_all, GQA_KV_HEADS, HEAD_DIM), _heads_rows_ext(va_all, GQA_KV_HEADS, HEAD_DIM),
                     GQA_Q_HEADS, tq)
    if latent:
        o_n = _na_attention(qn, _heads_rows(kn, NA_HEADS, HEAD_DIM), _heads_rows_ext(vn, NA_HEADS, HEAD_DIM),
                            _heads_rows(flat(c_nk), NA_HEADS, HEAD_DIM), _heads_rows_ext(flat(c_nv), NA_HEADS, HEAD_DIM),
                            lp["na_bias"])
    else:
        o_n = _attention(qn, _heads_t(flat(kn), NA_HEADS, HEAD_DIM), _heads_rows_ext(flat(vn), NA_HEADS, HEAD_DIM),
                         NA_HEADS, tq)
    o_d = _diff_attention(qd, _heads_t(kd_all, 2 * DIFF_HEADS, DIFF_QK_DIM),
                          _heads_rows_ext(vd_all, DIFF_HEADS, DIFF_V_DIM), lp["diff_par"], tq)
    o_h, sfin = _hgrn(hg, lp["hg_lb"], lp["hg_norm"], s0t)
    x1 = _post_mixer(x, (o_a, o_n, o_d, o_h), shift1, scale1, gate1, lp["w_gates"], lp["w_branch"], lp["w_out"],
                     lp["ln_mix_g"], lp["ln_mix_b"], alpha)
    x1r = x1 if latent else x1.reshape(1, bsz * t_len, D_MODEL)
    y = _moe(x1r, shift2, scale2, gate2, lp["w_route"], lp["b_route"], lp["w_gu"], lp["w_dn"],
             lp["ln_ffn_g"], lp["ln_ffn_b"], alpha).reshape(bsz, t_len, D_MODEL)
    caches = None
    if not latent:
        caches = (ka.reshape(bsz, t_len, GQA_KV_HEADS, HEAD_DIM), va.reshape(bsz, t_len, GQA_KV_HEADS, HEAD_DIM),
                  kn.reshape(bsz, t_len, NA_HEADS, HEAD_DIM), vn.reshape(bsz, t_len, NA_HEADS, HEAD_DIM),
                  kd.reshape(bsz, t_len, DIFF_HEADS, 2, DIFF_QK_DIM), vd.reshape(bsz, t_len, DIFF_HEADS, DIFF_V_DIM),
                  _blockdiag_t_to_state(sfin))
    return y, caches


def kernel(x_prompt, x_sample, cache_gqa_k, cache_gqa_v, cache_na_k, cache_na_v, cache_diff_k, cache_diff_v, state_hgrn, c, c_ctx, w_ada, b_ada, w_in, gqa_q_norm, gqa_k_norm, na_rpb, diff_lambda, diff_subln, hgrn_lb, hgrn_norm, w_branch, w_out, ln_mix_g, ln_mix_b, ln_ffn_g, ln_ffn_b, w_group, b_group, w_router, b_router, w_gate, w_up, w_down):
    depth = w_in.shape[0]
    d = D_MODEL
    alpha = (2 * depth) ** 0.25
    dec_b = c.shape[0]

    lb = jax.nn.softmax(hgrn_lb.astype(F32), axis=1)
    lb = jnp.cumsum(lb, axis=1) - lb[:, :1]

    cond_rows = -(-(dec_b + 1) // 8) * 8
    cond = jnp.zeros((cond_rows, d), F32).at[:dec_b].set(c).at[dec_b].set(c_ctx)
    mods_all = _ada_mods(cond, w_ada, b_ada)

    def pair_cols(w):
        w = w.reshape(MOE_STEPS, 2, d, D_EXPERT).astype(BF16)
        return jnp.transpose(w, (0, 2, 1, 3)).reshape(MOE_STEPS, d, 2 * D_EXPERT)

    layers = []
    for l in range(depth):
        lam_init = 0.8 - 0.6 * math.exp(-0.3 * l)
        lp_lam = diff_lambda[l].astype(F32)
        lam = jnp.exp(jnp.sum(lp_lam[0] * lp_lam[1])) - jnp.exp(jnp.sum(lp_lam[2] * lp_lam[3])) + lam_init
        diff_par = jnp.zeros((8, DIFF_V_DIM), F32).at[0].set(lam).at[1].set(diff_subln[l] * (1.0 - lam_init))
        w_route = jnp.zeros((d, ROUTER_LANES), F32)
        w_route = w_route.at[:, :N_GROUPS].set(w_group[l]).at[:, N_GROUPS:N_GROUPS + N_EXPERTS].set(
            w_router[l].reshape(d, N_EXPERTS))
        w_route_hi = w_route.astype(BF16)
        b_route = jnp.zeros((1, ROUTER_LANES), F32)
        b_route = b_route.at[0, :N_GROUPS].set(b_group[l]).at[0, N_GROUPS:N_GROUPS + N_EXPERTS].set(
            b_router[l].reshape(N_EXPERTS))
        layers.append({
            "w_heads": w_in[l, :, :D_HEADS_IN].astype(BF16),
            "w_gates": w_in[l, :, D_HEADS_IN:].astype(BF16),
            "gqa_qn": gqa_q_norm[l], "gqa_kn": gqa_k_norm[l],
            "na_bias": _na_bias_table(na_rpb[l]),
            "diff_par": diff_par,
            "hg_lb": lb[:, l], "hg_norm": hgrn_norm[l],
            "w_branch": w_branch[l].astype(BF16), "w_out": w_out[l].astype(BF16),
            "ln_mix_g": ln_mix_g[l], "ln_mix_b": ln_mix_b[l], "ln_ffn_g": ln_ffn_g[l], "ln_ffn_b": ln_ffn_b[l],
            "w_route": jnp.stack([w_route_hi, (w_route - w_route_hi.astype(F32)).astype(BF16)]), "b_route": b_route,
            "w_gu": jnp.concatenate([pair_cols(w_gate[l]), pair_cols(w_up[l])], axis=-1),
            "w_dn": w_down[l].reshape(MOE_STEPS, 2 * D_EXPERT, d).astype(BF16),
        })

    def split_mods(rows):
        return tuple(rows[:, None, k * d:(k + 1) * d] for k in range(6))

    xp = x_prompt
    ctx_out = []
    for l in range(depth):
        xp, caches = _layer(xp, split_mods(mods_all[l, dec_b:dec_b + 1]), layers[l], None, alpha)
        ctx_out.append(caches)
    stack = lambda i: jnp.stack([t[i] for t in ctx_out], axis=1)

    xs = x_sample
    for l in range(depth):
        ctx = (cache_gqa_k[:, l], cache_gqa_v[:, l], cache_na_k[:, l], cache_na_v[:, l],
               cache_diff_k[:, l], cache_diff_v[:, l], state_hgrn[:, l])
        xs, _ = _layer(xs, split_mods(mods_all[l, :dec_b]), layers[l], ctx, alpha)

    return (xp, xs, stack(0), stack(1), stack(2), stack(3), stack(4), stack(5), stack(6))
```

```python
import functools
import math

import jax
import jax.numpy as jnp
import numpy as np
from jax import lax
from jax.experimental import pallas as pl
from jax.experimental.pallas import tpu as pltpu

F32 = jnp.float32
BF16 = jnp.bfloat16

D_MODEL = 1024
GRID_W = 64
HEAD_DIM = 64
GQA_Q_HEADS = 4
GQA_KV_HEADS = 2
NA_HEADS = 4
NA_KH = 8
NA_KW = 16
DIFF_HEADS = 4
DIFF_QK_DIM = 32
DIFF_V_DIM = 64
HG_HEADS = 4
HG_DK = 64
HG_DV = 64
BRANCH_W = 256
N_BRANCH = 4
N_GROUPS = 4
EXPERTS_PER_GROUP = 4
N_EXPERTS = N_GROUPS * EXPERTS_PER_GROUP
D_EXPERT = 512
ROPE_THETA = 10000.0
LN_EPS = 1e-5
RMS_EPS = 1e-6
F_FLOOR = 1e-30
NEG_BIG = -1e30
LOG2E = math.log2(math.e)

C_GQ, C_GK, C_GV = 0, 256, 384
C_NQ, C_NK, C_NV = 512, 768, 1024
C_DQ, C_DK, C_DV = 1280, 1536, 1792
C_HG = 2048
D_HEADS_IN = 3328
HG_COLS = 1280

VMEM_LIMIT = 56 * 1024 * 1024


def _cparams(sem):
    return pltpu.CompilerParams(dimension_semantics=sem, vmem_limit_bytes=VMEM_LIMIT)


def _dot(a, b):
    return jnp.dot(a, b, preferred_element_type=F32)


def _dot_nt(a, b):
    return lax.dot_general(a, b, (((1,), (1,)), ((), ())), preferred_element_type=F32)


def _split_dot(x, j):
    hi = x.astype(BF16)
    lo = (x - hi.astype(F32)).astype(BF16)
    return _dot(hi, j) + _dot(lo, j)


def _sigmoid(z):
    return 1.0 / (1.0 + jnp.exp(-z))


def _ln_plain(x):
    mu = jnp.mean(x, axis=-1, keepdims=True)
    xc = x - mu
    var = jnp.mean(xc * xc, axis=-1, keepdims=True)
    return xc * lax.rsqrt(var + LN_EPS)


def _block_ones(n, blk):
    idx = np.arange(n) // blk
    return jnp.asarray((idx[:, None] == idx[None, :]).astype(np.float32), dtype=BF16)


def _ada_kernel(c_ref, w_ref, b_ref, o_ref):
    c = c_ref[...]
    s = (c * _sigmoid(c)).astype(BF16)
    o_ref[0] = _dot(s, w_ref[0].astype(BF16)) + b_ref[0]


def _ada_mods(cond, w_ada, b_ada):
    depth, d, n = w_ada.shape
    rows = cond.shape[0]
    tn = 1536
    return pl.pallas_call(
        _ada_kernel,
        grid=(depth, n // tn),
        in_specs=[
            pl.BlockSpec((rows, d), lambda l, j: (0, 0)),
            pl.BlockSpec((1, d, tn), lambda l, j: (l, 0, j)),
            pl.BlockSpec((1, 1, tn), lambda l, j: (l, 0, j)),
        ],
        out_specs=pl.BlockSpec((1, rows, tn), lambda l, j: (l, 0, j)),
        out_shape=jax.ShapeDtypeStruct((depth, rows, n), F32),
        compiler_params=_cparams(("parallel", "parallel")),
        name="ada_mods",
    )(cond, w_ada, b_ada.reshape(depth, 1, n))


def _rope_tables(t_len, d, reps):
    quarter = d // 4
    t = np.arange(t_len)
    inv = ROPE_THETA ** (-np.arange(quarter, dtype=np.float64) / quarter)
    ang_r = (t // GRID_W).astype(np.float64)[:, None] * inv
    ang_c = (t % GRID_W).astype(np.float64)[:, None] * inv
    ang = np.concatenate([ang_r, ang_r, ang_c, ang_c], axis=-1)
    cos, sin = np.cos(ang), np.sin(ang)
    first = (np.arange(d) % (2 * quarter)) < quarter
    sin_a = np.where(first, -sin, 0.0)
    sin_b = np.where(first, 0.0, sin)
    tile = lambda a: jnp.asarray(np.tile(a, (1, reps)), dtype=F32)
    return tile(cos), tile(sin_a), tile(sin_b)


def _rope(x, cos, sin_a, sin_b, quarter):
    w = x.shape[-1]
    return x * cos + pltpu.roll(x, w - quarter, 1) * sin_a + pltpu.roll(x, quarter, 1) * sin_b


def _seg_rms(x, j):
    ss = _split_dot(x * x, j)
    return x * lax.rsqrt(ss * (1.0 / HEAD_DIM) + RMS_EPS)


def _pad_heads(x, heads, fill):
    pad = jnp.full((x.shape[0], HEAD_DIM), fill, x.dtype)
    pieces = []
    for h in range(heads):
        pieces += [x[:, h * HEAD_DIM:(h + 1) * HEAD_DIM], pad]
    return jnp.concatenate(pieces, axis=-1)


def _inproj_kernel(*refs, latent):
    n_in = 13 if latent else 7
    ins, outs = refs[:n_in], refs[n_in:]
    x_ref, sh_ref, sc_ref, w_ref, qn_ref, kn_ref, j_ref = ins[:7]
    qa_ref, kat_ref, va_ref, qnb_ref, knb_ref, vnb_ref, qd_ref, kdt_ref, vd_ref, hg_ref = outs[:10]
    h = (_ln_plain(x_ref[0]) * (1.0 + sc_ref[0]) + sh_ref[0]).astype(BF16)
    proj = lambda c0, n: _dot(h, w_ref[:, c0:c0 + n])

    gq = _seg_rms(proj(C_GQ, 256), j_ref[...]) * qn_ref[...]
    gk = _seg_rms(proj(C_GK, 128), j_ref[0:128, 0:128]) * kn_ref[...]
    gv = proj(C_GV, 128)
    nq = proj(C_NQ, 256) * (LOG2E * HEAD_DIM ** -0.5)
    nk = proj(C_NK, 256)
    nv = proj(C_NV, 256)
    dq = proj(C_DQ, 256)
    dk = proj(C_DK, 256)
    dv = proj(C_DV, 256)
    if latent:
        c64, a64, b64, c32, a32, b32 = (r[...] for r in ins[7:13])
        gq = _rope(gq, c64, a64, b64, HEAD_DIM // 4)
        gk = _rope(gk, c64[:, 0:128], a64[:, 0:128], b64[:, 0:128], HEAD_DIM // 4)
        dq = _rope(dq, c32, a32, b32, DIFF_QK_DIM // 4)
        dk = _rope(dk, c32, a32, b32, DIFF_QK_DIM // 4)
    else:
        for ref, val in zip(outs[10:], (gk, gv, nk, nv, dk, dv)):
            ref[0] = val
    qa_ref[0] = (gq * (LOG2E * HEAD_DIM ** -0.5)).astype(BF16)
    kat_ref[0] = jnp.transpose(gk).astype(BF16)
    va_ref[0] = _pad_heads(gv, GQA_KV_HEADS, 1.0).astype(BF16)
    if latent:
        qnb_ref[0] = _pad_heads(nq, NA_HEADS, 0.0).astype(BF16)
        knb_ref[0] = _pad_heads(nk, NA_HEADS, 0.0).astype(BF16)
    else:
        qnb_ref[0] = nq.astype(BF16)
        knb_ref[0] = jnp.transpose(nk).astype(BF16)
    vnb_ref[0] = _pad_heads(nv, NA_HEADS, 1.0).astype(BF16)
    qd_ref[0] = (dq * (LOG2E * DIFF_QK_DIM ** -0.5)).astype(BF16)
    kdt_ref[0] = jnp.transpose(dk).astype(BF16)
    vd_ref[0] = _pad_heads(dv, DIFF_HEADS, 1.0).astype(BF16)
    for s in range(HG_COLS // 256):
        hg_ref[0, :, s * 256:(s + 1) * 256] = proj(C_HG + s * 256, 256)


def _inproj(x, shift, scale, w_heads, qn, kn, latent, tm=256):
    bsz, t_len, d = x.shape
    nb = shift.shape[0]
    mod_map = (lambda b, i: (b, 0, 0)) if nb > 1 else (lambda b, i: (0, 0, 0))
    tok = lambda n: pl.BlockSpec((1, tm, n), lambda b, i: (b, i, 0))
    tok_t = lambda n: pl.BlockSpec((1, n, tm), lambda b, i: (b, 0, i))
    const = lambda shape: pl.BlockSpec(shape, lambda b, i: (0,) * len(shape))
    in_specs = [tok(d), pl.BlockSpec((1, 1, d), mod_map), pl.BlockSpec((1, 1, d), mod_map),
                const((d, D_HEADS_IN)), const((1, 256)), const((1, 128)), const((256, 256))]
    args = [x, shift, scale, w_heads, jnp.tile(qn, GQA_Q_HEADS)[None], jnp.tile(kn, GQA_KV_HEADS)[None],
            _block_ones(256, HEAD_DIM)]
    if latent:
        in_specs += [pl.BlockSpec((tm, 256), lambda b, i: (i, 0))] * 6
        args += list(_rope_tables(t_len, HEAD_DIM, 4)) + list(_rope_tables(t_len, DIFF_QK_DIM, 8))
    rows = lambda n, dt=BF16: (jax.ShapeDtypeStruct((bsz, t_len, n), dt), tok(n))
    cols = lambda n: (jax.ShapeDtypeStruct((bsz, n, t_len), BF16), tok_t(n))
    outs = [rows(256), cols(128), rows(256),
            rows(512) if latent else rows(256), rows(512) if latent else cols(256), rows(512),
            rows(256), cols(256), rows(512), rows(HG_COLS, F32)]
    if not latent:
        outs += [rows(n, F32) for n in (128, 128, 256, 256, 256, 256)]
    return pl.pallas_call(
        functools.partial(_inproj_kernel, latent=latent),
        grid=(bsz, t_len // tm),
        in_specs=in_specs,
        out_specs=[o[1] for o in outs],
        out_shape=[o[0] for o in outs],
        compiler_params=_cparams(("parallel", "parallel")),
        name="in_proj_rope" if latent else "in_proj",
    )(*args)


def _softmax_p(s):
    return jnp.exp2(s - jnp.max(s, axis=-1, keepdims=True)).astype(BF16)


def _pv_norm(p, v_ext):
    dv = v_ext.shape[-1] // 2
    o = _dot(p, v_ext)
    return o[:, :dv] * (1.0 / o[:, dv:])


def _attn_kernel(q_ref, kt_ref, v_ref, o_ref, *, hq, group, d, dv):
    q = q_ref[0]
    tq = q.shape[0]
    n_groups = hq // group

    def scores(g):
        qs = jnp.concatenate([q[:, h * d:(h + 1) * d] for h in range(g * group, (g + 1) * group)], axis=0)
        return _dot(qs, kt_ref[0, g * d:(g + 1) * d, :])

    outs = []
    s_next = scores(0)
    for g in range(n_groups):
        s_cur, s_next = s_next, (scores(g + 1) if g + 1 < n_groups else None)
        o = _pv_norm(_softmax_p(s_cur), v_ref[0, :, g * 2 * dv:(g + 1) * 2 * dv])
        outs += [o[j * tq:(j + 1) * tq] for j in range(group)]
    o_ref[0] = jnp.concatenate(outs, axis=-1).astype(o_ref.dtype)


def _attention(q, kt, v_ext, hq, hkv, tq):
    bsz, t_len, qw = q.shape
    l_len = kt.shape[-1]
    d = kt.shape[1] // hkv
    dv = v_ext.shape[-1] // (2 * hkv)
    return pl.pallas_call(
        functools.partial(_attn_kernel, hq=hq, group=hq // hkv, d=d, dv=dv),
        grid=(bsz, t_len // tq),
        in_specs=[
            pl.BlockSpec((1, tq, qw), lambda b, i: (b, i, 0)),
            pl.BlockSpec((1, hkv * d, l_len), lambda b, i: (b, 0, 0)),
            pl.BlockSpec((1, l_len, hkv * 2 * dv), lambda b, i: (b, 0, 0)),
        ],
        out_specs=pl.BlockSpec((1, tq, hq * dv), lambda b, i: (b, i, 0)),
        out_shape=jax.ShapeDtypeStruct((bsz, t_len, hq * dv), BF16),
        compiler_params=_cparams(("parallel", "parallel")),
        name="attention",
    )(q, kt, v_ext)


def _diff_attn_kernel(q_ref, kt_ref, v_ref, par_ref, o_ref):
    q = q_ref[0]
    tq = q.shape[0]
    lam = par_ref[0:1, :]
    gain = par_ref[1:2, :]
    dq = DIFF_QK_DIM
    scores = lambda i: _dot(q[:, i * dq:(i + 1) * dq], kt_ref[0, i * dq:(i + 1) * dq, :])
    outs = []
    s_next = (scores(0), scores(1))
    for h in range(DIFF_HEADS):
        s_cur, s_next = s_next, ((scores(2 * h + 2), scores(2 * h + 3)) if h + 1 < DIFF_HEADS else None)
        o12 = _pv_norm(jnp.concatenate([_softmax_p(s_cur[0]), _softmax_p(s_cur[1])], axis=0),
                       v_ref[0, :, h * 2 * DIFF_V_DIM:(h + 1) * 2 * DIFF_V_DIM])
        o = o12[:tq] - lam * o12[tq:]
        ms = jnp.mean(o * o, axis=-1, keepdims=True)
        outs.append(o * lax.rsqrt(ms + RMS_EPS) * gain)
    o_ref[0] = jnp.concatenate(outs, axis=-1).astype(o_ref.dtype)


def _diff_attention(q, kt, v_ext, par, tq):
    bsz, t_len, qw = q.shape
    l_len = kt.shape[-1]
    return pl.pallas_call(
        _diff_attn_kernel,
        grid=(bsz, t_len // tq),
        in_specs=[
            pl.BlockSpec((1, tq, qw), lambda b, i: (b, i, 0)),
            pl.BlockSpec((1, 2 * DIFF_HEADS * DIFF_QK_DIM, l_len), lambda b, i: (b, 0, 0)),
            pl.BlockSpec((1, l_len, DIFF_HEADS * 2 * DIFF_V_DIM), lambda b, i: (b, 0, 0)),
            pl.BlockSpec((8, DIFF_V_DIM), lambda b, i: (0, 0)),
        ],
        out_specs=pl.BlockSpec((1, tq, DIFF_HEADS * DIFF_V_DIM), lambda b, i: (b, i, 0)),
        out_shape=jax.ShapeDtypeStruct((bsz, t_len, DIFF_HEADS * DIFF_V_DIM), BF16),
        compiler_params=_cparams(("parallel", "parallel")),
        name="diff_attention",
    )(q, kt, v_ext, par)


def _na_bias_table(rpb):
    shift = np.arange(NA_KH)[:, None]
    win_row = np.arange(NA_KH)[None, :]
    row_off = win_row - shift + (NA_KH - 1)
    col = np.arange(GRID_W)
    col_start = np.clip(col - NA_KW // 2, 0, GRID_W - NA_KW)
    key_col = np.arange(GRID_W)[None, :]
    valid = (key_col >= col_start[:, None]) & (key_col < col_start[:, None] + NA_KW)
    col_off = key_col - col[:, None] + (NA_KW - 1)
    row_sel = (row_off[:, :, None] == np.arange(2 * NA_KH - 1)).astype(np.float32)
    col_sel = ((col_off[:, :, None] == np.arange(2 * NA_KW - 1)) & valid[:, :, None]).astype(np.float32)
    tbl = jnp.einsum("sia,lhab,cxb->lshcix", row_sel, rpb.astype(F32) * LOG2E, col_sel,
                     precision=lax.Precision.HIGHEST)
    tbl = jnp.where(valid[None, None, None, :, None, :], tbl, NEG_BIG)
    return tbl.reshape(rpb.shape[0], NA_KH, rpb.shape[1], GRID_W, NA_KH * GRID_W)


def _na_kernel(q_ref, k_ref, v_ref, kc_ref, vc_ref, bias_ref, o_ref, *, rows_per_step, rows):
    i = pl.program_id(1)
    n_win = NA_KH * GRID_W
    units = [(rr, h) for rr in range(rows_per_step) for h in range(NA_HEADS)]

    def window(rr):
        r = i * rows_per_step + rr
        r_start = jnp.clip(r - NA_KH // 2, 0, rows - NA_KH)
        return r - r_start, pl.multiple_of(r_start * GRID_W, GRID_W)

    def scores(unit):
        rr, h = unit
        shift, start = window(rr)
        lanes = slice(h * 2 * HEAD_DIM, (h + 1) * 2 * HEAD_DIM)
        qh = q_ref[0, rr * GRID_W:(rr + 1) * GRID_W, lanes]
        s_loc = _dot_nt(qh, k_ref[0, pl.ds(start, n_win), lanes]) + bias_ref[shift, h]
        return s_loc, _dot_nt(qh, kc_ref[0, :, lanes])

    def output(unit, s_loc, s_ctx):
        rr, h = unit
        _, start = window(rr)
        m = jnp.maximum(jnp.max(s_loc, axis=-1, keepdims=True), jnp.max(s_ctx, axis=-1, keepdims=True))
        p_loc = jnp.exp2(s_loc - m).astype(BF16)
        p_ctx = jnp.exp2(s_ctx - m).astype(BF16)
        lanes = slice(h * 2 * HEAD_DIM, (h + 1) * 2 * HEAD_DIM)
        o = _dot(p_loc, v_ref[0, pl.ds(start, n_win), lanes]) + _dot(p_ctx, vc_ref[0, :, lanes])
        return o[:, :HEAD_DIM] * (1.0 / o[:, HEAD_DIM:])

    depth = 4
    staged = [scores(u) for u in units[:depth]]
    outs = []
    for n, unit in enumerate(units):
        cur = staged.pop(0)
        if n + depth < len(units):
            staged.append(scores(units[n + depth]))
        outs.append(output(unit, *cur))
        if unit[1] == NA_HEADS - 1:
            rr = unit[0]
            o_ref[0, rr * GRID_W:(rr + 1) * GRID_W, :] = jnp.concatenate(outs, axis=-1).astype(o_ref.dtype)
            outs = []


def _na_attention(q, k, v_ext, kc, vc_ext, bias, rows_per_step=8):
    bsz, n, qw = q.shape
    rows = n // GRID_W
    assert rows >= NA_KH and rows % rows_per_step == 0
    l_ctx = kc.shape[1]
    tq = rows_per_step * GRID_W
    full = lambda length: pl.BlockSpec((1, length, qw), lambda b, i: (b, 0, 0))
    return pl.pallas_call(
        functools.partial(_na_kernel, rows_per_step=rows_per_step, rows=rows),
        grid=(bsz, rows // rows_per_step),
        in_specs=[
            pl.BlockSpec((1, tq, qw), lambda b, i: (b, i, 0)),
            full(n), full(n), full(l_ctx), full(l_ctx),
            pl.BlockSpec(bias.shape, lambda b, i: (0, 0, 0, 0)),
        ],
        out_specs=pl.BlockSpec((1, tq, NA_HEADS * HEAD_DIM), lambda b, i: (b, i, 0)),
        out_shape=jax.ShapeDtypeStruct((bsz, n, NA_HEADS * HEAD_DIM), BF16),
        compiler_params=_cparams(("parallel", "parallel")),
        name="na_attention",
    )(q, k, v_ext, kc, vc_ext, bias)


HG_SUB = 16
HG_W = HG_HEADS * HG_DK


def _hgrn_direction(hg_ref, lb, tri, jv, bd_mask, st_ref, z_col, reverse, tc):
    c = HG_SUB
    n_sub = tc // c
    hq = hg_ref[0, :, 0:HG_W]
    q = hq * _sigmoid(hq)
    z = hg_ref[0, :, z_col:z_col + HG_W]
    v = hg_ref[0, :, 3 * HG_W:4 * HG_W]
    f = lb + (1.0 - lb) * _sigmoid(z)
    log_f = jnp.log(jnp.maximum(f, F_FLOOR))
    kk = (1.0 - lb) * _sigmoid(-z)
    b = jnp.dot(tri, log_f * LOG2E, preferred_element_type=F32, precision=lax.Precision.HIGHEST)
    st = st_ref[...]
    o_inter = _dot_nt((q * jnp.exp2(b)).astype(BF16), st.astype(BF16))
    b_end = b[0:1] if reverse else b[tc - 1:tc]
    upd = _dot(jnp.transpose(v).astype(BF16), (kk * jnp.exp2(b_end - b)).astype(BF16))
    st_ref[...] = jnp.exp2(b_end) * st + jnp.where(bd_mask, upd, 0.0)

    half = c // 2
    row = lax.broadcasted_iota(jnp.int32, (half, HG_W), 0)
    lane_head = lax.broadcasted_iota(jnp.int32, (c, HG_W), 1) // HG_DK
    v_bf = v.astype(BF16)
    zeros = jnp.zeros((half, HG_W), F32)
    def reach(s, top):
        if reverse:
            return ("all" if s >= half else "tri") if top else ("none" if s < half else "tri")
        return ("none" if s >= half else "tri") if top else ("all" if s < half else "tri")

    def score_stage(j):
        sl = slice(j * c, (j + 1) * c)
        bj, qj, kj = b[sl], q[sl], kk[sl]
        cj = bj - jnp.log2(kj)
        prods = []
        for s in range(c):
            for top in (True, False):
                kind = reach(s, top)
                if kind == "none":
                    prods.append(zeros)
                    continue
                hs = slice(0, half) if top else slice(half, c)
                decay_k = jnp.exp2(bj[hs] - cj[s:s + 1])
                if kind == "tri":
                    s_in = s % half
                    decay_k = jnp.where((row <= s_in) if reverse else (row >= s_in), decay_k, 0.0)
                prods.append(qj[hs] * decay_k)
        diag = _dot(jnp.concatenate(prods, axis=0).astype(BF16), jv)
        if reverse:
            keys, edge = slice((j + 1) * c, tc), (j + 1) * c
        else:
            keys, edge = slice(0, j * c), j * c - 1
        if keys.stop == keys.start:
            return diag, None, keys
        ref_b = b[edge:edge + 1]
        qd = qj * jnp.exp2(bj - ref_b)
        kd = (kk[keys] * jnp.exp2(ref_b - b[keys])).astype(BF16)
        qm = jnp.concatenate([jnp.where(lane_head == h, qd, 0.0) for h in range(HG_HEADS)], axis=0)
        return diag, _dot_nt(qm.astype(BF16), kd), keys

    def output_stage(j, diag, off, keys):
        sl = slice(j * c, (j + 1) * c)
        vj = v[sl]
        o_top, o_bot = o_inter[j * c:j * c + half], o_inter[j * c + half:(j + 1) * c]
        for s in range(c):
            if reach(s, True) != "none":
                o_top = o_top + diag[s * c:s * c + half] * vj[s:s + 1]
            if reach(s, False) != "none":
                o_bot = o_bot + diag[s * c + half:(s + 1) * c] * vj[s:s + 1]
        o = jnp.concatenate([o_top, o_bot], axis=0)
        if off is not None:
            ov = _dot(off.astype(BF16), v_bf[keys])
            for h in range(HG_HEADS):
                o = o + jnp.where(lane_head == h, ov[h * c:(h + 1) * c], 0.0)
        return o

    outs = []
    staged = score_stage(0)
    for j in range(n_sub):
        cur, staged = staged, (score_stage(j + 1) if j + 1 < n_sub else None)
        outs.append(output_stage(j, *cur))
    return jnp.concatenate(outs, axis=0)


def _hgrn_kernel(hg_ref, lb_ref, gn_ref, s0_ref, tri_ref, jv_ref, o_ref, sfin_ref, st_scr, of_scr, *, tc, nc):
    p = pl.program_id(1)
    i = pl.program_id(2)
    r_i = lax.broadcasted_iota(jnp.int32, (HG_W, HG_W), 0) // HG_DV
    c_i = lax.broadcasted_iota(jnp.int32, (HG_W, HG_W), 1) // HG_DK
    bd_mask = r_i == c_i
    jv = jv_ref[...]

    @pl.when(i == 0)
    def _():
        st_scr[...] = s0_ref[0, p]

    @pl.when(p == 0)
    def _():
        o = _hgrn_direction(hg_ref, lb_ref[0:1, :], tri_ref[0], jv, bd_mask, st_scr, HG_W, False, tc)
        of_scr[pl.ds(pl.multiple_of(i * tc, tc), tc), :] = o

    @pl.when(p == 1)
    def _():
        o_b = _hgrn_direction(hg_ref, lb_ref[1:2, :], tri_ref[1], jv, bd_mask, st_scr, 2 * HG_W, True, tc)
        o = of_scr[pl.ds(pl.multiple_of((nc - 1 - i) * tc, tc), tc), :] + o_b
        ms = _split_dot(o * o, jv) * (1.0 / HG_DV)
        g = hg_ref[0, :, 4 * HG_W:5 * HG_W]
        o_ref[0] = (o * lax.rsqrt(ms + RMS_EPS) * gn_ref[...] * (g * _sigmoid(g))).astype(o_ref.dtype)

    @pl.when(i == nc - 1)
    def _():
        sfin_ref[0, p] = st_scr[...]


def _hgrn(hg, lb2, gn, s0t, tc=256):
    bsz, t_len, _ = hg.shape
    nc = t_len // tc
    lower = np.arange(tc)[:, None] >= np.arange(tc)[None, :]
    tri = jnp.asarray(np.stack([lower, lower.T]).astype(np.float32))
    chunk = lambda b, p, i: (b, i + p * (nc - 1 - 2 * i), 0)
    o, sfin = pl.pallas_call(
        functools.partial(_hgrn_kernel, tc=tc, nc=nc),
        grid=(bsz, 2, nc),
        in_specs=[
            pl.BlockSpec((1, tc, HG_COLS), chunk),
            pl.BlockSpec((2, HG_W), lambda b, p, i: (0, 0)),
            pl.BlockSpec((1, HG_W), lambda b, p, i: (0, 0)),
            pl.BlockSpec((1, 2, HG_W, HG_W), lambda b, p, i: (b, 0, 0, 0)),
            pl.BlockSpec((2, tc, tc), lambda b, p, i: (0, 0, 0)),
            pl.BlockSpec((HG_W, HG_W), lambda b, p, i: (0, 0)),
        ],
        out_specs=[
            pl.BlockSpec((1, tc, HG_W), lambda b, p, i: (b, nc - 1 - i * p, 0)),
            pl.BlockSpec((1, 2, HG_W, HG_W), lambda b, p, i: (b, 0, 0, 0)),
        ],
        out_shape=[jax.ShapeDtypeStruct((bsz, t_len, HG_W), BF16),
                   jax.ShapeDtypeStruct((bsz, 2, HG_W, HG_W), F32)],
        scratch_shapes=[pltpu.VMEM((HG_W, HG_W), F32), pltpu.VMEM((t_len, HG_W), F32)],
        compiler_params=_cparams(("parallel", "arbitrary", "arbitrary")),
        name="hgrn",
    )(hg, lb2, jnp.tile(gn, HG_HEADS)[None], s0t, tri, _block_ones(HG_W, HG_DV))
    return o, sfin


def _state_to_blockdiag_t(s0):
    bsz = s0.shape[0]
    s0t = jnp.swapaxes(s0, -1, -2)
    eye = jnp.eye(HG_HEADS, dtype=s0.dtype)
    return jnp.einsum("bdhvk,hg->bdhvgk", s0t, eye).reshape(bsz, 2, HG_W, HG_W)


def _blockdiag_t_to_state(st):
    bsz = st.shape[0]
    s6 = st.reshape(bsz, 2, HG_HEADS, HG_DV, HG_HEADS, HG_DK)
    diag = jnp.stack([s6[:, :, h, :, h, :] for h in range(HG_HEADS)], axis=2)
    return jnp.swapaxes(diag, -1, -2)


def _post_kernel(x_ref, oa_ref, on_ref, od_ref, oh_ref, sh_ref, sc_ref, g1_ref, wg_ref, wb_ref, wo_ref,
                 lg_ref, lb_ref, o_ref, *, alpha):
    x = x_ref[0]
    h = (_ln_plain(x) * (1.0 + sc_ref[0]) + sh_ref[0]).astype(BF16)
    mixed = None
    for n, br_ref in enumerate((oa_ref, on_ref, od_ref, oh_ref)):
        gate = _sigmoid(_dot(h, wg_ref[:, n * D_MODEL:(n + 1) * D_MODEL]))
        term = gate * _dot(br_ref[0], wb_ref[n])
        mixed = term if mixed is None else mixed + term
    y = _dot(mixed.astype(BF16), wo_ref[...])
    o_ref[0] = _ln_plain(alpha * x + g1_ref[0] * y) * lg_ref[...] + lb_ref[...]


def _post_mixer(x, branches, shift, scale, gate, w_gates, w_branch, w_out, ln_g, ln_b, alpha, tm=256):
    bsz, t_len, d = x.shape
    nb = shift.shape[0]
    mod_map = (lambda b, i: (b, 0, 0)) if nb > 1 else (lambda b, i: (0, 0, 0))
    tok = lambda n: pl.BlockSpec((1, tm, n), lambda b, i: (b, i, 0))
    const = lambda shape: pl.BlockSpec(shape, lambda b, i: (0,) * len(shape))
    mod = pl.BlockSpec((1, 1, d), mod_map)
    return pl.pallas_call(
        functools.partial(_post_kernel, alpha=alpha),
        grid=(bsz, t_len // tm),
        in_specs=[tok(d)] + [tok(BRANCH_W)] * 4 + [mod, mod, mod,
                  const((d, N_BRANCH * d)), const((N_BRANCH, BRANCH_W, d)), const((d, d)),
                  const((1, d)), const((1, d))],
        out_specs=tok(d),
        out_shape=jax.ShapeDtypeStruct((bsz, t_len, d), F32),
        compiler_params=_cparams(("parallel", "parallel")),
        name="post_mixer",
    )(x, *branches, shift, scale, gate, w_gates, w_branch, w_out, ln_g[None], ln_b[None])


ROUTER_LANES = 128
MOE_BT = 1024
MOE_ROWS = 128
MOE_STEPS = N_EXPERTS // 2


def _router_weights(logits):
    lane = lax.broadcasted_iota(jnp.int32, logits.shape, 1)
    first_at = lambda mask: jnp.min(jnp.where(mask, lane, ROUTER_LANES), axis=-1, keepdims=True)
    is_g = lane < N_GROUPS
    gl = jnp.where(is_g, logits, NEG_BIG)
    g_max = jnp.max(gl, axis=-1, keepdims=True)
    g_idx = first_at(is_g & (gl == g_max))
    g_prob = 1.0 / jnp.sum(jnp.where(is_g, jnp.exp(gl - g_max), 0.0), axis=-1, keepdims=True)
    lo = N_GROUPS + g_idx * EXPERTS_PER_GROUP
    in_grp = (lane >= lo) & (lane < lo + EXPERTS_PER_GROUP)
    el = jnp.where(in_grp, logits, NEG_BIG)
    v1 = jnp.max(el, axis=-1, keepdims=True)
    i1 = first_at(in_grp & (el == v1))
    rest = in_grp & (lane != i1)
    el2 = jnp.where(rest, logits, NEG_BIG)
    v2 = jnp.max(el2, axis=-1, keepdims=True)
    i2 = first_at(rest & (el2 == v2))
    e2 = jnp.exp(v2 - v1)
    w1 = g_prob / (1.0 + e2)
    w2 = g_prob * e2 / (1.0 + e2)
    return g_idx, jnp.where(lane == i1, w1, jnp.where(lane == i2, w2, 0.0))


def _moe_kernel(x_ref, sh_ref, sc_ref, g2_ref, wr_ref, br_ref, lt_ref, wg_ref, wu_ref, wd_ref, lg_ref, lb_ref, o_ref,
                xs_scr, pt_scr, cws_scr, ys_scr, seg_ref, *, alpha):
    s = pl.program_id(2)
    bt = xs_scr.shape[0]
    rows = MOE_ROWS

    @pl.when(s == 0)
    def _():
        xm = _ln_plain(x_ref[0]) * (1.0 + sc_ref[0]) + sh_ref[0]
        x_hi = xm.astype(BF16)
        x_lo = (xm - x_hi.astype(F32)).astype(BF16)
        logits = _dot(x_hi, wr_ref[0]) + _dot(x_lo, wr_ref[0]) + _dot(x_hi, wr_ref[1]) + br_ref[...]
        g_idx, cw = _router_weights(logits)
        lane = lax.broadcasted_iota(jnp.int32, (bt, ROUTER_LANES), 1)
        in_g = lane == g_idx
        onehot = jnp.where(in_g, 1.0, 0.0)
        before = _dot(lt_ref[...], onehot.astype(BF16))
        count = jnp.sum(onehot, axis=0, keepdims=True)
        lane1 = lax.broadcasted_iota(jnp.int32, (1, ROUTER_LANES), 1)
        start = jnp.zeros((1, ROUTER_LANES), F32)
        for k in range(N_GROUPS - 1):
            start = start + jnp.where(lane1 > k, count[:, k:k + 1], 0.0)
        dest = jnp.sum(jnp.where(in_g, start + before, 0.0), axis=-1, keepdims=True)
        dest_row = jnp.transpose(jnp.broadcast_to(dest, (bt, ROUTER_LANES)))[0:1, :]
        pt_scr[...] = jnp.where(dest.astype(jnp.int32) == lax.broadcasted_iota(jnp.int32, (bt, bt), 1),
                                1.0, 0.0).astype(BF16)
        perm = jnp.where(dest_row.astype(jnp.int32) == lax.broadcasted_iota(jnp.int32, (bt, bt), 0),
                         1.0, 0.0).astype(BF16)
        xs_scr[...] = _dot(perm, x_hi).astype(BF16)
        c_hi = cw.astype(BF16)
        c_lo = (cw - c_hi.astype(F32)).astype(BF16)
        cws_scr[...] = _dot(perm, c_hi) + _dot(perm, c_lo)
        ys_scr[...] = jnp.zeros_like(ys_scr)
        for g in range(N_GROUPS):
            seg_ref[g] = jnp.sum(jnp.where(lane1 == g, start, 0.0)).astype(jnp.int32)
            seg_ref[N_GROUPS + g] = jnp.sum(jnp.where(lane1 == g, start + count, 0.0)).astype(jnp.int32)

    g = s // (MOE_STEPS // N_GROUPS)
    seg_lo = seg_ref[g]
    seg_hi = seg_ref[N_GROUPS + g]
    t_lo = seg_lo // rows
    t_hi = jnp.where(seg_hi > seg_lo, (seg_hi + rows - 1) // rows, t_lo)
    lane_r = lax.broadcasted_iota(jnp.int32, (rows, ROUTER_LANES), 1)

    def tile(t, carry):
        r0 = pl.multiple_of(t * rows, rows)
        xt = xs_scr[pl.ds(r0, rows), :]
        cwt = cws_scr[pl.ds(r0, rows), :]
        y = None
        for e in range(2):
            w_e = jnp.sum(jnp.where(lane_r == N_GROUPS + 2 * s + e, cwt, 0.0), axis=-1, keepdims=True)
            gate = _dot(xt, wg_ref[e])
            hid = (gate * _sigmoid(gate) * _dot(xt, wu_ref[e]) * w_e).astype(BF16)
            y_e = _dot(hid, wd_ref[e])
            y = y_e if y is None else y + y_e
        ys_scr[pl.ds(r0, rows), :] += y
        return carry

    lax.fori_loop(t_lo, t_hi, tile, 0)

    @pl.when(s == MOE_STEPS - 1)
    def _():
        ys = ys_scr[...]
        hi = ys.astype(BF16)
        lo = (ys - hi.astype(F32)).astype(BF16)
        pt = pt_scr[...]
        y = _dot(pt, hi) + _dot(pt, lo)
        o_ref[0] = _ln_plain(alpha * x_ref[0] + g2_ref[0] * y) * lg_ref[...] + lb_ref[...]


def _moe(x, shift, scale, gate, w_route, b_route, w_gate, w_up, w_down, ln_g, ln_b, alpha):
    bsz, t_len, d = x.shape
    bt = min(MOE_BT, t_len)
    assert t_len % bt == 0 and bt % MOE_ROWS == 0
    nb = shift.shape[0]
    mod_map = (lambda b, i, e: (b, 0, 0)) if nb > 1 else (lambda b, i, e: (0, 0, 0))
    tok = pl.BlockSpec((1, bt, d), lambda b, i, e: (b, i, 0))
    const = lambda shape: pl.BlockSpec(shape, lambda b, i, e: (0,) * len(shape))
    mod = pl.BlockSpec((1, 1, d), mod_map)
    pair = lambda shape: pl.BlockSpec((2,) + shape, lambda b, i, e: (e, 0, 0))
    lower = jnp.asarray(np.tril(np.ones((bt, bt), np.float32), -1), dtype=BF16)
    return pl.pallas_call(
        functools.partial(_moe_kernel, alpha=alpha),
        grid=(bsz, t_len // bt, MOE_STEPS),
        in_specs=[tok, mod, mod, mod, const((2, d, ROUTER_LANES)), const((1, ROUTER_LANES)), const((bt, bt)),
                  pair((d, D_EXPERT)), pair((d, D_EXPERT)), pair((D_EXPERT, d)), const((1, d)), const((1, d))],
        out_specs=tok,
        out_shape=jax.ShapeDtypeStruct((bsz, t_len, d), F32),
        scratch_shapes=[pltpu.VMEM((bt, d), BF16), pltpu.VMEM((bt, bt), BF16), pltpu.VMEM((bt, ROUTER_LANES), F32),
                        pltpu.VMEM((bt, d), F32), pltpu.SMEM((2 * N_GROUPS,), jnp.int32)],
        compiler_params=_cparams(("parallel", "parallel", "arbitrary")),
        name="moe",
    )(x, shift, scale, gate, w_route, b_route, lower, w_gate, w_up, w_down, ln_g[None], ln_b[None])


def _layer(x, mods, lp, ctx, alpha):
    bsz, t_len, _ = x.shape
    latent = ctx is not None
    shift1, scale1, gate1, shift2, scale2, gate2 = mods
    outs = _inproj(x, shift1, scale1, lp["w_heads"], lp["gqa_qn"], lp["gqa_kn"], latent)
    qa, kat, va, qn, kn, vn, qd, kdt, vd, hg = outs[:10]
    if latent:
        kat = jnp.concatenate([ctx["gk_t"], kat], axis=2)
        va = jnp.concatenate([ctx["gv_ext"], va], axis=1)
        kdt = jnp.concatenate([ctx["dk_t"], kdt], axis=2)
        vd = jnp.concatenate([ctx["dv_ext"], vd], axis=1)
        s0t = _state_to_blockdiag_t(ctx["state"].astype(F32))
        tq = 256
    else:
        s0t = jnp.zeros((bsz, 2, HG_W, HG_W), F32)
        tq = t_len
    o_a = _attention(qa, kat, va, GQA_Q_HEADS, GQA_KV_HEADS, tq)
    if latent:
        o_n = _na_attention(qn, kn, vn, ctx["nk_ext"], ctx["nv_ext"], lp["na_bias"])
    else:
        o_n = _attention(qn, kn, vn, NA_HEADS, NA_HEADS, tq)
    o_d = _diff_attention(qd, kdt, vd, lp["diff_par"], tq)
    o_h, sfin = _hgrn(hg, lp["hg_lb"], lp["hg_norm"], s0t)
    x1 = _post_mixer(x, (o_a, o_n, o_d, o_h), shift1, scale1, gate1, lp["w_gates"], lp["w_branch"], lp["w_out"],
                     lp["ln_mix_g"], lp["ln_mix_b"], alpha)
    x1r = x1 if latent else x1.reshape(1, bsz * t_len, D_MODEL)
    y = _moe(x1r, shift2, scale2, gate2, lp["w_route"], lp["b_route"], lp["w_gate"], lp["w_up"], lp["w_down"],
             lp["ln_ffn_g"], lp["ln_ffn_b"], alpha).reshape(bsz, t_len, D_MODEL)
    caches = None
    if not latent:
        ka, va32, kn32, vn32, kd32, vd32 = outs[10:]
        caches = (ka.reshape(bsz, t_len, GQA_KV_HEADS, HEAD_DIM), va32.reshape(bsz, t_len, GQA_KV_HEADS, HEAD_DIM),
                  kn32.reshape(bsz, t_len, NA_HEADS, HEAD_DIM), vn32.reshape(bsz, t_len, NA_HEADS, HEAD_DIM),
                  kd32.reshape(bsz, t_len, DIFF_HEADS, 2, DIFF_QK_DIM), vd32.reshape(bsz, t_len, DIFF_HEADS, DIFF_V_DIM),
                  _blockdiag_t_to_state(sfin))
    return y, caches


def _padded_heads(a, fill):
    a = a.astype(BF16)
    return jnp.concatenate([a, jnp.full(a.shape, fill, BF16)], axis=-1).reshape(a.shape[:3] + (-1,))


def _keys_t(a):
    a = a.astype(BF16).reshape(a.shape[:3] + (-1,))
    return jnp.swapaxes(a, 2, 3)


def kernel(x_prompt, x_sample, cache_gqa_k, cache_gqa_v, cache_na_k, cache_na_v, cache_diff_k, cache_diff_v, state_hgrn, c, c_ctx, w_ada, b_ada, w_in, gqa_q_norm, gqa_k_norm, na_rpb, diff_lambda, diff_subln, hgrn_lb, hgrn_norm, w_branch, w_out, ln_mix_g, ln_mix_b, ln_ffn_g, ln_ffn_b, w_group, b_group, w_router, b_router, w_gate, w_up, w_down):
    depth = w_in.shape[0]
    d = D_MODEL
    alpha = (2 * depth) ** 0.25
    dec_b = c.shape[0]

    lb = jax.nn.softmax(hgrn_lb.astype(F32), axis=1)
    lb = jnp.cumsum(lb, axis=1) - lb[:, :1]

    cond_rows = -(-(dec_b + 1) // 8) * 8
    cond = jnp.zeros((cond_rows, d), F32).at[:dec_b].set(c).at[dec_b].set(c_ctx)
    mods_all = _ada_mods(cond, w_ada, b_ada)

    na_bias = _na_bias_table(na_rpb)
    layers = []
    for l in range(depth):
        lam_init = 0.8 - 0.6 * math.exp(-0.3 * l)
        lp_lam = diff_lambda[l].astype(F32)
        lam = jnp.exp(jnp.sum(lp_lam[0] * lp_lam[1])) - jnp.exp(jnp.sum(lp_lam[2] * lp_lam[3])) + lam_init
        diff_par = jnp.zeros((8, DIFF_V_DIM), F32).at[0].set(lam).at[1].set(diff_subln[l] * (1.0 - lam_init))
        w_route = jnp.zeros((d, ROUTER_LANES), F32)
        w_route = w_route.at[:, :N_GROUPS].set(w_group[l]).at[:, N_GROUPS:N_GROUPS + N_EXPERTS].set(
            w_router[l].reshape(d, N_EXPERTS))
        w_route_hi = w_route.astype(BF16)
        b_route = jnp.zeros((1, ROUTER_LANES), F32)
        b_route = b_route.at[0, :N_GROUPS].set(b_group[l]).at[0, N_GROUPS:N_GROUPS + N_EXPERTS].set(
            b_router[l].reshape(N_EXPERTS))
        layers.append({
            "w_heads": w_in[l, :, :D_HEADS_IN].astype(BF16),
            "w_gates": w_in[l, :, D_HEADS_IN:].astype(BF16),
            "gqa_qn": gqa_q_norm[l], "gqa_kn": gqa_k_norm[l],
            "na_bias": na_bias[l],
            "diff_par": diff_par,
            "hg_lb": lb[:, l], "hg_norm": hgrn_norm[l],
            "w_branch": w_branch[l].astype(BF16), "w_out": w_out[l].astype(BF16),
            "ln_mix_g": ln_mix_g[l], "ln_mix_b": ln_mix_b[l], "ln_ffn_g": ln_ffn_g[l], "ln_ffn_b": ln_ffn_b[l],
            "w_route": jnp.stack([w_route_hi, (w_route - w_route_hi.astype(F32)).astype(BF16)]), "b_route": b_route,
            "w_gate": w_gate[l].reshape(N_EXPERTS, d, D_EXPERT).astype(BF16),
            "w_up": w_up[l].reshape(N_EXPERTS, d, D_EXPERT).astype(BF16),
            "w_down": w_down[l].reshape(N_EXPERTS, D_EXPERT, d).astype(BF16),
        })

    def split_mods(rows):
        return tuple(rows[:, None, k * d:(k + 1) * d] for k in range(6))

    xp = x_prompt
    ctx_out = []
    for l in range(depth):
        xp, caches = _layer(xp, split_mods(mods_all[l, dec_b:dec_b + 1]), layers[l], None, alpha)
        ctx_out.append(caches)
    stack = lambda i: jnp.stack([t[i] for t in ctx_out], axis=1)

    cached = {"gk_t": _keys_t(cache_gqa_k), "gv_ext": _padded_heads(cache_gqa_v, 1.0),
              "nk_ext": _padded_heads(cache_na_k, 0.0), "nv_ext": _padded_heads(cache_na_v, 1.0),
              "dk_t": _keys_t(cache_diff_k), "dv_ext": _padded_heads(cache_diff_v, 1.0), "state": state_hgrn}
    xs = x_sample
    for l in range(depth):
        ctx = {name: t[:, l] for name, t in cached.items()}
        xs, _ = _layer(xs, split_mods(mods_all[l, :dec_b]), layers[l], ctx, alpha)

    return (xp, xs, stack(0), stack(1), stack(2), stack(3), stack(4), stack(5), stack(6))
```

```python
import functools
import math

import jax
import jax.numpy as jnp
import numpy as np
from jax import lax
from jax.experimental import pallas as pl
from jax.experimental.pallas import tpu as pltpu

F32 = jnp.float32
BF16 = jnp.bfloat16

D_MODEL = 1024
GRID_W = 64
HEAD_DIM = 64
GQA_Q_HEADS = 4
GQA_KV_HEADS = 2
NA_HEADS = 4
NA_KH = 8
NA_KW = 16
DIFF_HEADS = 4
DIFF_QK_DIM = 32
DIFF_V_DIM = 64
HG_HEADS = 4
HG_DK = 64
HG_DV = 64
BRANCH_W = 256
N_BRANCH = 4
N_GROUPS = 4
EXPERTS_PER_GROUP = 4
N_EXPERTS = N_GROUPS * EXPERTS_PER_GROUP
D_EXPERT = 512
ROPE_THETA = 10000.0
LN_EPS = 1e-5
RMS_EPS = 1e-6
F_FLOOR = 1e-30
NEG_BIG = -1e30
LOG2E = math.log2(math.e)

C_GQ, C_GK, C_GV = 0, 256, 384
C_NQ, C_NK, C_NV = 512, 768, 1024
C_DQ, C_DK, C_DV = 1280, 1536, 1792
C_HG = 2048
D_HEADS_IN = 3328
HG_COLS = 1280

VMEM_LIMIT = 56 * 1024 * 1024


def _cparams(sem):
    return pltpu.CompilerParams(dimension_semantics=sem, vmem_limit_bytes=VMEM_LIMIT)


def _dot(a, b):
    return jnp.dot(a, b, preferred_element_type=F32)


def _dot_nt(a, b):
    return lax.dot_general(a, b, (((1,), (1,)), ((), ())), preferred_element_type=F32)


def _split_dot(x, j):
    hi = x.astype(BF16)
    lo = (x - hi.astype(F32)).astype(BF16)
    return _dot(hi, j) + _dot(lo, j)


def _sigmoid(z):
    return 1.0 / (1.0 + jnp.exp(-z))


def _ln_plain(x):
    mu = jnp.mean(x, axis=-1, keepdims=True)
    xc = x - mu
    var = jnp.mean(xc * xc, axis=-1, keepdims=True)
    return xc * lax.rsqrt(var + LN_EPS)


def _block_ones(n, blk):
    idx = np.arange(n) // blk
    return jnp.asarray((idx[:, None] == idx[None, :]).astype(np.float32), dtype=BF16)


def _ada_kernel(c_ref, w_ref, b_ref, o_ref):
    c = c_ref[...]
    s = (c * _sigmoid(c)).astype(BF16)
    o_ref[0] = _dot(s, w_ref[0].astype(BF16)) + b_ref[0]


def _ada_mods(cond, w_ada, b_ada):
    depth, d, n = w_ada.shape
    rows = cond.shape[0]
    tn = 1536
    return pl.pallas_call(
        _ada_kernel,
        grid=(depth, n // tn),
        in_specs=[
            pl.BlockSpec((rows, d), lambda l, j: (0, 0)),
            pl.BlockSpec((1, d, tn), lambda l, j: (l, 0, j)),
            pl.BlockSpec((1, 1, tn), lambda l, j: (l, 0, j)),
        ],
        out_specs=pl.BlockSpec((1, rows, tn), lambda l, j: (l, 0, j)),
        out_shape=jax.ShapeDtypeStruct((depth, rows, n), F32),
        compiler_params=_cparams(("parallel", "parallel")),
        name="ada_mods",
    )(cond, w_ada, b_ada.reshape(depth, 1, n))


def _rope_tables(t_len, d, reps):
    quarter = d // 4
    t = np.arange(t_len)
    inv = ROPE_THETA ** (-np.arange(quarter, dtype=np.float64) / quarter)
    ang_r = (t // GRID_W).astype(np.float64)[:, None] * inv
    ang_c = (t % GRID_W).astype(np.float64)[:, None] * inv
    ang = np.concatenate([ang_r, ang_r, ang_c, ang_c], axis=-1)
    cos, sin = np.cos(ang), np.sin(ang)
    first = (np.arange(d) % (2 * quarter)) < quarter
    sin_a = np.where(first, -sin, 0.0)
    sin_b = np.where(first, 0.0, sin)
    tile = lambda a: jnp.asarray(np.tile(a, (1, reps)), dtype=F32)
    return tile(cos), tile(sin_a), tile(sin_b)


def _rope(x, cos, sin_a, sin_b, quarter):
    w = x.shape[-1]
    return x * cos + pltpu.roll(x, w - quarter, 1) * sin_a + pltpu.roll(x, quarter, 1) * sin_b


def _seg_rms(x, j):
    ss = _split_dot(x * x, j)
    return x * lax.rsqrt(ss * (1.0 / HEAD_DIM) + RMS_EPS)


def _pad_heads(x, heads, fill):
    pad = jnp.full((x.shape[0], HEAD_DIM), fill, x.dtype)
    pieces = []
    for h in range(heads):
        pieces += [x[:, h * HEAD_DIM:(h + 1) * HEAD_DIM], pad]
    return jnp.concatenate(pieces, axis=-1)


def _inproj_kernel(*refs, latent):
    n_in = 13 if latent else 7
    ins, outs = refs[:n_in], refs[n_in:]
    x_ref, sh_ref, sc_ref, w_ref, qn_ref, kn_ref, j_ref = ins[:7]
    qa_ref, kat_ref, va_ref, qnb_ref, knb_ref, vnb_ref, qd_ref, kdt_ref, vd_ref, hg_ref = outs[:10]
    h = (_ln_plain(x_ref[0]) * (1.0 + sc_ref[0]) + sh_ref[0]).astype(BF16)
    proj = lambda c0, n: _dot(h, w_ref[:, c0:c0 + n])

    gq = _seg_rms(proj(C_GQ, 256), j_ref[...]) * qn_ref[...]
    gk = _seg_rms(proj(C_GK, 128), j_ref[0:128, 0:128]) * kn_ref[...]
    gv = proj(C_GV, 128)
    nq = proj(C_NQ, 256) * (LOG2E * HEAD_DIM ** -0.5)
    nk = proj(C_NK, 256)
    nv = proj(C_NV, 256)
    dq = proj(C_DQ, 256)
    dk = proj(C_DK, 256)
    dv = proj(C_DV, 256)
    if latent:
        c64, a64, b64, c32, a32, b32 = (r[...] for r in ins[7:13])
        gq = _rope(gq, c64, a64, b64, HEAD_DIM // 4)
        gk = _rope(gk, c64[:, 0:128], a64[:, 0:128], b64[:, 0:128], HEAD_DIM // 4)
        dq = _rope(dq, c32, a32, b32, DIFF_QK_DIM // 4)
        dk = _rope(dk, c32, a32, b32, DIFF_QK_DIM // 4)
    else:
        for ref, val in zip(outs[10:], (gk, gv, nk, nv, dk, dv)):
            ref[0] = val
    qa_ref[0] = (gq * (LOG2E * HEAD_DIM ** -0.5)).astype(BF16)
    kat_ref[0] = jnp.transpose(gk).astype(BF16)
    va_ref[0] = _pad_heads(gv, GQA_KV_HEADS, 1.0).astype(BF16)
    if latent:
        qnb_ref[0] = _pad_heads(nq, NA_HEADS, 0.0).astype(BF16)
        knb_ref[0] = _pad_heads(nk, NA_HEADS, 0.0).astype(BF16)
    else:
        qnb_ref[0] = nq.astype(BF16)
        knb_ref[0] = jnp.transpose(nk).astype(BF16)
    vnb_ref[0] = _pad_heads(nv, NA_HEADS, 1.0).astype(BF16)
    qd_ref[0] = (dq * (LOG2E * DIFF_QK_DIM ** -0.5)).astype(BF16)
    kdt_ref[0] = jnp.transpose(dk).astype(BF16)
    vd_ref[0] = _pad_heads(dv, DIFF_HEADS, 1.0).astype(BF16)
    for s in range(HG_COLS // 256):
        hg_ref[0, :, s * 256:(s + 1) * 256] = proj(C_HG + s * 256, 256)


def _inproj(x, shift, scale, w_heads, qn, kn, latent, past=0, tm=256):
    bsz, t_len, d = x.shape
    assert past % tm == 0
    skip = past // tm
    nb = shift.shape[0]
    mod_map = (lambda b, i: (b, 0, 0)) if nb > 1 else (lambda b, i: (0, 0, 0))
    tok = lambda n: pl.BlockSpec((1, tm, n), lambda b, i: (b, i, 0))
    tok_t = lambda n: pl.BlockSpec((1, n, tm), lambda b, i: (b, 0, i))
    const = lambda shape: pl.BlockSpec(shape, lambda b, i: (0,) * len(shape))
    in_specs = [tok(d), pl.BlockSpec((1, 1, d), mod_map), pl.BlockSpec((1, 1, d), mod_map),
                const((d, D_HEADS_IN)), const((1, 256)), const((1, 128)), const((256, 256))]
    args = [x, shift, scale, w_heads, jnp.tile(qn, GQA_Q_HEADS)[None], jnp.tile(kn, GQA_KV_HEADS)[None],
            _block_ones(256, HEAD_DIM)]
    if latent:
        in_specs += [pl.BlockSpec((tm, 256), lambda b, i: (i, 0))] * 6
        args += list(_rope_tables(t_len, HEAD_DIM, 4)) + list(_rope_tables(t_len, DIFF_QK_DIM, 8))
    rows = lambda n, dt=BF16: (jax.ShapeDtypeStruct((bsz, t_len, n), dt), tok(n))
    cols = lambda n: (jax.ShapeDtypeStruct((bsz, n, t_len), BF16), tok_t(n))
    rows_kv = lambda n: (jax.ShapeDtypeStruct((bsz, past + t_len, n), BF16),
                         pl.BlockSpec((1, tm, n), lambda b, i: (b, i + skip, 0)))
    cols_kv = lambda n: (jax.ShapeDtypeStruct((bsz, n, past + t_len), BF16),
                         pl.BlockSpec((1, n, tm), lambda b, i: (b, 0, i + skip)))
    outs = [rows(256), cols_kv(128), rows_kv(256),
            rows(512) if latent else rows(256), rows(512) if latent else cols(256), rows(512),
            rows(256), cols_kv(256), rows_kv(512), rows(HG_COLS, F32)]
    if not latent:
        outs += [rows(n, F32) for n in (128, 128, 256, 256, 256, 256)]
    return pl.pallas_call(
        functools.partial(_inproj_kernel, latent=latent),
        grid=(bsz, t_len // tm),
        in_specs=in_specs,
        out_specs=[o[1] for o in outs],
        out_shape=[o[0] for o in outs],
        compiler_params=_cparams(("parallel", "parallel")),
        name="in_proj_rope" if latent else "in_proj",
    )(*args)


def _softmax_p(s):
    return jnp.exp2(s - jnp.max(s, axis=-1, keepdims=True)).astype(BF16)


def _pv_norm(p, v_ext):
    dv = v_ext.shape[-1] // 2
    o = _dot(p, v_ext)
    return o[:, :dv] * (1.0 / o[:, dv:])


def _attn_kernel(q_ref, kt_ref, v_ref, o_ref, *, hq, group, d, dv):
    q = q_ref[0]
    tq = q.shape[0]
    n_groups = hq // group

    def scores(g):
        qs = jnp.concatenate([q[:, h * d:(h + 1) * d] for h in range(g * group, (g + 1) * group)], axis=0)
        return _dot(qs, kt_ref[0, g * d:(g + 1) * d, :])

    outs = []
    s_next = scores(0)
    for g in range(n_groups):
        s_cur, s_next = s_next, (scores(g + 1) if g + 1 < n_groups else None)
        o = _pv_norm(_softmax_p(s_cur), v_ref[0, :, g * 2 * dv:(g + 1) * 2 * dv])
        outs += [o[j * tq:(j + 1) * tq] for j in range(group)]
    o_ref[0] = jnp.concatenate(outs, axis=-1).astype(o_ref.dtype)


def _attention(q, kt, v_ext, hq, hkv, tq):
    bsz, t_len, qw = q.shape
    l_len = kt.shape[-1]
    d = kt.shape[1] // hkv
    dv = v_ext.shape[-1] // (2 * hkv)
    return pl.pallas_call(
        functools.partial(_attn_kernel, hq=hq, group=hq // hkv, d=d, dv=dv),
        grid=(bsz, t_len // tq),
        in_specs=[
            pl.BlockSpec((1, tq, qw), lambda b, i: (b, i, 0)),
            pl.BlockSpec((1, hkv * d, l_len), lambda b, i: (b, 0, 0)),
            pl.BlockSpec((1, l_len, hkv * 2 * dv), lambda b, i: (b, 0, 0)),
        ],
        out_specs=pl.BlockSpec((1, tq, hq * dv), lambda b, i: (b, i, 0)),
        out_shape=jax.ShapeDtypeStruct((bsz, t_len, hq * dv), BF16),
        compiler_params=_cparams(("parallel", "parallel")),
        name="attention",
    )(q, kt, v_ext)


def _diff_attn_kernel(q_ref, kt_ref, v_ref, par_ref, o_ref):
    q = q_ref[0]
    tq = q.shape[0]
    lam = par_ref[0:1, :]
    gain = par_ref[1:2, :]
    dq = DIFF_QK_DIM
    scores = lambda i: _dot(q[:, i * dq:(i + 1) * dq], kt_ref[0, i * dq:(i + 1) * dq, :])
    outs = []
    s_next = (scores(0), scores(1))
    for h in range(DIFF_HEADS):
        s_cur, s_next = s_next, ((scores(2 * h + 2), scores(2 * h + 3)) if h + 1 < DIFF_HEADS else None)
        o12 = _pv_norm(jnp.concatenate([_softmax_p(s_cur[0]), _softmax_p(s_cur[1])], axis=0),
                       v_ref[0, :, h * 2 * DIFF_V_DIM:(h + 1) * 2 * DIFF_V_DIM])
        o = o12[:tq] - lam * o12[tq:]
        ms = jnp.mean(o * o, axis=-1, keepdims=True)
        outs.append(o * lax.rsqrt(ms + RMS_EPS) * gain)
    o_ref[0] = jnp.concatenate(outs, axis=-1).astype(o_ref.dtype)


def _diff_attention(q, kt, v_ext, par, tq):
    bsz, t_len, qw = q.shape
    l_len = kt.shape[-1]
    return pl.pallas_call(
        _diff_attn_kernel,
        grid=(bsz, t_len // tq),
        in_specs=[
            pl.BlockSpec((1, tq, qw), lambda b, i: (b, i, 0)),
            pl.BlockSpec((1, 2 * DIFF_HEADS * DIFF_QK_DIM, l_len), lambda b, i: (b, 0, 0)),
            pl.BlockSpec((1, l_len, DIFF_HEADS * 2 * DIFF_V_DIM), lambda b, i: (b, 0, 0)),
            pl.BlockSpec((8, DIFF_V_DIM), lambda b, i: (0, 0)),
        ],
        out_specs=pl.BlockSpec((1, tq, DIFF_HEADS * DIFF_V_DIM), lambda b, i: (b, i, 0)),
        out_shape=jax.ShapeDtypeStruct((bsz, t_len, DIFF_HEADS * DIFF_V_DIM), BF16),
        compiler_params=_cparams(("parallel", "parallel")),
        name="diff_attention",
    )(q, kt, v_ext, par)


def _na_bias_table(rpb):
    shift = np.arange(NA_KH)[:, None]
    win_row = np.arange(NA_KH)[None, :]
    row_off = win_row - shift + (NA_KH - 1)
    col = np.arange(GRID_W)
    col_start = np.clip(col - NA_KW // 2, 0, GRID_W - NA_KW)
    key_col = np.arange(GRID_W)[None, :]
    valid = (key_col >= col_start[:, None]) & (key_col < col_start[:, None] + NA_KW)
    col_off = key_col - col[:, None] + (NA_KW - 1)
    row_sel = (row_off[:, :, None] == np.arange(2 * NA_KH - 1)).astype(np.float32)
    col_sel = ((col_off[:, :, None] == np.arange(2 * NA_KW - 1)) & valid[:, :, None]).astype(np.float32)
    tbl = jnp.einsum("sia,lhab,cxb->lshcix", row_sel, rpb.astype(F32) * LOG2E, col_sel,
                     precision=lax.Precision.HIGHEST)
    tbl = jnp.where(valid[None, None, None, :, None, :], tbl, NEG_BIG)
    return tbl.reshape(rpb.shape[0], NA_KH, rpb.shape[1], GRID_W, NA_KH * GRID_W)


def _na_kernel(q_ref, k_ref, v_ref, kc_ref, vc_ref, bias_ref, o_ref, *, rows_per_step, rows):
    i = pl.program_id(1)
    n_win = NA_KH * GRID_W
    units = [(rr, h) for rr in range(rows_per_step) for h in range(NA_HEADS)]

    def window(rr):
        r = i * rows_per_step + rr
        r_start = jnp.clip(r - NA_KH // 2, 0, rows - NA_KH)
        return r - r_start, pl.multiple_of(r_start * GRID_W, GRID_W)

    def scores(unit):
        rr, h = unit
        shift, start = window(rr)
        lanes = slice(h * 2 * HEAD_DIM, (h + 1) * 2 * HEAD_DIM)
        qh = q_ref[0, rr * GRID_W:(rr + 1) * GRID_W, lanes]
        s_loc = _dot_nt(qh, k_ref[0, pl.ds(start, n_win), lanes]) + bias_ref[shift, h]
        return s_loc, _dot_nt(qh, kc_ref[0, :, lanes])

    def output(unit, s_loc, s_ctx):
        rr, h = unit
        _, start = window(rr)
        m = jnp.maximum(jnp.max(s_loc, axis=-1, keepdims=True), jnp.max(s_ctx, axis=-1, keepdims=True))
        p_loc = jnp.exp2(s_loc - m).astype(BF16)
        p_ctx = jnp.exp2(s_ctx - m).astype(BF16)
        lanes = slice(h * 2 * HEAD_DIM, (h + 1) * 2 * HEAD_DIM)
        o = _dot(p_loc, v_ref[0, pl.ds(start, n_win), lanes]) + _dot(p_ctx, vc_ref[0, :, lanes])
        return o[:, :HEAD_DIM] * (1.0 / o[:, HEAD_DIM:])

    depth = 4
    staged = [scores(u) for u in units[:depth]]
    outs = []
    for n, unit in enumerate(units):
        cur = staged.pop(0)
        if n + depth < len(units):
            staged.append(scores(units[n + depth]))
        outs.append(output(unit, *cur))
        if unit[1] == NA_HEADS - 1:
            rr = unit[0]
            o_ref[0, rr * GRID_W:(rr + 1) * GRID_W, :] = jnp.concatenate(outs, axis=-1).astype(o_ref.dtype)
            outs = []


def _na_attention(q, k, v_ext, kc, vc_ext, bias, rows_per_step=8):
    bsz, n, qw = q.shape
    rows = n // GRID_W
    assert rows >= NA_KH and rows % rows_per_step == 0
    l_ctx = kc.shape[1]
    tq = rows_per_step * GRID_W
    full = lambda length: pl.BlockSpec((1, length, qw), lambda b, i: (b, 0, 0))
    return pl.pallas_call(
        functools.partial(_na_kernel, rows_per_step=rows_per_step, rows=rows),
        grid=(bsz, rows // rows_per_step),
        in_specs=[
            pl.BlockSpec((1, tq, qw), lambda b, i: (b, i, 0)),
            full(n), full(n), full(l_ctx), full(l_ctx),
            pl.BlockSpec(bias.shape, lambda b, i: (0, 0, 0, 0)),
        ],
        out_specs=pl.BlockSpec((1, tq, NA_HEADS * HEAD_DIM), lambda b, i: (b, i, 0)),
        out_shape=jax.ShapeDtypeStruct((bsz, n, NA_HEADS * HEAD_DIM), BF16),
        compiler_params=_cparams(("parallel", "parallel")),
        name="na_attention",
    )(q, k, v_ext, kc, vc_ext, bias)


HG_SUB = 16
HG_W = HG_HEADS * HG_DK


def _hgrn_direction(hg_ref, lb, tri, jv, bd_mask, st_ref, z_col, reverse, tc):
    c = HG_SUB
    n_sub = tc // c
    hq = hg_ref[0, :, 0:HG_W]
    q = hq * _sigmoid(hq)
    z = hg_ref[0, :, z_col:z_col + HG_W]
    v = hg_ref[0, :, 3 * HG_W:4 * HG_W]
    f = lb + (1.0 - lb) * _sigmoid(z)
    log_f = jnp.log(jnp.maximum(f, F_FLOOR))
    kk = (1.0 - lb) * _sigmoid(-z)
    b = jnp.dot(tri, log_f * LOG2E, preferred_element_type=F32, precision=lax.Precision.HIGHEST)
    st = st_ref[...]
    o_inter = _dot_nt((q * jnp.exp2(b)).astype(BF16), st.astype(BF16))
    b_end = b[0:1] if reverse else b[tc - 1:tc]
    upd = _dot(jnp.transpose(v).astype(BF16), (kk * jnp.exp2(b_end - b)).astype(BF16))
    st_ref[...] = jnp.exp2(b_end) * st + jnp.where(bd_mask, upd, 0.0)

    half = c // 2
    row = lax.broadcasted_iota(jnp.int32, (half, HG_W), 0)
    lane_head = lax.broadcasted_iota(jnp.int32, (c, HG_W), 1) // HG_DK
    v_bf = v.astype(BF16)
    zeros = jnp.zeros((half, HG_W), F32)
    def reach(s, top):
        if reverse:
            return ("all" if s >= half else "tri") if top else ("none" if s < half else "tri")
        return ("none" if s >= half else "tri") if top else ("all" if s < half else "tri")

    def score_stage(j):
        sl = slice(j * c, (j + 1) * c)
        bj, qj, kj = b[sl], q[sl], kk[sl]
        cj = bj - jnp.log2(kj)
        prods = []
        for s in range(c):
            for top in (True, False):
                kind = reach(s, top)
                if kind == "none":
                    prods.append(zeros)
                    continue
                hs = slice(0, half) if top else slice(half, c)
                decay_k = jnp.exp2(bj[hs] - cj[s:s + 1])
                if kind == "tri":
                    s_in = s % half
                    decay_k = jnp.where((row <= s_in) if reverse else (row >= s_in), decay_k, 0.0)
                prods.append(qj[hs] * decay_k)
        diag = _dot(jnp.concatenate(prods, axis=0).astype(BF16), jv)
        if reverse:
            keys, edge = slice((j + 1) * c, tc), (j + 1) * c
        else:
            keys, edge = slice(0, j * c), j * c - 1
        if keys.stop == keys.start:
            return diag, None, keys
        ref_b = b[edge:edge + 1]
        qd = qj * jnp.exp2(bj - ref_b)
        kd = (kk[keys] * jnp.exp2(ref_b - b[keys])).astype(BF16)
        qm = jnp.concatenate([jnp.where(lane_head == h, qd, 0.0) for h in range(HG_HEADS)], axis=0)
        return diag, _dot_nt(qm.astype(BF16), kd), keys

    def output_stage(j, diag, off, keys):
        sl = slice(j * c, (j + 1) * c)
        vj = v[sl]
        o_top, o_bot = o_inter[j * c:j * c + half], o_inter[j * c + half:(j + 1) * c]
        for s in range(c):
            if reach(s, True) != "none":
                o_top = o_top + diag[s * c:s * c + half] * vj[s:s + 1]
            if reach(s, False) != "none":
                o_bot = o_bot + diag[s * c + half:(s + 1) * c] * vj[s:s + 1]
        o = jnp.concatenate([o_top, o_bot], axis=0)
        if off is not None:
            ov = _dot(off.astype(BF16), v_bf[keys])
            for h in range(HG_HEADS):
                o = o + jnp.where(lane_head == h, ov[h * c:(h + 1) * c], 0.0)
        return o

    depth = 2
    staged = [score_stage(j) for j in range(min(depth, n_sub))]
    outs = []
    for j in range(n_sub):
        cur = staged.pop(0)
        if j + depth < n_sub:
            staged.append(score_stage(j + depth))
        outs.append(output_stage(j, *cur))
    return jnp.concatenate(outs, axis=0)


def _hgrn_kernel(hg_ref, lb_ref, gn_ref, s0_ref, tri_ref, jv_ref, o_ref, sfin_ref, st_scr, of_scr, *, tc, nc):
    p = pl.program_id(1)
    i = pl.program_id(2)
    r_i = lax.broadcasted_iota(jnp.int32, (HG_W, HG_W), 0) // HG_DV
    c_i = lax.broadcasted_iota(jnp.int32, (HG_W, HG_W), 1) // HG_DK
    bd_mask = r_i == c_i
    jv = jv_ref[...]

    @pl.when(i == 0)
    def _():
        st_scr[...] = s0_ref[0, p]

    @pl.when(p == 0)
    def _():
        o = _hgrn_direction(hg_ref, lb_ref[0:1, :], tri_ref[0], jv, bd_mask, st_scr, HG_W, False, tc)
        of_scr[pl.ds(pl.multiple_of(i * tc, tc), tc), :] = o

    @pl.when(p == 1)
    def _():
        o_b = _hgrn_direction(hg_ref, lb_ref[1:2, :], tri_ref[1], jv, bd_mask, st_scr, 2 * HG_W, True, tc)
        o = of_scr[pl.ds(pl.multiple_of((nc - 1 - i) * tc, tc), tc), :] + o_b
        ms = _split_dot(o * o, jv) * (1.0 / HG_DV)
        g = hg_ref[0, :, 4 * HG_W:5 * HG_W]
        o_ref[0] = (o * lax.rsqrt(ms + RMS_EPS) * gn_ref[...] * (g * _sigmoid(g))).astype(o_ref.dtype)

    @pl.when(i == nc - 1)
    def _():
        sfin_ref[0, p] = st_scr[...]


def _hgrn(hg, lb2, gn, s0t, tc=256):
    bsz, t_len, _ = hg.shape
    nc = t_len // tc
    lower = np.arange(tc)[:, None] >= np.arange(tc)[None, :]
    tri = jnp.asarray(np.stack([lower, lower.T]).astype(np.float32))
    chunk = lambda b, p, i: (b, i + p * (nc - 1 - 2 * i), 0)
    o, sfin = pl.pallas_call(
        functools.partial(_hgrn_kernel, tc=tc, nc=nc),
        grid=(bsz, 2, nc),
        in_specs=[
            pl.BlockSpec((1, tc, HG_COLS), chunk),
            pl.BlockSpec((2, HG_W), lambda b, p, i: (0, 0)),
            pl.BlockSpec((1, HG_W), lambda b, p, i: (0, 0)),
            pl.BlockSpec((1, 2, HG_W, HG_W), lambda b, p, i: (b, 0, 0, 0)),
            pl.BlockSpec((2, tc, tc), lambda b, p, i: (0, 0, 0)),
            pl.BlockSpec((HG_W, HG_W), lambda b, p, i: (0, 0)),
        ],
        out_specs=[
            pl.BlockSpec((1, tc, HG_W), lambda b, p, i: (b, nc - 1 - i * p, 0)),
            pl.BlockSpec((1, 2, HG_W, HG_W), lambda b, p, i: (b, 0, 0, 0)),
        ],
        out_shape=[jax.ShapeDtypeStruct((bsz, t_len, HG_W), BF16),
                   jax.ShapeDtypeStruct((bsz, 2, HG_W, HG_W), F32)],
        scratch_shapes=[pltpu.VMEM((HG_W, HG_W), F32), pltpu.VMEM((t_len, HG_W), F32)],
        compiler_params=_cparams(("parallel", "arbitrary", "arbitrary")),
        name="hgrn",
    )(hg, lb2, jnp.tile(gn, HG_HEADS)[None], s0t, tri, _block_ones(HG_W, HG_DV))
    return o, sfin


def _state_to_blockdiag_t(s0):
    bsz = s0.shape[0]
    s0t = jnp.swapaxes(s0, -1, -2)
    eye = jnp.eye(HG_HEADS, dtype=s0.dtype)
    return jnp.einsum("bdhvk,hg->bdhvgk", s0t, eye).reshape(bsz, 2, HG_W, HG_W)


def _blockdiag_t_to_state(st):
    bsz = st.shape[0]
    s6 = st.reshape(bsz, 2, HG_HEADS, HG_DV, HG_HEADS, HG_DK)
    diag = jnp.stack([s6[:, :, h, :, h, :] for h in range(HG_HEADS)], axis=2)
    return jnp.swapaxes(diag, -1, -2)


def _post_kernel(x_ref, oa_ref, on_ref, od_ref, oh_ref, sh_ref, sc_ref, g1_ref, wg_ref, wb_ref, wo_ref,
                 lg_ref, lb_ref, o_ref, *, alpha):
    x = x_ref[0]
    h = (_ln_plain(x) * (1.0 + sc_ref[0]) + sh_ref[0]).astype(BF16)
    mixed = None
    for n, br_ref in enumerate((oa_ref, on_ref, od_ref, oh_ref)):
        gate = _sigmoid(_dot(h, wg_ref[:, n * D_MODEL:(n + 1) * D_MODEL]))
        term = gate * _dot(br_ref[0], wb_ref[n])
        mixed = term if mixed is None else mixed + term
    y = _dot(mixed.astype(BF16), wo_ref[...])
    o_ref[0] = _ln_plain(alpha * x + g1_ref[0] * y) * lg_ref[...] + lb_ref[...]


def _post_mixer(x, branches, shift, scale, gate, w_gates, w_branch, w_out, ln_g, ln_b, alpha, tm=256):
    bsz, t_len, d = x.shape
    nb = shift.shape[0]
    mod_map = (lambda b, i: (b, 0, 0)) if nb > 1 else (lambda b, i: (0, 0, 0))
    tok = lambda n: pl.BlockSpec((1, tm, n), lambda b, i: (b, i, 0))
    const = lambda shape: pl.BlockSpec(shape, lambda b, i: (0,) * len(shape))
    mod = pl.BlockSpec((1, 1, d), mod_map)
    return pl.pallas_call(
        functools.partial(_post_kernel, alpha=alpha),
        grid=(bsz, t_len // tm),
        in_specs=[tok(d)] + [tok(BRANCH_W)] * 4 + [mod, mod, mod,
                  const((d, N_BRANCH * d)), const((N_BRANCH, BRANCH_W, d)), const((d, d)),
                  const((1, d)), const((1, d))],
        out_specs=tok(d),
        out_shape=jax.ShapeDtypeStruct((bsz, t_len, d), F32),
        compiler_params=_cparams(("parallel", "parallel")),
        name="post_mixer",
    )(x, *branches, shift, scale, gate, w_gates, w_branch, w_out, ln_g[None], ln_b[None])


ROUTER_LANES = 128
MOE_BT = 1024
MOE_ROWS = 128
MOE_STEPS = N_EXPERTS // 2


def _router_weights(logits):
    lane = lax.broadcasted_iota(jnp.int32, logits.shape, 1)
    first_at = lambda mask: jnp.min(jnp.where(mask, lane, ROUTER_LANES), axis=-1, keepdims=True)
    is_g = lane < N_GROUPS
    gl = jnp.where(is_g, logits, NEG_BIG)
    g_max = jnp.max(gl, axis=-1, keepdims=True)
    g_idx = first_at(is_g & (gl == g_max))
    g_prob = 1.0 / jnp.sum(jnp.where(is_g, jnp.exp(gl - g_max), 0.0), axis=-1, keepdims=True)
    lo = N_GROUPS + g_idx * EXPERTS_PER_GROUP
    in_grp = (lane >= lo) & (lane < lo + EXPERTS_PER_GROUP)
    el = jnp.where(in_grp, logits, NEG_BIG)
    v1 = jnp.max(el, axis=-1, keepdims=True)
    i1 = first_at(in_grp & (el == v1))
    rest = in_grp & (lane != i1)
    el2 = jnp.where(rest, logits, NEG_BIG)
    v2 = jnp.max(el2, axis=-1, keepdims=True)
    i2 = first_at(rest & (el2 == v2))
    e2 = jnp.exp(v2 - v1)
    w1 = g_prob / (1.0 + e2)
    w2 = g_prob * e2 / (1.0 + e2)
    return g_idx, jnp.where(lane == i1, w1, jnp.where(lane == i2, w2, 0.0))


def _moe_kernel(x_ref, sh_ref, sc_ref, g2_ref, wr_ref, br_ref, lt_ref, wg_ref, wu_ref, wd_ref, lg_ref, lb_ref, o_ref,
                xs_scr, pt_scr, cws_scr, ys_scr, seg_ref, *, alpha):
    s = pl.program_id(2)
    bt = xs_scr.shape[0]
    rows = MOE_ROWS

    @pl.when(s == 0)
    def _():
        xm = _ln_plain(x_ref[0]) * (1.0 + sc_ref[0]) + sh_ref[0]
        x_hi = xm.astype(BF16)
        x_lo = (xm - x_hi.astype(F32)).astype(BF16)
        logits = _dot(x_hi, wr_ref[0]) + _dot(x_lo, wr_ref[0]) + _dot(x_hi, wr_ref[1]) + br_ref[...]
        g_idx, cw = _router_weights(logits)
        lane = lax.broadcasted_iota(jnp.int32, (bt, ROUTER_LANES), 1)
        in_g = lane == g_idx
        onehot = jnp.where(in_g, 1.0, 0.0)
        before = _dot(lt_ref[...], onehot.astype(BF16))
        count = jnp.sum(onehot, axis=0, keepdims=True)
        lane1 = lax.broadcasted_iota(jnp.int32, (1, ROUTER_LANES), 1)
        start = jnp.zeros((1, ROUTER_LANES), F32)
        for k in range(N_GROUPS - 1):
            start = start + jnp.where(lane1 > k, count[:, k:k + 1], 0.0)
        dest = jnp.sum(jnp.where(in_g, start + before, 0.0), axis=-1, keepdims=True)
        dest_row = jnp.transpose(jnp.broadcast_to(dest, (bt, ROUTER_LANES)))[0:1, :]
        pt_scr[...] = jnp.where(dest.astype(jnp.int32) == lax.broadcasted_iota(jnp.int32, (bt, bt), 1),
                                1.0, 0.0).astype(BF16)
        perm = jnp.where(dest_row.astype(jnp.int32) == lax.broadcasted_iota(jnp.int32, (bt, bt), 0),
                         1.0, 0.0).astype(BF16)
        xs_scr[...] = _dot(perm, x_hi).astype(BF16)
        c_hi = cw.astype(BF16)
        c_lo = (cw - c_hi.astype(F32)).astype(BF16)
        cws_scr[...] = _dot(perm, c_hi) + _dot(perm, c_lo)
        ys_scr[...] = jnp.zeros_like(ys_scr)
        for g in range(N_GROUPS):
            seg_ref[g] = jnp.sum(jnp.where(lane1 == g, start, 0.0)).astype(jnp.int32)
            seg_ref[N_GROUPS + g] = jnp.sum(jnp.where(lane1 == g, start + count, 0.0)).astype(jnp.int32)

    g = s // (MOE_STEPS // N_GROUPS)
    seg_lo = seg_ref[g]
    seg_hi = seg_ref[N_GROUPS + g]
    t_lo = seg_lo // rows
    t_hi = jnp.where(seg_hi > seg_lo, (seg_hi + rows - 1) // rows, t_lo)
    lane_r = lax.broadcasted_iota(jnp.int32, (rows, ROUTER_LANES), 1)

    def gate_up(t):
        xt = xs_scr[pl.ds(pl.multiple_of(t * rows, rows), rows), :]
        return [(_dot(xt, wg_ref[e]), _dot(xt, wu_ref[e])) for e in range(2)]

    def down(t, gu):
        r0 = pl.multiple_of(t * rows, rows)
        cwt = cws_scr[pl.ds(r0, rows), :]
        hids = []
        for e, (gate, up) in enumerate(gu):
            w_e = jnp.sum(jnp.where(lane_r == N_GROUPS + 2 * s + e, cwt, 0.0), axis=-1, keepdims=True)
            hids.append((gate * _sigmoid(gate) * up * w_e).astype(BF16))
        ys_scr[pl.ds(r0, rows), :] += _dot(hids[0], wd_ref[0]) + _dot(hids[1], wd_ref[1])

    def tile_pair(i, carry):
        t = t_lo + 2 * i
        gu_a, gu_b = gate_up(t), gate_up(t + 1)
        down(t, gu_a)
        down(t + 1, gu_b)
        return carry

    n_tiles = t_hi - t_lo
    lax.fori_loop(0, n_tiles // 2, tile_pair, 0)

    @pl.when(n_tiles % 2 == 1)
    def _():
        down(t_hi - 1, gate_up(t_hi - 1))

    @pl.when(s == MOE_STEPS - 1)
    def _():
        ys = ys_scr[...]
        hi = ys.astype(BF16)
        lo = (ys - hi.astype(F32)).astype(BF16)
        pt = pt_scr[...]
        y = _dot(pt, hi) + _dot(pt, lo)
        o_ref[0] = _ln_plain(alpha * x_ref[0] + g2_ref[0] * y) * lg_ref[...] + lb_ref[...]


def _moe(x, shift, scale, gate, w_route, b_route, w_gate, w_up, w_down, ln_g, ln_b, alpha):
    bsz, t_len, d = x.shape
    bt = min(MOE_BT, t_len)
    assert t_len % bt == 0 and bt % MOE_ROWS == 0
    nb = shift.shape[0]
    mod_map = (lambda b, i, e: (b, 0, 0)) if nb > 1 else (lambda b, i, e: (0, 0, 0))
    tok = pl.BlockSpec((1, bt, d), lambda b, i, e: (b, i, 0))
    const = lambda shape: pl.BlockSpec(shape, lambda b, i, e: (0,) * len(shape))
    mod = pl.BlockSpec((1, 1, d), mod_map)
    pair = lambda shape: pl.BlockSpec((2,) + shape, lambda b, i, e: (e, 0, 0))
    lower = jnp.asarray(np.tril(np.ones((bt, bt), np.float32), -1), dtype=BF16)
    return pl.pallas_call(
        functools.partial(_moe_kernel, alpha=alpha),
        grid=(bsz, t_len // bt, MOE_STEPS),
        in_specs=[tok, mod, mod, mod, const((2, d, ROUTER_LANES)), const((1, ROUTER_LANES)), const((bt, bt)),
                  pair((d, D_EXPERT)), pair((d, D_EXPERT)), pair((D_EXPERT, d)), const((1, d)), const((1, d))],
        out_specs=tok,
        out_shape=jax.ShapeDtypeStruct((bsz, t_len, d), F32),
        scratch_shapes=[pltpu.VMEM((bt, d), BF16), pltpu.VMEM((bt, bt), BF16), pltpu.VMEM((bt, ROUTER_LANES), F32),
                        pltpu.VMEM((bt, d), F32), pltpu.SMEM((2 * N_GROUPS,), jnp.int32)],
        compiler_params=_cparams(("parallel", "parallel", "arbitrary")),
        name="moe",
    )(x, shift, scale, gate, w_route, b_route, lower, w_gate, w_up, w_down, ln_g[None], ln_b[None])


def _layer(x, mods, lp, ctx, alpha):
    bsz, t_len, _ = x.shape
    latent = ctx is not None
    shift1, scale1, gate1, shift2, scale2, gate2 = mods
    past = ctx["gk_t"].shape[-1] if latent else 0
    outs = _inproj(x, shift1, scale1, lp["w_heads"], lp["gqa_qn"], lp["gqa_kn"], latent, past)
    qa, kat, va, qn, kn, vn, qd, kdt, vd, hg = outs[:10]
    if latent:
        kat = lax.dynamic_update_slice(kat, ctx["gk_t"], (0, 0, 0))
        va = lax.dynamic_update_slice(va, ctx["gv_ext"], (0, 0, 0))
        kdt = lax.dynamic_update_slice(kdt, ctx["dk_t"], (0, 0, 0))
        vd = lax.dynamic_update_slice(vd, ctx["dv"], (0, 0, 0))
        s0t = _state_to_blockdiag_t(ctx["state"].astype(F32))
        tq = 256
    else:
        s0t = jnp.zeros((bsz, 2, HG_W, HG_W), F32)
        tq = t_len
    o_a = _attention(qa, kat, va, GQA_Q_HEADS, GQA_KV_HEADS, tq)
    if latent:
        o_n = _na_attention(qn, kn, vn, ctx["nk_ext"], ctx["nv_ext"], lp["na_bias"])
    else:
        o_n = _attention(qn, kn, vn, NA_HEADS, NA_HEADS, tq)
    o_d = _diff_attention(qd, kdt, vd, lp["diff_par"], tq)
    o_h, sfin = _hgrn(hg, lp["hg_lb"], lp["hg_norm"], s0t)
    x1 = _post_mixer(x, (o_a, o_n, o_d, o_h), shift1, scale1, gate1, lp["w_gates"], lp["w_branch"], lp["w_out"],
                     lp["ln_mix_g"], lp["ln_mix_b"], alpha)
    x1r = x1 if latent else x1.reshape(1, bsz * t_len, D_MODEL)
    y = _moe(x1r, shift2, scale2, gate2, lp["w_route"], lp["b_route"], lp["w_gate"], lp["w_up"], lp["w_down"],
             lp["ln_ffn_g"], lp["ln_ffn_b"], alpha).reshape(bsz, t_len, D_MODEL)
    caches = None
    if not latent:
        ka, va32, kn32, vn32, kd32, vd32 = outs[10:]
        caches = (ka.reshape(bsz, t_len, GQA_KV_HEADS, HEAD_DIM), va32.reshape(bsz, t_len, GQA_KV_HEADS, HEAD_DIM),
                  kn32.reshape(bsz, t_len, NA_HEADS, HEAD_DIM), vn32.reshape(bsz, t_len, NA_HEADS, HEAD_DIM),
                  kd32.reshape(bsz, t_len, DIFF_HEADS, 2, DIFF_QK_DIM), vd32.reshape(bsz, t_len, DIFF_HEADS, DIFF_V_DIM),
                  _blockdiag_t_to_state(sfin))
    return y, caches


def _padded_heads(a, fill):
    a = a.astype(BF16)
    return jnp.concatenate([a, jnp.full(a.shape, fill, BF16)], axis=-1).reshape(a.shape[:3] + (-1,))


def _keys_t(a):
    a = a.astype(BF16).reshape(a.shape[:3] + (-1,))
    return jnp.swapaxes(a, 2, 3)


def kernel(x_prompt, x_sample, cache_gqa_k, cache_gqa_v, cache_na_k, cache_na_v, cache_diff_k, cache_diff_v, state_hgrn, c, c_ctx, w_ada, b_ada, w_in, gqa_q_norm, gqa_k_norm, na_rpb, diff_lambda, diff_subln, hgrn_lb, hgrn_norm, w_branch, w_out, ln_mix_g, ln_mix_b, ln_ffn_g, ln_ffn_b, w_group, b_group, w_router, b_router, w_gate, w_up, w_down):
    depth = w_in.shape[0]
    d = D_MODEL
    alpha = (2 * depth) ** 0.25
    dec_b = c.shape[0]

    lb = jax.nn.softmax(hgrn_lb.astype(F32), axis=1)
    lb = jnp.cumsum(lb, axis=1) - lb[:, :1]

    cond_rows = -(-(dec_b + 1) // 8) * 8
    cond = jnp.zeros((cond_rows, d), F32).at[:dec_b].set(c).at[dec_b].set(c_ctx)
    mods_all = _ada_mods(cond, w_ada, b_ada)

    na_bias = _na_bias_table(na_rpb)
    layers = []
    for l in range(depth):
        lam_init = 0.8 - 0.6 * math.exp(-0.3 * l)
        lp_lam = diff_lambda[l].astype(F32)
        lam = jnp.exp(jnp.sum(lp_lam[0] * lp_lam[1])) - jnp.exp(jnp.sum(lp_lam[2] * lp_lam[3])) + lam_init
        diff_par = jnp.zeros((8, DIFF_V_DIM), F32).at[0].set(lam).at[1].set(diff_subln[l] * (1.0 - lam_init))
        w_route = jnp.zeros((d, ROUTER_LANES), F32)
        w_route = w_route.at[:, :N_GROUPS].set(w_group[l]).at[:, N_GROUPS:N_GROUPS + N_EXPERTS].set(
            w_router[l].reshape(d, N_EXPERTS))
        w_route_hi = w_route.astype(BF16)
        b_route = jnp.zeros((1, ROUTER_LANES), F32)
        b_route = b_route.at[0, :N_GROUPS].set(b_group[l]).at[0, N_GROUPS:N_GROUPS + N_EXPERTS].set(
            b_router[l].reshape(N_EXPERTS))
        layers.append({
            "w_heads": w_in[l, :, :D_HEADS_IN].astype(BF16),
            "w_gates": w_in[l, :, D_HEADS_IN:].astype(BF16),
            "gqa_qn": gqa_q_norm[l], "gqa_kn": gqa_k_norm[l],
            "na_bias": na_bias[l],
            "diff_par": diff_par,
            "hg_lb": lb[:, l], "hg_norm": hgrn_norm[l],
            "w_branch": w_branch[l].astype(BF16), "w_out": w_out[l].astype(BF16),
            "ln_mix_g": ln_mix_g[l], "ln_mix_b": ln_mix_b[l], "ln_ffn_g": ln_ffn_g[l], "ln_ffn_b": ln_ffn_b[l],
            "w_route": jnp.stack([w_route_hi, (w_route - w_route_hi.astype(F32)).astype(BF16)]), "b_route": b_route,
            "w_gate": w_gate[l].reshape(N_EXPERTS, d, D_EXPERT).astype(BF16),
            "w_up": w_up[l].reshape(N_EXPERTS, d, D_EXPERT).astype(BF16),
            "w_down": w_down[l].reshape(N_EXPERTS, D_EXPERT, d).astype(BF16),
        })

    def split_mods(rows):
        return tuple(rows[:, None, k * d:(k + 1) * d] for k in range(6))

    xp = x_prompt
    ctx_out = []
    for l in range(depth):
        xp, caches = _layer(xp, split_mods(mods_all[l, dec_b:dec_b + 1]), layers[l], None, alpha)
        ctx_out.append(caches)
    stack = lambda i: jnp.stack([t[i] for t in ctx_out], axis=1)

    cached = {"gk_t": _keys_t(cache_gqa_k), "gv_ext": _padded_heads(cache_gqa_v, 1.0),
              "nk_ext": _padded_heads(cache_na_k, 0.0), "nv_ext": _padded_heads(cache_na_v, 1.0),
              "dk_t": _keys_t(cache_diff_k), "dv": _padded_heads(cache_diff_v, 1.0), "state": state_hgrn}
    xs = x_sample
    for l in range(depth):
        ctx = {name: t[:, l] for name, t in cached.items()}
        xs, _ = _layer(xs, split_mods(mods_all[l, :dec_b]), layers[l], ctx, alpha)

    return (xp, xs, stack(0), stack(1), stack(2), stack(3), stack(4), stack(5), stack(6))
```

```python
import functools
import math

import jax
import jax.numpy as jnp
import numpy as np
from jax import lax
from jax.experimental import pallas as pl
from jax.experimental.pallas import tpu as pltpu

F32 = jnp.float32
BF16 = jnp.bfloat16

D_MODEL = 1024
GRID_W = 64
HEAD_DIM = 64
GQA_Q_HEADS = 4
GQA_KV_HEADS = 2
NA_HEADS = 4
NA_KH = 8
NA_KW = 16
DIFF_HEADS = 4
DIFF_QK_DIM = 32
DIFF_V_DIM = 64
HG_HEADS = 4
HG_DK = 64
HG_DV = 64
BRANCH_W = 256
N_BRANCH = 4
N_GROUPS = 4
EXPERTS_PER_GROUP = 4
N_EXPERTS = N_GROUPS * EXPERTS_PER_GROUP
D_EXPERT = 512
ROPE_THETA = 10000.0
LN_EPS = 1e-5
RMS_EPS = 1e-6
F_FLOOR = 1e-30
NEG_BIG = -1e30
LOG2E = math.log2(math.e)

C_GQ, C_GK, C_GV = 0, 256, 384
C_NQ, C_NK, C_NV = 512, 768, 1024
C_DQ, C_DK, C_DV = 1280, 1536, 1792
C_HG = 2048
D_HEADS_IN = 3328
HG_COLS = 1280

VMEM_LIMIT = 56 * 1024 * 1024


def _cparams(sem):
    return pltpu.CompilerParams(dimension_semantics=sem, vmem_limit_bytes=VMEM_LIMIT)


def _dot(a, b):
    return jnp.dot(a, b, preferred_element_type=F32)


def _dot_nt(a, b):
    return lax.dot_general(a, b, (((1,), (1,)), ((), ())), preferred_element_type=F32)


def _split_dot(x, j):
    hi = x.astype(BF16)
    lo = (x - hi.astype(F32)).astype(BF16)
    return _dot(hi, j) + _dot(lo, j)


def _sigmoid(z):
    return 1.0 / (1.0 + jnp.exp(-z))


def _ln_plain(x):
    mu = jnp.mean(x, axis=-1, keepdims=True)
    xc = x - mu
    var = jnp.mean(xc * xc, axis=-1, keepdims=True)
    return xc * lax.rsqrt(var + LN_EPS)


def _block_ones(n, blk):
    idx = np.arange(n) // blk
    return jnp.asarray((idx[:, None] == idx[None, :]).astype(np.float32), dtype=BF16)


def _ada_kernel(c_ref, w_ref, b_ref, o_ref):
    c = c_ref[...]
    s = (c * _sigmoid(c)).astype(BF16)
    o_ref[0] = _dot(s, w_ref[0].astype(BF16)) + b_ref[0]


def _ada_mods(cond, w_ada, b_ada):
    depth, d, n = w_ada.shape
    rows = cond.shape[0]
    tn = 1536
    return pl.pallas_call(
        _ada_kernel,
        grid=(depth, n // tn),
        in_specs=[
            pl.BlockSpec((rows, d), lambda l, j: (0, 0)),
            pl.BlockSpec((1, d, tn), lambda l, j: (l, 0, j)),
            pl.BlockSpec((1, 1, tn), lambda l, j: (l, 0, j)),
        ],
        out_specs=pl.BlockSpec((1, rows, tn), lambda l, j: (l, 0, j)),
        out_shape=jax.ShapeDtypeStruct((depth, rows, n), F32),
        compiler_params=_cparams(("parallel", "parallel")),
        name="ada_mods",
    )(cond, w_ada, b_ada.reshape(depth, 1, n))


def _rope_tables(t_len, d, reps):
    quarter = d // 4
    t = np.arange(t_len)
    inv = ROPE_THETA ** (-np.arange(quarter, dtype=np.float64) / quarter)
    ang_r = (t // GRID_W).astype(np.float64)[:, None] * inv
    ang_c = (t % GRID_W).astype(np.float64)[:, None] * inv
    ang = np.concatenate([ang_r, ang_r, ang_c, ang_c], axis=-1)
    cos, sin = np.cos(ang), np.sin(ang)
    first = (np.arange(d) % (2 * quarter)) < quarter
    sin_a = np.where(first, -sin, 0.0)
    sin_b = np.where(first, 0.0, sin)
    tile = lambda a: jnp.asarray(np.tile(a, (1, reps)), dtype=F32)
    return tile(cos), tile(sin_a), tile(sin_b)


def _rope(x, cos, sin_a, sin_b, quarter):
    w = x.shape[-1]
    return x * cos + pltpu.roll(x, w - quarter, 1) * sin_a + pltpu.roll(x, quarter, 1) * sin_b


def _seg_rms(x, j):
    ss = _split_dot(x * x, j)
    return x * lax.rsqrt(ss * (1.0 / HEAD_DIM) + RMS_EPS)


def _pad_heads(x, heads, fill):
    pad = jnp.full((x.shape[0], HEAD_DIM), fill, x.dtype)
    pieces = []
    for h in range(heads):
        pieces += [x[:, h * HEAD_DIM:(h + 1) * HEAD_DIM], pad]
    return jnp.concatenate(pieces, axis=-1)


def _inproj_kernel(*refs, latent):
    n_in = 13 if latent else 7
    ins, outs = refs[:n_in], refs[n_in:]
    x_ref, sh_ref, sc_ref, w_ref, qn_ref, kn_ref, j_ref = ins[:7]
    qa_ref, kat_ref, va_ref, qnb_ref, knb_ref, vnb_ref, qd_ref, kdt_ref, vd_ref, hg_ref = outs[:10]
    h = (_ln_plain(x_ref[0]) * (1.0 + sc_ref[0]) + sh_ref[0]).astype(BF16)
    proj = lambda c0, n: _dot(h, w_ref[:, c0:c0 + n])

    gq = _seg_rms(proj(C_GQ, 256), j_ref[...]) * qn_ref[...]
    gk = _seg_rms(proj(C_GK, 128), j_ref[0:128, 0:128]) * kn_ref[...]
    gv = proj(C_GV, 128)
    nq = proj(C_NQ, 256) * (LOG2E * HEAD_DIM ** -0.5)
    nk = proj(C_NK, 256)
    nv = proj(C_NV, 256)
    dq = proj(C_DQ, 256)
    dk = proj(C_DK, 256)
    dv = proj(C_DV, 256)
    if latent:
        c64, a64, b64, c32, a32, b32 = (r[...] for r in ins[7:13])
        gq = _rope(gq, c64, a64, b64, HEAD_DIM // 4)
        gk = _rope(gk, c64[:, 0:128], a64[:, 0:128], b64[:, 0:128], HEAD_DIM // 4)
        dq = _rope(dq, c32, a32, b32, DIFF_QK_DIM // 4)
        dk = _rope(dk, c32, a32, b32, DIFF_QK_DIM // 4)
    else:
        for ref, val in zip(outs[10:], (gk, gv, nk, nv, dk, dv)):
            ref[0] = val
    qa_ref[0] = (gq * (LOG2E * HEAD_DIM ** -0.5)).astype(BF16)
    kat_ref[0] = jnp.transpose(gk).astype(BF16)
    va_ref[0] = _pad_heads(gv, GQA_KV_HEADS, 1.0).astype(BF16)
    if latent:
        qnb_ref[0] = _pad_heads(nq, NA_HEADS, 0.0).astype(BF16)
        knb_ref[0] = _pad_heads(nk, NA_HEADS, 0.0).astype(BF16)
    else:
        qnb_ref[0] = nq.astype(BF16)
        knb_ref[0] = jnp.transpose(nk).astype(BF16)
    vnb_ref[0] = _pad_heads(nv, NA_HEADS, 1.0).astype(BF16)
    qd_ref[0] = (dq * (LOG2E * DIFF_QK_DIM ** -0.5)).astype(BF16)
    kdt_ref[0] = jnp.transpose(dk).astype(BF16)
    vd_ref[0] = _pad_heads(dv, DIFF_HEADS, 1.0).astype(BF16)
    for s in range(HG_COLS // 256):
        hg_ref[0, :, s * 256:(s + 1) * 256] = proj(C_HG + s * 256, 256)


def _inproj(x, shift, scale, w_heads, qn, kn, latent, past=0, tm=256):
    bsz, t_len, d = x.shape
    assert past % tm == 0
    skip = past // tm
    nb = shift.shape[0]
    mod_map = (lambda b, i: (b, 0, 0)) if nb > 1 else (lambda b, i: (0, 0, 0))
    tok = lambda n: pl.BlockSpec((1, tm, n), lambda b, i: (b, i, 0))
    tok_t = lambda n: pl.BlockSpec((1, n, tm), lambda b, i: (b, 0, i))
    const = lambda shape: pl.BlockSpec(shape, lambda b, i: (0,) * len(shape))
    in_specs = [tok(d), pl.BlockSpec((1, 1, d), mod_map), pl.BlockSpec((1, 1, d), mod_map),
                const((d, D_HEADS_IN)), const((1, 256)), const((1, 128)), const((256, 256))]
    args = [x, shift, scale, w_heads, jnp.tile(qn, GQA_Q_HEADS)[None], jnp.tile(kn, GQA_KV_HEADS)[None],
            _block_ones(256, HEAD_DIM)]
    if latent:
        in_specs += [pl.BlockSpec((tm, 256), lambda b, i: (i, 0))] * 6
        args += list(_rope_tables(t_len, HEAD_DIM, 4)) + list(_rope_tables(t_len, DIFF_QK_DIM, 8))
    rows = lambda n, dt=BF16: (jax.ShapeDtypeStruct((bsz, t_len, n), dt), tok(n))
    cols = lambda n: (jax.ShapeDtypeStruct((bsz, n, t_len), BF16), tok_t(n))
    rows_kv = lambda n: (jax.ShapeDtypeStruct((bsz, past + t_len, n), BF16),
                         pl.BlockSpec((1, tm, n), lambda b, i: (b, i + skip, 0)))
    cols_kv = lambda n: (jax.ShapeDtypeStruct((bsz, n, past + t_len), BF16),
                         pl.BlockSpec((1, n, tm), lambda b, i: (b, 0, i + skip)))
    outs = [rows(256), cols_kv(128), rows_kv(256),
            rows(512) if latent else rows(256), rows(512) if latent else cols(256), rows(512),
            rows(256), cols_kv(256), rows_kv(512), rows(HG_COLS, F32)]
    if not latent:
        outs += [rows(n, F32) for n in (128, 128, 256, 256, 256, 256)]
    return pl.pallas_call(
        functools.partial(_inproj_kernel, latent=latent),
        grid=(bsz, t_len // tm),
        in_specs=in_specs,
        out_specs=[o[1] for o in outs],
        out_shape=[o[0] for o in outs],
        compiler_params=_cparams(("parallel", "parallel")),
        name="in_proj_rope" if latent else "in_proj",
    )(*args)


def _softmax_p(s):
    return jnp.exp2(s - jnp.max(s, axis=-1, keepdims=True)).astype(BF16)


def _pv_norm(p, v_ext):
    dv = v_ext.shape[-1] // 2
    o = _dot(p, v_ext)
    return o[:, :dv] * (1.0 / o[:, dv:])


def _attn_kernel(q_ref, kt_ref, v_ref, o_ref, *, hq, group, d, dv):
    q = q_ref[0]
    tq = q.shape[0]
    n_groups = hq // group

    def scores(g):
        qs = jnp.concatenate([q[:, h * d:(h + 1) * d] for h in range(g * group, (g + 1) * group)], axis=0)
        return _dot(qs, kt_ref[0, g * d:(g + 1) * d, :])

    outs = []
    s_next = scores(0)
    for g in range(n_groups):
        s_cur, s_next = s_next, (scores(g + 1) if g + 1 < n_groups else None)
        o = _pv_norm(_softmax_p(s_cur), v_ref[0, :, g * 2 * dv:(g + 1) * 2 * dv])
        outs += [o[j * tq:(j + 1) * tq] for j in range(group)]
    o_ref[0] = jnp.concatenate(outs, axis=-1).astype(o_ref.dtype)


def _attention(q, kt, v_ext, hq, hkv, tq):
    bsz, t_len, qw = q.shape
    l_len = kt.shape[-1]
    d = kt.shape[1] // hkv
    dv = v_ext.shape[-1] // (2 * hkv)
    return pl.pallas_call(
        functools.partial(_attn_kernel, hq=hq, group=hq // hkv, d=d, dv=dv),
        grid=(bsz, t_len // tq),
        in_specs=[
            pl.BlockSpec((1, tq, qw), lambda b, i: (b, i, 0)),
            pl.BlockSpec((1, hkv * d, l_len), lambda b, i: (b, 0, 0)),
            pl.BlockSpec((1, l_len, hkv * 2 * dv), lambda b, i: (b, 0, 0)),
        ],
        out_specs=pl.BlockSpec((1, tq, hq * dv), lambda b, i: (b, i, 0)),
        out_shape=jax.ShapeDtypeStruct((bsz, t_len, hq * dv), BF16),
        compiler_params=_cparams(("parallel", "parallel")),
        name="attention",
    )(q, kt, v_ext)


def _diff_attn_kernel(q_ref, kt_ref, v_ref, par_ref, o_ref):
    q = q_ref[0]
    tq = q.shape[0]
    lam = par_ref[0:1, :]
    gain = par_ref[1:2, :]
    dq = DIFF_QK_DIM
    scores = lambda i: _dot(q[:, i * dq:(i + 1) * dq], kt_ref[0, i * dq:(i + 1) * dq, :])
    outs = []
    s_next = (scores(0), scores(1))
    for h in range(DIFF_HEADS):
        s_cur, s_next = s_next, ((scores(2 * h + 2), scores(2 * h + 3)) if h + 1 < DIFF_HEADS else None)
        o12 = _pv_norm(jnp.concatenate([_softmax_p(s_cur[0]), _softmax_p(s_cur[1])], axis=0),
                       v_ref[0, :, h * 2 * DIFF_V_DIM:(h + 1) * 2 * DIFF_V_DIM])
        o = o12[:tq] - lam * o12[tq:]
        ms = jnp.mean(o * o, axis=-1, keepdims=True)
        outs.append(o * lax.rsqrt(ms + RMS_EPS) * gain)
    o_ref[0] = jnp.concatenate(outs, axis=-1).astype(o_ref.dtype)


def _diff_attention(q, kt, v_ext, par, tq):
    bsz, t_len, qw = q.shape
    l_len = kt.shape[-1]
    return pl.pallas_call(
        _diff_attn_kernel,
        grid=(bsz, t_len // tq),
        in_specs=[
            pl.BlockSpec((1, tq, qw), lambda b, i: (b, i, 0)),
            pl.BlockSpec((1, 2 * DIFF_HEADS * DIFF_QK_DIM, l_len), lambda b, i: (b, 0, 0)),
            pl.BlockSpec((1, l_len, DIFF_HEADS * 2 * DIFF_V_DIM), lambda b, i: (b, 0, 0)),
            pl.BlockSpec((8, DIFF_V_DIM), lambda b, i: (0, 0)),
        ],
        out_specs=pl.BlockSpec((1, tq, DIFF_HEADS * DIFF_V_DIM), lambda b, i: (b, i, 0)),
        out_shape=jax.ShapeDtypeStruct((bsz, t_len, DIFF_HEADS * DIFF_V_DIM), BF16),
        compiler_params=_cparams(("parallel", "parallel")),
        name="diff_attention",
    )(q, kt, v_ext, par)


def _na_bias_table(rpb):
    shift = np.arange(NA_KH)[:, None]
    win_row = np.arange(NA_KH)[None, :]
    row_off = win_row - shift + (NA_KH - 1)
    col = np.arange(GRID_W)
    col_start = np.clip(col - NA_KW // 2, 0, GRID_W - NA_KW)
    key_col = np.arange(GRID_W)[None, :]
    valid = (key_col >= col_start[:, None]) & (key_col < col_start[:, None] + NA_KW)
    col_off = key_col - col[:, None] + (NA_KW - 1)
    row_sel = (row_off[:, :, None] == np.arange(2 * NA_KH - 1)).astype(np.float32)
    col_sel = ((col_off[:, :, None] == np.arange(2 * NA_KW - 1)) & valid[:, :, None]).astype(np.float32)
    tbl = jnp.einsum("sia,lhab,cxb->lshcix", row_sel, rpb.astype(F32) * LOG2E, col_sel,
                     precision=lax.Precision.HIGHEST)
    tbl = jnp.where(valid[None, None, None, :, None, :], tbl, NEG_BIG)
    return tbl.reshape(rpb.shape[0], NA_KH, rpb.shape[1], GRID_W, NA_KH * GRID_W)


def _na_kernel(q_ref, k_ref, v_ref, kc_ref, vc_ref, bias_ref, o_ref, *, rows_per_step, rows):
    i = pl.program_id(1)
    n_win = NA_KH * GRID_W
    units = [(rr, h) for rr in range(rows_per_step) for h in range(NA_HEADS)]

    def window(rr):
        r = i * rows_per_step + rr
        r_start = jnp.clip(r - NA_KH // 2, 0, rows - NA_KH)
        return r - r_start, pl.multiple_of(r_start * GRID_W, GRID_W)

    def scores(unit):
        rr, h = unit
        shift, start = window(rr)
        lanes = slice(h * 2 * HEAD_DIM, (h + 1) * 2 * HEAD_DIM)
        qh = q_ref[0, rr * GRID_W:(rr + 1) * GRID_W, lanes]
        s_loc = _dot_nt(qh, k_ref[0, pl.ds(start, n_win), lanes]) + bias_ref[shift, h]
        return s_loc, _dot_nt(qh, kc_ref[0, :, lanes])

    def output(unit, s_loc, s_ctx):
        rr, h = unit
        _, start = window(rr)
        m = jnp.maximum(jnp.max(s_loc, axis=-1, keepdims=True), jnp.max(s_ctx, axis=-1, keepdims=True))
        p_loc = jnp.exp2(s_loc - m).astype(BF16)
        p_ctx = jnp.exp2(s_ctx - m).astype(BF16)
        lanes = slice(h * 2 * HEAD_DIM, (h + 1) * 2 * HEAD_DIM)
        o = _dot(p_loc, v_ref[0, pl.ds(start, n_win), lanes]) + _dot(p_ctx, vc_ref[0, :, lanes])
        return o[:, :HEAD_DIM] * (1.0 / o[:, HEAD_DIM:])

    depth = 4
    staged = [scores(u) for u in units[:depth]]
    outs = []
    for n, unit in enumerate(units):
        cur = staged.pop(0)
        if n + depth < len(units):
            staged.append(scores(units[n + depth]))
        outs.append(output(unit, *cur))
        if unit[1] == NA_HEADS - 1:
            rr = unit[0]
            o_ref[0, rr * GRID_W:(rr + 1) * GRID_W, :] = jnp.concatenate(outs, axis=-1).astype(o_ref.dtype)
            outs = []


def _na_attention(q, k, v_ext, kc, vc_ext, bias, rows_per_step=8):
    bsz, n, qw = q.shape
    rows = n // GRID_W
    assert rows >= NA_KH and rows % rows_per_step == 0
    l_ctx = kc.shape[1]
    tq = rows_per_step * GRID_W
    full = lambda length: pl.BlockSpec((1, length, qw), lambda b, i: (b, 0, 0))
    return pl.pallas_call(
        functools.partial(_na_kernel, rows_per_step=rows_per_step, rows=rows),
        grid=(bsz, rows // rows_per_step),
        in_specs=[
            pl.BlockSpec((1, tq, qw), lambda b, i: (b, i, 0)),
            full(n), full(n), full(l_ctx), full(l_ctx),
            pl.BlockSpec(bias.shape, lambda b, i: (0, 0, 0, 0)),
        ],
        out_specs=pl.BlockSpec((1, tq, NA_HEADS * HEAD_DIM), lambda b, i: (b, i, 0)),
        out_shape=jax.ShapeDtypeStruct((bsz, n, NA_HEADS * HEAD_DIM), BF16),
        compiler_params=_cparams(("parallel", "parallel")),
        name="na_attention",
    )(q, k, v_ext, kc, vc_ext, bias)


HG_SUB = 16
HG_W = HG_HEADS * HG_DK


def _hgrn_direction(hg_ref, lb, tri, jv, bd_mask, st_ref, z_col, reverse, tc):
    c = HG_SUB
    n_sub = tc // c
    hq = hg_ref[0, :, 0:HG_W]
    q = hq * _sigmoid(hq)
    z = hg_ref[0, :, z_col:z_col + HG_W]
    v = hg_ref[0, :, 3 * HG_W:4 * HG_W]
    f = lb + (1.0 - lb) * _sigmoid(z)
    log_f = jnp.log(jnp.maximum(f, F_FLOOR))
    kk = (1.0 - lb) * _sigmoid(-z)
    b = jnp.dot(tri, log_f * LOG2E, preferred_element_type=F32, precision=lax.Precision.HIGHEST)
    st = st_ref[...]
    o_inter = _dot_nt((q * jnp.exp2(b)).astype(BF16), st.astype(BF16))
    b_end = b[0:1] if reverse else b[tc - 1:tc]
    upd = _dot(jnp.transpose(v).astype(BF16), (kk * jnp.exp2(b_end - b)).astype(BF16))
    st_ref[...] = jnp.exp2(b_end) * st + jnp.where(bd_mask, upd, 0.0)

    half = c // 2
    row = lax.broadcasted_iota(jnp.int32, (half, HG_W), 0)
    lane_head = lax.broadcasted_iota(jnp.int32, (c, HG_W), 1) // HG_DK
    v_bf = v.astype(BF16)
    zeros = jnp.zeros((half, HG_W), F32)
    def reach(s, top):
        if reverse:
            return ("all" if s >= half else "tri") if top else ("none" if s < half else "tri")
        return ("none" if s >= half else "tri") if top else ("all" if s < half else "tri")

    def score_stage(j):
        sl = slice(j * c, (j + 1) * c)
        bj, qj, kj = b[sl], q[sl], kk[sl]
        cj = bj - jnp.log2(kj)
        prods = []
        for s in range(c):
            for top in (True, False):
                kind = reach(s, top)
                if kind == "none":
                    prods.append(zeros)
                    continue
                hs = slice(0, half) if top else slice(half, c)
                decay_k = jnp.exp2(bj[hs] - cj[s:s + 1])
                if kind == "tri":
                    s_in = s % half
                    decay_k = jnp.where((row <= s_in) if reverse else (row >= s_in), decay_k, 0.0)
                prods.append(qj[hs] * decay_k)
        diag = _dot(jnp.concatenate(prods, axis=0).astype(BF16), jv)
        if reverse:
            keys, edge = slice((j + 1) * c, tc), (j + 1) * c
        else:
            keys, edge = slice(0, j * c), j * c - 1
        if keys.stop == keys.start:
            return diag, None, keys
        ref_b = b[edge:edge + 1]
        qd = qj * jnp.exp2(bj - ref_b)
        kd = (kk[keys] * jnp.exp2(ref_b - b[keys])).astype(BF16)
        qm = jnp.concatenate([jnp.where(lane_head == h, qd, 0.0) for h in range(HG_HEADS)], axis=0)
        return diag, _dot_nt(qm.astype(BF16), kd), keys

    def output_stage(j, diag, off, keys):
        sl = slice(j * c, (j + 1) * c)
        vj = v[sl]
        o_top, o_bot = o_inter[j * c:j * c + half], o_inter[j * c + half:(j + 1) * c]
        for s in range(c):
            if reach(s, True) != "none":
                o_top = o_top + diag[s * c:s * c + half] * vj[s:s + 1]
            if reach(s, False) != "none":
                o_bot = o_bot + diag[s * c + half:(s + 1) * c] * vj[s:s + 1]
        o = jnp.concatenate([o_top, o_bot], axis=0)
        if off is not None:
            ov = _dot(off.astype(BF16), v_bf[keys])
            for h in range(HG_HEADS):
                o = o + jnp.where(lane_head == h, ov[h * c:(h + 1) * c], 0.0)
        return o

    depth = 2
    staged = [score_stage(j) for j in range(min(depth, n_sub))]
    outs = []
    for j in range(n_sub):
        cur = staged.pop(0)
        if j + depth < n_sub:
            staged.append(score_stage(j + depth))
        outs.append(output_stage(j, *cur))
    return jnp.concatenate(outs, axis=0)


def _hgrn_kernel(hg_ref, lb_ref, gn_ref, s0_ref, tri_ref, jv_ref, o_ref, sfin_ref, st_scr, of_scr, *, tc, nc):
    p = pl.program_id(1)
    i = pl.program_id(2)
    r_i = lax.broadcasted_iota(jnp.int32, (HG_W, HG_W), 0) // HG_DV
    c_i = lax.broadcasted_iota(jnp.int32, (HG_W, HG_W), 1) // HG_DK
    bd_mask = r_i == c_i
    jv = jv_ref[...]

    @pl.when(i == 0)
    def _():
        st_scr[...] = s0_ref[0, p]

    @pl.when(p == 0)
    def _():
        o = _hgrn_direction(hg_ref, lb_ref[0:1, :], tri_ref[0], jv, bd_mask, st_scr, HG_W, False, tc)
        of_scr[pl.ds(pl.multiple_of(i * tc, tc), tc), :] = o

    @pl.when(p == 1)
    def _():
        o_b = _hgrn_direction(hg_ref, lb_ref[1:2, :], tri_ref[1], jv, bd_mask, st_scr, 2 * HG_W, True, tc)
        o = of_scr[pl.ds(pl.multiple_of((nc - 1 - i) * tc, tc), tc), :] + o_b
        ms = _split_dot(o * o, jv) * (1.0 / HG_DV)
        g = hg_ref[0, :, 4 * HG_W:5 * HG_W]
        o_ref[0] = (o * lax.rsqrt(ms + RMS_EPS) * gn_ref[...] * (g * _sigmoid(g))).astype(o_ref.dtype)

    @pl.when(i == nc - 1)
    def _():
        sfin_ref[0, p] = st_scr[...]


def _hgrn(hg, lb2, gn, s0t, tc=256):
    bsz, t_len, _ = hg.shape
    nc = t_len // tc
    lower = np.arange(tc)[:, None] >= np.arange(tc)[None, :]
    tri = jnp.asarray(np.stack([lower, lower.T]).astype(np.float32))
    chunk = lambda b, p, i: (b, i + p * (nc - 1 - 2 * i), 0)
    o, sfin = pl.pallas_call(
        functools.partial(_hgrn_kernel, tc=tc, nc=nc),
        grid=(bsz, 2, nc),
        in_specs=[
            pl.BlockSpec((1, tc, HG_COLS), chunk),
            pl.BlockSpec((2, HG_W), lambda b, p, i: (0, 0)),
            pl.BlockSpec((1, HG_W), lambda b, p, i: (0, 0)),
            pl.BlockSpec((1, 2, HG_W, HG_W), lambda b, p, i: (b, 0, 0, 0)),
            pl.BlockSpec((2, tc, tc), lambda b, p, i: (0, 0, 0)),
            pl.BlockSpec((HG_W, HG_W), lambda b, p, i: (0, 0)),
        ],
        out_specs=[
            pl.BlockSpec((1, tc, HG_W), lambda b, p, i: (b, nc - 1 - i * p, 0)),
            pl.BlockSpec((1, 2, HG_W, HG_W), lambda b, p, i: (b, 0, 0, 0)),
        ],
        out_shape=[jax.ShapeDtypeStruct((bsz, t_len, HG_W), BF16),
                   jax.ShapeDtypeStruct((bsz, 2, HG_W, HG_W), F32)],
        scratch_shapes=[pltpu.VMEM((HG_W, HG_W), F32), pltpu.VMEM((t_len, HG_W), F32)],
        compiler_params=_cparams(("parallel", "arbitrary", "arbitrary")),
        name="hgrn",
    )(hg, lb2, jnp.tile(gn, HG_HEADS)[None], s0t, tri, _block_ones(HG_W, HG_DV))
    return o, sfin


def _state_to_blockdiag_t(s0):
    bsz = s0.shape[0]
    s0t = jnp.swapaxes(s0, -1, -2)
    eye = jnp.eye(HG_HEADS, dtype=s0.dtype)
    return jnp.einsum("bdhvk,hg->bdhvgk", s0t, eye).reshape(bsz, 2, HG_W, HG_W)


def _blockdiag_t_to_state(st):
    bsz = st.shape[0]
    s6 = st.reshape(bsz, 2, HG_HEADS, HG_DV, HG_HEADS, HG_DK)
    diag = jnp.stack([s6[:, :, h, :, h, :] for h in range(HG_HEADS)], axis=2)
    return jnp.swapaxes(diag, -1, -2)


def _post_kernel(x_ref, oa_ref, on_ref, od_ref, oh_ref, sh_ref, sc_ref, g1_ref, wg_ref, wb_ref, wo_ref,
                 lg_ref, lb_ref, o_ref, *, alpha):
    x = x_ref[0]
    h = (_ln_plain(x) * (1.0 + sc_ref[0]) + sh_ref[0]).astype(BF16)
    mixed = None
    for n, br_ref in enumerate((oa_ref, on_ref, od_ref, oh_ref)):
        gate = _sigmoid(_dot(h, wg_ref[:, n * D_MODEL:(n + 1) * D_MODEL]))
        term = gate * _dot(br_ref[0], wb_ref[n])
        mixed = term if mixed is None else mixed + term
    y = _dot(mixed.astype(BF16), wo_ref[...])
    o_ref[0] = _ln_plain(alpha * x + g1_ref[0] * y) * lg_ref[...] + lb_ref[...]


def _post_mixer(x, branches, shift, scale, gate, w_gates, w_branch, w_out, ln_g, ln_b, alpha, tm=256):
    bsz, t_len, d = x.shape
    nb = shift.shape[0]
    mod_map = (lambda b, i: (b, 0, 0)) if nb > 1 else (lambda b, i: (0, 0, 0))
    tok = lambda n: pl.BlockSpec((1, tm, n), lambda b, i: (b, i, 0))
    const = lambda shape: pl.BlockSpec(shape, lambda b, i: (0,) * len(shape))
    mod = pl.BlockSpec((1, 1, d), mod_map)
    return pl.pallas_call(
        functools.partial(_post_kernel, alpha=alpha),
        grid=(bsz, t_len // tm),
        in_specs=[tok(d)] + [tok(BRANCH_W)] * 4 + [mod, mod, mod,
                  const((d, N_BRANCH * d)), const((N_BRANCH, BRANCH_W, d)), const((d, d)),
                  const((1, d)), const((1, d))],
        out_specs=tok(d),
        out_shape=jax.ShapeDtypeStruct((bsz, t_len, d), F32),
        compiler_params=_cparams(("parallel", "parallel")),
        name="post_mixer",
    )(x, *branches, shift, scale, gate, w_gates, w_branch, w_out, ln_g[None], ln_b[None])


ROUTER_LANES = 128
MOE_BT = 1024
MOE_ROWS = 128
MOE_STEPS = N_EXPERTS // 2
MOE_CHUNKS = 4


ROUTER_ROWS = 24


def _router_weights(logits):
    lt = jnp.transpose(logits)[0:ROUTER_ROWS, :]
    row = lax.broadcasted_iota(jnp.int32, lt.shape, 0)
    first_at = lambda mask: jnp.min(jnp.where(mask, row, ROUTER_LANES), axis=0, keepdims=True)
    is_g = row < N_GROUPS
    gl = jnp.where(is_g, lt, NEG_BIG)
    g_max = jnp.max(gl, axis=0, keepdims=True)
    g_idx = first_at(is_g & (gl == g_max))
    g_prob = 1.0 / jnp.sum(jnp.where(is_g, jnp.exp(gl - g_max), 0.0), axis=0, keepdims=True)
    lo = N_GROUPS + g_idx * EXPERTS_PER_GROUP
    in_grp = (row >= lo) & (row < lo + EXPERTS_PER_GROUP)
    el = jnp.where(in_grp, lt, NEG_BIG)
    v1 = jnp.max(el, axis=0, keepdims=True)
    i1 = first_at(in_grp & (el == v1))
    rest = in_grp & (row != i1)
    el2 = jnp.where(rest, lt, NEG_BIG)
    v2 = jnp.max(el2, axis=0, keepdims=True)
    i2 = first_at(rest & (el2 == v2))
    e2 = jnp.exp(v2 - v1)
    w1 = g_prob / (1.0 + e2)
    w2 = g_prob * e2 / (1.0 + e2)
    picked = jnp.where(row == g_idx, 1.0, jnp.where(row == i1, w1, jnp.where(row == i2, w2, 0.0)))
    pad = jnp.zeros((ROUTER_LANES - ROUTER_ROWS, lt.shape[1]), F32)
    return jnp.transpose(jnp.concatenate([picked, pad], axis=0))


def _moe_kernel(x_ref, sh_ref, sc_ref, g2_ref, wr_ref, br_ref, lt_ref, wg_ref, wu_ref, wd_ref, lg_ref, lb_ref, o_ref,
                xs_scr, pt_scr, cws_scr, ys_scr, seg_ref, *, alpha):
    s = pl.program_id(2)
    bt = xs_scr.shape[0]
    rows = MOE_ROWS

    @pl.when(s == 0)
    def _():
        x_parts, logit_parts = [], []
        for r in range(MOE_CHUNKS):
            sl = slice(r * bt // MOE_CHUNKS, (r + 1) * bt // MOE_CHUNKS)
            xm = _ln_plain(x_ref[0, sl, :]) * (1.0 + sc_ref[0]) + sh_ref[0]
            hi = xm.astype(BF16)
            lo = (xm - hi.astype(F32)).astype(BF16)
            x_parts.append(hi)
            logit_parts.append(_dot(hi, wr_ref[0]) + _dot(lo, wr_ref[0]) + _dot(hi, wr_ref[1]) + br_ref[...])
        x_hi = jnp.concatenate(x_parts, axis=0)
        logits = jnp.concatenate(logit_parts, axis=0)
        routed = _router_weights(logits)
        lane = lax.broadcasted_iota(jnp.int32, (bt, ROUTER_LANES), 1)
        onehot = jnp.where(lane < N_GROUPS, routed, 0.0)
        in_g = onehot > 0.0
        cw = routed - onehot
        before = _dot(lt_ref[...], onehot.astype(BF16))
        count = jnp.sum(onehot, axis=0, keepdims=True)
        lane1 = lax.broadcasted_iota(jnp.int32, (1, ROUTER_LANES), 1)
        start = jnp.zeros((1, ROUTER_LANES), F32)
        for k in range(N_GROUPS - 1):
            start = start + jnp.where(lane1 > k, count[:, k:k + 1], 0.0)
        dest = jnp.sum(jnp.where(in_g, start + before, 0.0), axis=-1, keepdims=True)
        dest_row = jnp.transpose(jnp.broadcast_to(dest, (bt, ROUTER_LANES)))[0:1, :]
        pt_scr[...] = jnp.where(dest.astype(jnp.int32) == lax.broadcasted_iota(jnp.int32, (bt, bt), 1),
                                1.0, 0.0).astype(BF16)
        perm = jnp.where(dest_row.astype(jnp.int32) == lax.broadcasted_iota(jnp.int32, (bt, bt), 0),
                         1.0, 0.0).astype(BF16)
        xs_scr[...] = _dot(perm, x_hi).astype(BF16)
        c_hi = cw.astype(BF16)
        c_lo = (cw - c_hi.astype(F32)).astype(BF16)
        cws_scr[...] = _dot(perm, c_hi) + _dot(perm, c_lo)
        ys_scr[...] = jnp.zeros_like(ys_scr)
        for g in range(1, N_GROUPS):
            seg_ref[g] = jnp.sum(jnp.where(lane1 == g, start, 0.0)).astype(jnp.int32)

    g = s // (MOE_STEPS // N_GROUPS)
    seg_lo = jnp.where(g == 0, 0, seg_ref[jnp.maximum(g, 1)])
    seg_hi = jnp.where(g == N_GROUPS - 1, bt, seg_ref[jnp.minimum(g + 1, N_GROUPS - 1)])
    t_lo = seg_lo // rows
    t_hi = jnp.where(seg_hi > seg_lo, (seg_hi + rows - 1) // rows, t_lo)
    lane_r = lax.broadcasted_iota(jnp.int32, (rows, ROUTER_LANES), 1)

    def gate_up(t):
        xt = xs_scr[pl.ds(pl.multiple_of(t * rows, rows), rows), :]
        return [(_dot(xt, wg_ref[e]), _dot(xt, wu_ref[e])) for e in range(2)]

    def down(t, gu):
        r0 = pl.multiple_of(t * rows, rows)
        cwt = cws_scr[pl.ds(r0, rows), :]
        hids = []
        for e, (gate, up) in enumerate(gu):
            w_e = jnp.sum(jnp.where(lane_r == N_GROUPS + 2 * s + e, cwt, 0.0), axis=-1, keepdims=True)
            hids.append((gate * _sigmoid(gate) * up * w_e).astype(BF16))
        ys_scr[pl.ds(r0, rows), :] += _dot(hids[0], wd_ref[0]) + _dot(hids[1], wd_ref[1])

    def tile_pair(i, carry):
        t = t_lo + 2 * i
        gu_a, gu_b = gate_up(t), gate_up(t + 1)
        down(t, gu_a)
        down(t + 1, gu_b)
        return carry

    n_tiles = t_hi - t_lo
    lax.fori_loop(0, n_tiles // 2, tile_pair, 0)

    @pl.when(n_tiles % 2 == 1)
    def _():
        down(t_hi - 1, gate_up(t_hi - 1))

    @pl.when(s == MOE_STEPS - 1)
    def _():
        ys = ys_scr[...]
        hi = ys.astype(BF16)
        lo = (ys - hi.astype(F32)).astype(BF16)
        rows_of = lambda r: slice(r * bt // MOE_CHUNKS, (r + 1) * bt // MOE_CHUNKS)
        back = lambda r: _dot(pt_scr[rows_of(r), :], hi) + _dot(pt_scr[rows_of(r), :], lo)
        y_next = back(0)
        for r in range(MOE_CHUNKS):
            y, y_next = y_next, (back(r + 1) if r + 1 < MOE_CHUNKS else None)
            sl = rows_of(r)
            o_ref[0, sl, :] = _ln_plain(alpha * x_ref[0, sl, :] + g2_ref[0] * y) * lg_ref[...] + lb_ref[...]


def _moe(x, shift, scale, gate, w_route, b_route, w_gate, w_up, w_down, ln_g, ln_b, alpha):
    bsz, t_len, d = x.shape
    bt = min(MOE_BT, t_len)
    assert t_len % bt == 0 and bt % MOE_ROWS == 0 and bt % (8 * MOE_CHUNKS) == 0
    nb = shift.shape[0]
    mod_map = (lambda b, i, e: (b, 0, 0)) if nb > 1 else (lambda b, i, e: (0, 0, 0))
    tok = pl.BlockSpec((1, bt, d), lambda b, i, e: (b, i, 0))
    const = lambda shape: pl.BlockSpec(shape, lambda b, i, e: (0,) * len(shape))
    mod = pl.BlockSpec((1, 1, d), mod_map)
    pair = lambda shape: pl.BlockSpec((2,) + shape, lambda b, i, e: (e, 0, 0))
    lower = jnp.asarray(np.tril(np.ones((bt, bt), np.float32), -1), dtype=BF16)
    return pl.pallas_call(
        functools.partial(_moe_kernel, alpha=alpha),
        grid=(bsz, t_len // bt, MOE_STEPS),
        in_specs=[tok, mod, mod, mod, const((2, d, ROUTER_LANES)), const((1, ROUTER_LANES)), const((bt, bt)),
                  pair((d, D_EXPERT)), pair((d, D_EXPERT)), pair((D_EXPERT, d)), const((1, d)), const((1, d))],
        out_specs=tok,
        out_shape=jax.ShapeDtypeStruct((bsz, t_len, d), F32),
        scratch_shapes=[pltpu.VMEM((bt, d), BF16), pltpu.VMEM((bt, bt), BF16), pltpu.VMEM((bt, ROUTER_LANES), F32),
                        pltpu.VMEM((bt, d), F32), pltpu.SMEM((N_GROUPS,), jnp.int32)],
        compiler_params=_cparams(("parallel", "parallel", "arbitrary")),
        name="moe",
    )(x, shift, scale, gate, w_route, b_route, lower, w_gate, w_up, w_down, ln_g[None], ln_b[None])


def _layer(x, mods, lp, ctx, alpha):
    bsz, t_len, _ = x.shape
    latent = ctx is not None
    shift1, scale1, gate1, shift2, scale2, gate2 = mods
    past = ctx["gk_t"].shape[-1] if latent else 0
    outs = _inproj(x, shift1, scale1, lp["w_heads"], lp["gqa_qn"], lp["gqa_kn"], latent, past)
    qa, kat, va, qn, kn, vn, qd, kdt, vd, hg = outs[:10]
    if latent:
        kat = lax.dynamic_update_slice(kat, ctx["gk_t"], (0, 0, 0))
        va = lax.dynamic_update_slice(va, ctx["gv_ext"], (0, 0, 0))
        kdt = lax.dynamic_update_slice(kdt, ctx["dk_t"], (0, 0, 0))
        vd = lax.dynamic_update_slice(vd, ctx["dv"], (0, 0, 0))
        s0t = _state_to_blockdiag_t(ctx["state"].astype(F32))
        tq = 256
    else:
        s0t = jnp.zeros((bsz, 2, HG_W, HG_W), F32)
        tq = t_len
    o_a = _attention(qa, kat, va, GQA_Q_HEADS, GQA_KV_HEADS, tq)
    if latent:
        o_n = _na_attention(qn, kn, vn, ctx["nk_ext"], ctx["nv_ext"], lp["na_bias"])
    else:
        o_n = _attention(qn, kn, vn, NA_HEADS, NA_HEADS, tq)
    o_d = _diff_attention(qd, kdt, vd, lp["diff_par"], tq)
    o_h, sfin = _hgrn(hg, lp["hg_lb"], lp["hg_norm"], s0t)
    x1 = _post_mixer(x, (o_a, o_n, o_d, o_h), shift1, scale1, gate1, lp["w_gates"], lp["w_branch"], lp["w_out"],
                     lp["ln_mix_g"], lp["ln_mix_b"], alpha)
    x1r = x1 if latent else x1.reshape(1, bsz * t_len, D_MODEL)
    y = _moe(x1r, shift2, scale2, gate2, lp["w_route"], lp["b_route"], lp["w_gate"], lp["w_up"], lp["w_down"],
             lp["ln_ffn_g"], lp["ln_ffn_b"], alpha).reshape(bsz, t_len, D_MODEL)
    caches = None
    if not latent:
        ka, va32, kn32, vn32, kd32, vd32 = outs[10:]
        caches = (ka.reshape(bsz, t_len, GQA_KV_HEADS, HEAD_DIM), va32.reshape(bsz, t_len, GQA_KV_HEADS, HEAD_DIM),
                  kn32.reshape(bsz, t_len, NA_HEADS, HEAD_DIM), vn32.reshape(bsz, t_len, NA_HEADS, HEAD_DIM),
                  kd32.reshape(bsz, t_len, DIFF_HEADS, 2, DIFF_QK_DIM), vd32.reshape(bsz, t_len, DIFF_HEADS, DIFF_V_DIM),
                  _blockdiag_t_to_state(sfin))
    return y, caches


def _padded_heads(a, fill):
    a = a.astype(BF16)
    return jnp.concatenate([a, jnp.full(a.shape, fill, BF16)], axis=-1).reshape(a.shape[:3] + (-1,))


def _keys_t(a):
    a = a.astype(BF16).reshape(a.shape[:3] + (-1,))
    return jnp.swapaxes(a, 2, 3)


def kernel(x_prompt, x_sample, cache_gqa_k, cache_gqa_v, cache_na_k, cache_na_v, cache_diff_k, cache_diff_v, state_hgrn, c, c_ctx, w_ada, b_ada, w_in, gqa_q_norm, gqa_k_norm, na_rpb, diff_lambda, diff_subln, hgrn_lb, hgrn_norm, w_branch, w_out, ln_mix_g, ln_mix_b, ln_ffn_g, ln_ffn_b, w_group, b_group, w_router, b_router, w_gate, w_up, w_down):
    depth = w_in.shape[0]
    d = D_MODEL
    alpha = (2 * depth) ** 0.25
    dec_b = c.shape[0]

    lb = jax.nn.softmax(hgrn_lb.astype(F32), axis=1)
    lb = jnp.cumsum(lb, axis=1) - lb[:, :1]

    cond_rows = -(-(dec_b + 1) // 8) * 8
    cond = jnp.zeros((cond_rows, d), F32).at[:dec_b].set(c).at[dec_b].set(c_ctx)
    mods_all = _ada_mods(cond, w_ada, b_ada)

    na_bias = _na_bias_table(na_rpb)
    layers = []
    for l in range(depth):
        lam_init = 0.8 - 0.6 * math.exp(-0.3 * l)
        lp_lam = diff_lambda[l].astype(F32)
        lam = jnp.exp(jnp.sum(lp_lam[0] * lp_lam[1])) - jnp.exp(jnp.sum(lp_lam[2] * lp_lam[3])) + lam_init
        diff_par = jnp.zeros((8, DIFF_V_DIM), F32).at[0].set(lam).at[1].set(diff_subln[l] * (1.0 - lam_init))
        w_route = jnp.zeros((d, ROUTER_LANES), F32)
        w_route = w_route.at[:, :N_GROUPS].set(w_group[l]).at[:, N_GROUPS:N_GROUPS + N_EXPERTS].set(
            w_router[l].reshape(d, N_EXPERTS))
        w_route_hi = w_route.astype(BF16)
        b_route = jnp.zeros((1, ROUTER_LANES), F32)
        b_route = b_route.at[0, :N_GROUPS].set(b_group[l]).at[0, N_GROUPS:N_GROUPS + N_EXPERTS].set(
            b_router[l].reshape(N_EXPERTS))
        layers.append({
            "w_heads": w_in[l, :, :D_HEADS_IN].astype(BF16),
            "w_gates": w_in[l, :, D_HEADS_IN:].astype(BF16),
            "gqa_qn": gqa_q_norm[l], "gqa_kn": gqa_k_norm[l],
            "na_bias": na_bias[l],
            "diff_par": diff_par,
            "hg_lb": lb[:, l], "hg_norm": hgrn_norm[l],
            "w_branch": w_branch[l].astype(BF16), "w_out": w_out[l].astype(BF16),
            "ln_mix_g": ln_mix_g[l], "ln_mix_b": ln_mix_b[l], "ln_ffn_g": ln_ffn_g[l], "ln_ffn_b": ln_ffn_b[l],
            "w_route": jnp.stack([w_route_hi, (w_route - w_route_hi.astype(F32)).astype(BF16)]), "b_route": b_route,
            "w_gate": w_gate[l].reshape(N_EXPERTS, d, D_EXPERT).astype(BF16),
            "w_up": w_up[l].reshape(N_EXPERTS, d, D_EXPERT).astype(BF16),
            "w_down": w_down[l].reshape(N_EXPERTS, D_EXPERT, d).astype(BF16),
        })

    def split_mods(rows):
        return tuple(rows[:, None, k * d:(k + 1) * d] for k in range(6))

    xp = x_prompt
    ctx_out = []
    for l in range(depth):
        xp, caches = _layer(xp, split_mods(mods_all[l, dec_b:dec_b + 1]), layers[l], None, alpha)
        ctx_out.append(caches)
    stack = lambda i: jnp.stack([t[i] for t in ctx_out], axis=1)

    cached = {"gk_t": _keys_t(cache_gqa_k), "gv_ext": _padded_heads(cache_gqa_v, 1.0),
              "nk_ext": _padded_heads(cache_na_k, 0.0), "nv_ext": _padded_heads(cache_na_v, 1.0),
              "dk_t": _keys_t(cache_diff_k), "dv": _padded_heads(cache_diff_v, 1.0), "state": state_hgrn}
    xs = x_sample
    for l in range(depth):
        ctx = {name: t[:, l] for name, t in cached.items()}
        xs, _ = _layer(xs, split_mods(mods_all[l, :dec_b]), layers[l], ctx, alpha)

    return (xp, xs, stack(0), stack(1), stack(2), stack(3), stack(4), stack(5), stack(6))
```

```python
import functools
import math

import jax
import jax.numpy as jnp
import numpy as np
from jax import lax
from jax.experimental import pallas as pl
from jax.experimental.pallas import tpu as pltpu

F32 = jnp.float32
BF16 = jnp.bfloat16

D_MODEL = 1024
GRID_W = 64
HEAD_DIM = 64
GQA_Q_HEADS = 4
GQA_KV_HEADS = 2
NA_HEADS = 4
NA_KH = 8
NA_KW = 16
DIFF_HEADS = 4
DIFF_QK_DIM = 32
DIFF_V_DIM = 64
HG_HEADS = 4
HG_DK = 64
HG_DV = 64
BRANCH_W = 256
N_BRANCH = 4
N_GROUPS = 4
EXPERTS_PER_GROUP = 4
N_EXPERTS = N_GROUPS * EXPERTS_PER_GROUP
D_EXPERT = 512
ROPE_THETA = 10000.0
LN_EPS = 1e-5
RMS_EPS = 1e-6
F_FLOOR = 1e-30
NEG_BIG = -1e30
LOG2E = math.log2(math.e)
MOD_SHIFT1, MOD_SCALE1, MOD_GATE1, MOD_SHIFT2, MOD_SCALE2, MOD_GATE2 = range(6)

C_GQ, C_GK, C_GV = 0, 256, 384
C_NQ, C_NK, C_NV = 512, 768, 1024
C_DQ, C_DK, C_DV = 1280, 1536, 1792
C_HG = 2048
D_HEADS_IN = 3328
HG_COLS = 1280

VMEM_LIMIT = 56 * 1024 * 1024


def _cparams(sem):
    return pltpu.CompilerParams(dimension_semantics=sem, vmem_limit_bytes=VMEM_LIMIT)


def _dot(a, b):
    return jnp.dot(a, b, preferred_element_type=F32)


def _dot_nt(a, b):
    return lax.dot_general(a, b, (((1,), (1,)), ((), ())), preferred_element_type=F32)


def _split_dot(x, j):
    hi = x.astype(BF16)
    lo = (x - hi.astype(F32)).astype(BF16)
    return _dot(hi, j) + _dot(lo, j)


def _sigmoid(z):
    return 1.0 / (1.0 + jnp.exp(-z))


def _ln_plain(x):
    mu = jnp.mean(x, axis=-1, keepdims=True)
    xc = x - mu
    var = jnp.mean(xc * xc, axis=-1, keepdims=True)
    return xc * lax.rsqrt(var + LN_EPS)


def _mod_spec(mods, k, rank):
    per_batch = mods.shape[0] > 1
    d = mods.shape[-1] // 6
    if rank == 2:
        return pl.BlockSpec((1, 1, d), lambda b, i: (b if per_batch else 0, 0, k))
    return pl.BlockSpec((1, 1, d), lambda b, i, e: (b if per_batch else 0, 0, k))


def _block_ones(n, blk):
    idx = np.arange(n) // blk
    return jnp.asarray((idx[:, None] == idx[None, :]).astype(np.float32), dtype=BF16)


def _ada_kernel(c_ref, w_ref, b_ref, o_ref):
    c = c_ref[...]
    s = (c * _sigmoid(c)).astype(BF16)
    o_ref[0] = _dot(s, w_ref[0].astype(BF16)) + b_ref[0]


def _ada_mods(cond, w_ada, b_ada):
    depth, d, n = w_ada.shape
    rows = cond.shape[0]
    tn = 1536
    return pl.pallas_call(
        _ada_kernel,
        grid=(depth, n // tn),
        in_specs=[
            pl.BlockSpec((rows, d), lambda l, j: (0, 0)),
            pl.BlockSpec((1, d, tn), lambda l, j: (l, 0, j)),
            pl.BlockSpec((1, 1, tn), lambda l, j: (l, 0, j)),
        ],
        out_specs=pl.BlockSpec((1, rows, tn), lambda l, j: (l, 0, j)),
        out_shape=jax.ShapeDtypeStruct((depth, rows, n), F32),
        compiler_params=_cparams(("parallel", "parallel")),
        name="ada_mods",
    )(cond, w_ada, b_ada.reshape(depth, 1, n))


def _rope_tables(t_len, d, reps):
    quarter = d // 4
    t = np.arange(t_len)
    inv = ROPE_THETA ** (-np.arange(quarter, dtype=np.float64) / quarter)
    ang_r = (t // GRID_W).astype(np.float64)[:, None] * inv
    ang_c = (t % GRID_W).astype(np.float64)[:, None] * inv
    ang = np.concatenate([ang_r, ang_r, ang_c, ang_c], axis=-1)
    cos, sin = np.cos(ang), np.sin(ang)
    first = (np.arange(d) % (2 * quarter)) < quarter
    sin_a = np.where(first, -sin, 0.0)
    sin_b = np.where(first, 0.0, sin)
    tile = lambda a: jnp.asarray(np.tile(a, (1, reps)), dtype=F32)
    return tile(cos), tile(sin_a), tile(sin_b)


def _rope(x, cos, sin_a, sin_b, quarter):
    w = x.shape[-1]
    return x * cos + pltpu.roll(x, w - quarter, 1) * sin_a + pltpu.roll(x, quarter, 1) * sin_b


def _seg_rms(x, j):
    ss = _split_dot(x * x, j)
    return x * lax.rsqrt(ss * (1.0 / HEAD_DIM) + RMS_EPS)


def _pad_heads(x, heads, fill):
    pad = jnp.full((x.shape[0], HEAD_DIM), fill, x.dtype)
    pieces = []
    for h in range(heads):
        pieces += [x[:, h * HEAD_DIM:(h + 1) * HEAD_DIM], pad]
    return jnp.concatenate(pieces, axis=-1)


def _inproj_kernel(*refs, latent):
    n_in = 13 if latent else 7
    ins, outs = refs[:n_in], refs[n_in:]
    x_ref, sh_ref, sc_ref, w_ref, qn_ref, kn_ref, j_ref = ins[:7]
    qa_ref, kat_ref, va_ref, qnb_ref, knb_ref, vnb_ref, qd_ref, kdt_ref, vd_ref, hg_ref = outs[:10]
    h = (_ln_plain(x_ref[0]) * (1.0 + sc_ref[0]) + sh_ref[0]).astype(BF16)
    proj = lambda c0, n: _dot(h, w_ref[:, c0:c0 + n])

    gq = _seg_rms(proj(C_GQ, 256), j_ref[...]) * qn_ref[...]
    gk = _seg_rms(proj(C_GK, 128), j_ref[0:128, 0:128]) * kn_ref[...]
    gv = proj(C_GV, 128)
    nq = proj(C_NQ, 256) * (LOG2E * HEAD_DIM ** -0.5)
    nk = proj(C_NK, 256)
    nv = proj(C_NV, 256)
    dq = proj(C_DQ, 256)
    dk = proj(C_DK, 256)
    dv = proj(C_DV, 256)
    if latent:
        c64, a64, b64, c32, a32, b32 = (r[...] for r in ins[7:13])
        gq = _rope(gq, c64, a64, b64, HEAD_DIM // 4)
        gk = _rope(gk, c64[:, 0:128], a64[:, 0:128], b64[:, 0:128], HEAD_DIM // 4)
        dq = _rope(dq, c32, a32, b32, DIFF_QK_DIM // 4)
        dk = _rope(dk, c32, a32, b32, DIFF_QK_DIM // 4)
    else:
        for ref, val in zip(outs[10:], (gk, gv, nk, nv, dk, dv)):
            ref[0] = val
    qa_ref[0] = (gq * (LOG2E * HEAD_DIM ** -0.5)).astype(BF16)
    kat_ref[0] = jnp.transpose(gk).astype(BF16)
    va_ref[0] = _pad_heads(gv, GQA_KV_HEADS, 1.0).astype(BF16)
    if latent:
        qnb_ref[0] = _pad_heads(nq, NA_HEADS, 0.0).astype(BF16)
        knb_ref[0] = _pad_heads(nk, NA_HEADS, 0.0).astype(BF16)
    else:
        qnb_ref[0] = nq.astype(BF16)
        knb_ref[0] = jnp.transpose(nk).astype(BF16)
    vnb_ref[0] = _pad_heads(nv, NA_HEADS, 1.0).astype(BF16)
    qd_ref[0] = (dq * (LOG2E * DIFF_QK_DIM ** -0.5)).astype(BF16)
    kdt_ref[0] = jnp.transpose(dk).astype(BF16)
    vd_ref[0] = _pad_heads(dv, DIFF_HEADS, 1.0).astype(BF16)
    for s in range(HG_COLS // 256):
        hg_ref[0, :, s * 256:(s + 1) * 256] = proj(C_HG + s * 256, 256)


def _inproj(x, mods, w_heads, qn, kn, latent, past=0, tm=256):
    bsz, t_len, d = x.shape
    assert past % tm == 0
    skip = past // tm
    tok = lambda n: pl.BlockSpec((1, tm, n), lambda b, i: (b, i, 0))
    tok_t = lambda n: pl.BlockSpec((1, n, tm), lambda b, i: (b, 0, i))
    const = lambda shape: pl.BlockSpec(shape, lambda b, i: (0,) * len(shape))
    in_specs = [tok(d), _mod_spec(mods, MOD_SHIFT1, 2), _mod_spec(mods, MOD_SCALE1, 2),
                const((d, D_HEADS_IN)), const((1, 256)), const((1, 128)), const((256, 256))]
    args = [x, mods, mods, w_heads, jnp.tile(qn, GQA_Q_HEADS)[None], jnp.tile(kn, GQA_KV_HEADS)[None],
            _block_ones(256, HEAD_DIM)]
    if latent:
        in_specs += [pl.BlockSpec((tm, 256), lambda b, i: (i, 0))] * 6
        args += list(_rope_tables(t_len, HEAD_DIM, 4)) + list(_rope_tables(t_len, DIFF_QK_DIM, 8))
    rows = lambda n, dt=BF16: (jax.ShapeDtypeStruct((bsz, t_len, n), dt), tok(n))
    cols = lambda n: (jax.ShapeDtypeStruct((bsz, n, t_len), BF16), tok_t(n))
    rows_kv = lambda n: (jax.ShapeDtypeStruct((bsz, past + t_len, n), BF16),
                         pl.BlockSpec((1, tm, n), lambda b, i: (b, i + skip, 0)))
    cols_kv = lambda n: (jax.ShapeDtypeStruct((bsz, n, past + t_len), BF16),
                         pl.BlockSpec((1, n, tm), lambda b, i: (b, 0, i + skip)))
    outs = [rows(256), cols_kv(128), rows_kv(256),
            rows(512) if latent else rows(256), rows(512) if latent else cols(256), rows(512),
            rows(256), cols_kv(256), rows_kv(512), rows(HG_COLS, F32)]
    if not latent:
        outs += [rows(n, F32) for n in (128, 128, 256, 256, 256, 256)]
    return pl.pallas_call(
        functools.partial(_inproj_kernel, latent=latent),
        grid=(bsz, t_len // tm),
        in_specs=in_specs,
        out_specs=[o[1] for o in outs],
        out_shape=[o[0] for o in outs],
        compiler_params=_cparams(("parallel", "parallel")),
        name="in_proj_rope" if latent else "in_proj",
    )(*args)


def _softmax_p(s):
    return jnp.exp2(s - jnp.max(s, axis=-1, keepdims=True)).astype(BF16)


def _pv_norm(p, v_ext):
    dv = v_ext.shape[-1] // 2
    o = _dot(p, v_ext)
    return o[:, :dv] * (1.0 / o[:, dv:])


def _attn_kernel(q_ref, kt_ref, v_ref, o_ref, *, hq, group, d, dv):
    q = q_ref[0]
    tq = q.shape[0]
    n_groups = hq // group

    def scores(g):
        qs = jnp.concatenate([q[:, h * d:(h + 1) * d] for h in range(g * group, (g + 1) * group)], axis=0)
        return _dot(qs, kt_ref[0, g * d:(g + 1) * d, :])

    outs = []
    s_next = scores(0)
    for g in range(n_groups):
        s_cur, s_next = s_next, (scores(g + 1) if g + 1 < n_groups else None)
        o = _pv_norm(_softmax_p(s_cur), v_ref[0, :, g * 2 * dv:(g + 1) * 2 * dv])
        outs += [o[j * tq:(j + 1) * tq] for j in range(group)]
    o_ref[0] = jnp.concatenate(outs, axis=-1).astype(o_ref.dtype)


def _attention(q, kt, v_ext, hq, hkv, tq):
    bsz, t_len, qw = q.shape
    l_len = kt.shape[-1]
    d = kt.shape[1] // hkv
    dv = v_ext.shape[-1] // (2 * hkv)
    return pl.pallas_call(
        functools.partial(_attn_kernel, hq=hq, group=hq // hkv, d=d, dv=dv),
        grid=(bsz, t_len // tq),
        in_specs=[
            pl.BlockSpec((1, tq, qw), lambda b, i: (b, i, 0)),
            pl.BlockSpec((1, hkv * d, l_len), lambda b, i: (b, 0, 0)),
            pl.BlockSpec((1, l_len, hkv * 2 * dv), lambda b, i: (b, 0, 0)),
        ],
        out_specs=pl.BlockSpec((1, tq, hq * dv), lambda b, i: (b, i, 0)),
        out_shape=jax.ShapeDtypeStruct((bsz, t_len, hq * dv), BF16),
        compiler_params=_cparams(("parallel", "parallel")),
        name="attention",
    )(q, kt, v_ext)


def _diff_attn_kernel(q_ref, kt_ref, v_ref, par_ref, o_ref):
    q = q_ref[0]
    tq = q.shape[0]
    lam = par_ref[0:1, :]
    gain = par_ref[1:2, :]
    dq = DIFF_QK_DIM
    scores = lambda i: _dot(q[:, i * dq:(i + 1) * dq], kt_ref[0, i * dq:(i + 1) * dq, :])
    outs = []
    s_next = (scores(0), scores(1))
    for h in range(DIFF_HEADS):
        s_cur, s_next = s_next, ((scores(2 * h + 2), scores(2 * h + 3)) if h + 1 < DIFF_HEADS else None)
        o12 = _pv_norm(jnp.concatenate([_softmax_p(s_cur[0]), _softmax_p(s_cur[1])], axis=0),
                       v_ref[0, :, h * 2 * DIFF_V_DIM:(h + 1) * 2 * DIFF_V_DIM])
        o = o12[:tq] - lam * o12[tq:]
        ms = jnp.mean(o * o, axis=-1, keepdims=True)
        outs.append(o * lax.rsqrt(ms + RMS_EPS) * gain)
    o_ref[0] = jnp.concatenate(outs, axis=-1).astype(o_ref.dtype)


def _diff_attention(q, kt, v_ext, par, tq):
    bsz, t_len, qw = q.shape
    l_len = kt.shape[-1]
    return pl.pallas_call(
        _diff_attn_kernel,
        grid=(bsz, t_len // tq),
        in_specs=[
            pl.BlockSpec((1, tq, qw), lambda b, i: (b, i, 0)),
            pl.BlockSpec((1, 2 * DIFF_HEADS * DIFF_QK_DIM, l_len), lambda b, i: (b, 0, 0)),
            pl.BlockSpec((1, l_len, DIFF_HEADS * 2 * DIFF_V_DIM), lambda b, i: (b, 0, 0)),
            pl.BlockSpec((8, DIFF_V_DIM), lambda b, i: (0, 0)),
        ],
        out_specs=pl.BlockSpec((1, tq, DIFF_HEADS * DIFF_V_DIM), lambda b, i: (b, i, 0)),
        out_shape=jax.ShapeDtypeStruct((bsz, t_len, DIFF_HEADS * DIFF_V_DIM), BF16),
        compiler_params=_cparams(("parallel", "parallel")),
        name="diff_attention",
    )(q, kt, v_ext, par)


def _na_bias_table(rpb):
    shift = np.arange(NA_KH)[:, None]
    win_row = np.arange(NA_KH)[None, :]
    row_off = win_row - shift + (NA_KH - 1)
    col = np.arange(GRID_W)
    col_start = np.clip(col - NA_KW // 2, 0, GRID_W - NA_KW)
    key_col = np.arange(GRID_W)[None, :]
    valid = (key_col >= col_start[:, None]) & (key_col < col_start[:, None] + NA_KW)
    col_off = key_col - col[:, None] + (NA_KW - 1)
    row_sel = (row_off[:, :, None] == np.arange(2 * NA_KH - 1)).astype(np.float32)
    col_sel = ((col_off[:, :, None] == np.arange(2 * NA_KW - 1)) & valid[:, :, None]).astype(np.float32)
    tbl = jnp.einsum("sia,lhab,cxb->lshcix", row_sel, rpb.astype(F32) * LOG2E, col_sel,
                     precision=lax.Precision.HIGHEST)
    tbl = jnp.where(valid[None, None, None, :, None, :], tbl, NEG_BIG)
    return tbl.reshape(rpb.shape[0], NA_KH, rpb.shape[1], GRID_W, NA_KH * GRID_W)


def _na_kernel(q_ref, k_ref, v_ref, kc_ref, vc_ref, bias_ref, o_ref, *, rows_per_step, rows):
    i = pl.program_id(1)
    n_win = NA_KH * GRID_W
    units = [(rr, h) for rr in range(rows_per_step) for h in range(NA_HEADS)]

    def window(rr):
        r = i * rows_per_step + rr
        r_start = jnp.clip(r - NA_KH // 2, 0, rows - NA_KH)
        return r - r_start, pl.multiple_of(r_start * GRID_W, GRID_W)

    def scores(unit):
        rr, h = unit
        shift, start = window(rr)
        lanes = slice(h * 2 * HEAD_DIM, (h + 1) * 2 * HEAD_DIM)
        qh = q_ref[0, rr * GRID_W:(rr + 1) * GRID_W, lanes]
        s_loc = _dot_nt(qh, k_ref[0, pl.ds(start, n_win), lanes]) + bias_ref[shift, h]
        return s_loc, _dot_nt(qh, kc_ref[0, :, lanes])

    def output(unit, s_loc, s_ctx):
        rr, h = unit
        _, start = window(rr)
        m = jnp.maximum(jnp.max(s_loc, axis=-1, keepdims=True), jnp.max(s_ctx, axis=-1, keepdims=True))
        p_loc = jnp.exp2(s_loc - m).astype(BF16)
        p_ctx = jnp.exp2(s_ctx - m).astype(BF16)
        lanes = slice(h * 2 * HEAD_DIM, (h + 1) * 2 * HEAD_DIM)
        o = _dot(p_loc, v_ref[0, pl.ds(start, n_win), lanes]) + _dot(p_ctx, vc_ref[0, :, lanes])
        return o[:, :HEAD_DIM] * (1.0 / o[:, HEAD_DIM:])

    depth = 4
    staged = [scores(u) for u in units[:depth]]
    outs = []
    for n, unit in enumerate(units):
        cur = staged.pop(0)
        if n + depth < len(units):
            staged.append(scores(units[n + depth]))
        outs.append(output(unit, *cur))
        if unit[1] == NA_HEADS - 1:
            rr = unit[0]
            o_ref[0, rr * GRID_W:(rr + 1) * GRID_W, :] = jnp.concatenate(outs, axis=-1).astype(o_ref.dtype)
            outs = []


def _na_attention(q, k, v_ext, kc, vc_ext, bias, rows_per_step=8):
    bsz, n, qw = q.shape
    rows = n // GRID_W
    assert rows >= NA_KH and rows % rows_per_step == 0
    l_ctx = kc.shape[1]
    tq = rows_per_step * GRID_W
    full = lambda length: pl.BlockSpec((1, length, qw), lambda b, i: (b, 0, 0))
    return pl.pallas_call(
        functools.partial(_na_kernel, rows_per_step=rows_per_step, rows=rows),
        grid=(bsz, rows // rows_per_step),
        in_specs=[
            pl.BlockSpec((1, tq, qw), lambda b, i: (b, i, 0)),
            full(n), full(n), full(l_ctx), full(l_ctx),
            pl.BlockSpec(bias.shape, lambda b, i: (0, 0, 0, 0)),
        ],
        out_specs=pl.BlockSpec((1, tq, NA_HEADS * HEAD_DIM), lambda b, i: (b, i, 0)),
        out_shape=jax.ShapeDtypeStruct((bsz, n, NA_HEADS * HEAD_DIM), BF16),
        compiler_params=_cparams(("parallel", "parallel")),
        name="na_attention",
    )(q, k, v_ext, kc, vc_ext, bias)


HG_SUB = 16
HG_W = HG_HEADS * HG_DK


def _hgrn_direction(hg_ref, q, lb, tri, jv, bd_mask, st_ref, z_col, reverse, tc):
    c = HG_SUB
    n_sub = tc // c
    z = hg_ref[0, :, z_col:z_col + HG_W]
    v = hg_ref[0, :, 3 * HG_W:4 * HG_W]
    sig = _sigmoid(z)
    f = lb + (1.0 - lb) * sig
    log_f = jnp.log(jnp.maximum(f, F_FLOOR))
    kk = (1.0 - lb) * (1.0 - sig)
    b = jnp.dot(tri, log_f * LOG2E, preferred_element_type=F32, precision=lax.Precision.HIGHEST)
    st = st_ref[...]
    o_inter = _dot_nt((q * jnp.exp2(b)).astype(BF16), st.astype(BF16))
    b_end = b[0:1] if reverse else b[tc - 1:tc]
    upd = _dot(jnp.transpose(v).astype(BF16), (kk * jnp.exp2(b_end - b)).astype(BF16))
    st_ref[...] = jnp.exp2(b_end) * st + jnp.where(bd_mask, upd, 0.0)

    half = c // 2
    row = lax.broadcasted_iota(jnp.int32, (half, HG_W), 0)
    lane_head = lax.broadcasted_iota(jnp.int32, (c, HG_W), 1) // HG_DK
    v_bf = v.astype(BF16)
    zeros = jnp.zeros((half, HG_W), F32)
    def reach(s, top):
        if reverse:
            return ("all" if s >= half else "tri") if top else ("none" if s < half else "tri")
        return ("none" if s >= half else "tri") if top else ("all" if s < half else "tri")

    def score_stage(j):
        sl = slice(j * c, (j + 1) * c)
        bj, qj, kj = b[sl], q[sl], kk[sl]
        cj = bj - jnp.log2(kj)
        prods = []
        for s in range(c):
            for top in (True, False):
                kind = reach(s, top)
                if kind == "none":
                    prods.append(zeros)
                    continue
                hs = slice(0, half) if top else slice(half, c)
                decay_k = jnp.exp2(bj[hs] - cj[s:s + 1])
                if kind == "tri":
                    s_in = s % half
                    decay_k = jnp.where((row <= s_in) if reverse else (row >= s_in), decay_k, 0.0)
                prods.append(qj[hs] * decay_k)
        diag = _dot(jnp.concatenate(prods, axis=0).astype(BF16), jv)
        if reverse:
            keys, edge = slice((j + 1) * c, tc), (j + 1) * c
        else:
            keys, edge = slice(0, j * c), j * c - 1
        if keys.stop == keys.start:
            return diag, None, keys
        ref_b = b[edge:edge + 1]
        qd = qj * jnp.exp2(bj - ref_b)
        kd = (kk[keys] * jnp.exp2(ref_b - b[keys])).astype(BF16)
        qm = jnp.concatenate([jnp.where(lane_head == h, qd, 0.0) for h in range(HG_HEADS)], axis=0)
        return diag, _dot_nt(qm.astype(BF16), kd), keys

    def output_stage(j, diag, off, keys):
        sl = slice(j * c, (j + 1) * c)
        vj = v[sl]
        o_top, o_bot = o_inter[j * c:j * c + half], o_inter[j * c + half:(j + 1) * c]
        for s in range(c):
            if reach(s, True) != "none":
                o_top = o_top + diag[s * c:s * c + half] * vj[s:s + 1]
            if reach(s, False) != "none":
                o_bot = o_bot + diag[s * c + half:(s + 1) * c] * vj[s:s + 1]
        o = jnp.concatenate([o_top, o_bot], axis=0)
        if off is not None:
            ov = _dot(off.astype(BF16), v_bf[keys])
            for h in range(HG_HEADS):
                o = o + jnp.where(lane_head == h, ov[h * c:(h + 1) * c], 0.0)
        return o

    depth = 2
    staged = [score_stage(j) for j in range(min(depth, n_sub))]
    outs = []
    for j in range(n_sub):
        cur = staged.pop(0)
        if j + depth < n_sub:
            staged.append(score_stage(j + depth))
        outs.append(output_stage(j, *cur))
    return jnp.concatenate(outs, axis=0)


def _hgrn_kernel(hg_ref, lb_ref, gn_ref, s0_ref, tri_ref, jv_ref, o_ref, sfin_ref, st_scr, of_scr, q_scr, *, tc, nc):
    p = pl.program_id(1)
    i = pl.program_id(2)
    r_i = lax.broadcasted_iota(jnp.int32, (HG_W, HG_W), 0) // HG_DV
    c_i = lax.broadcasted_iota(jnp.int32, (HG_W, HG_W), 1) // HG_DK
    bd_mask = r_i == c_i
    jv = jv_ref[...]

    @pl.when(i == 0)
    def _():
        st_scr[...] = s0_ref[0, p]

    @pl.when(p == 0)
    def _():
        hq = hg_ref[0, :, 0:HG_W]
        q = hq * _sigmoid(hq)
        rows = pl.ds(pl.multiple_of(i * tc, tc), tc)
        q_scr[rows, :] = q
        of_scr[rows, :] = _hgrn_direction(hg_ref, q, lb_ref[0:1, :], tri_ref[0], jv, bd_mask, st_scr, HG_W, False, tc)

    @pl.when(p == 1)
    def _():
        rows = pl.ds(pl.multiple_of((nc - 1 - i) * tc, tc), tc)
        o_b = _hgrn_direction(hg_ref, q_scr[rows, :], lb_ref[1:2, :], tri_ref[1], jv, bd_mask, st_scr, 2 * HG_W, True, tc)
        o = of_scr[rows, :] + o_b
        ms = _split_dot(o * o, jv) * (1.0 / HG_DV)
        g = hg_ref[0, :, 4 * HG_W:5 * HG_W]
        o_ref[0] = (o * lax.rsqrt(ms + RMS_EPS) * gn_ref[...] * (g * _sigmoid(g))).astype(o_ref.dtype)

    @pl.when(i == nc - 1)
    def _():
        sfin_ref[0, p] = st_scr[...]


def _hgrn(hg, lb2, gn, s0t, tc=256):
    bsz, t_len, _ = hg.shape
    nc = t_len // tc
    lower = np.arange(tc)[:, None] >= np.arange(tc)[None, :]
    tri = jnp.asarray(np.stack([lower, lower.T]).astype(np.float32))
    chunk = lambda b, p, i: (b, i + p * (nc - 1 - 2 * i), 0)
    o, sfin = pl.pallas_call(
        functools.partial(_hgrn_kernel, tc=tc, nc=nc),
        grid=(bsz, 2, nc),
        in_specs=[
            pl.BlockSpec((1, tc, HG_COLS), chunk),
            pl.BlockSpec((2, HG_W), lambda b, p, i: (0, 0)),
            pl.BlockSpec((1, HG_W), lambda b, p, i: (0, 0)),
            pl.BlockSpec((1, 2, HG_W, HG_W), lambda b, p, i: (b, 0, 0, 0)),
            pl.BlockSpec((2, tc, tc), lambda b, p, i: (0, 0, 0)),
            pl.BlockSpec((HG_W, HG_W), lambda b, p, i: (0, 0)),
        ],
        out_specs=[
            pl.BlockSpec((1, tc, HG_W), lambda b, p, i: (b, nc - 1 - i * p, 0)),
            pl.BlockSpec((1, 2, HG_W, HG_W), lambda b, p, i: (b, 0, 0, 0)),
        ],
        out_shape=[jax.ShapeDtypeStruct((bsz, t_len, HG_W), BF16),
                   jax.ShapeDtypeStruct((bsz, 2, HG_W, HG_W), F32)],
        scratch_shapes=[pltpu.VMEM((HG_W, HG_W), F32), pltpu.VMEM((t_len, HG_W), F32), pltpu.VMEM((t_len, HG_W), F32)],
        compiler_params=_cparams(("parallel", "arbitrary", "arbitrary")),
        name="hgrn",
    )(hg, lb2, jnp.tile(gn, HG_HEADS)[None], s0t, tri, _block_ones(HG_W, HG_DV))
    return o, sfin


def _state_to_blockdiag_t(s0):
    bsz = s0.shape[0]
    s0t = jnp.swapaxes(s0, -1, -2)
    eye = jnp.eye(HG_HEADS, dtype=s0.dtype)
    return jnp.einsum("bdhvk,hg->bdhvgk", s0t, eye).reshape(bsz, 2, HG_W, HG_W)


def _blockdiag_t_to_state(st):
    bsz = st.shape[0]
    s6 = st.reshape(bsz, 2, HG_HEADS, HG_DV, HG_HEADS, HG_DK)
    diag = jnp.stack([s6[:, :, h, :, h, :] for h in range(HG_HEADS)], axis=2)
    return jnp.swapaxes(diag, -1, -2)


def _post_kernel(x_ref, oa_ref, on_ref, od_ref, oh_ref, sh_ref, sc_ref, g1_ref, wg_ref, wb_ref, wo_ref,
                 lg_ref, lb_ref, o_ref, *, alpha):
    x = x_ref[0]
    h = (_ln_plain(x) * (1.0 + sc_ref[0]) + sh_ref[0]).astype(BF16)
    mixed = None
    for n, br_ref in enumerate((oa_ref, on_ref, od_ref, oh_ref)):
        gate = _sigmoid(_dot(h, wg_ref[:, n * D_MODEL:(n + 1) * D_MODEL]))
        term = gate * _dot(br_ref[0], wb_ref[n])
        mixed = term if mixed is None else mixed + term
    y = _dot(mixed.astype(BF16), wo_ref[...])
    o_ref[0] = _ln_plain(alpha * x + g1_ref[0] * y) * lg_ref[...] + lb_ref[...]


def _post_mixer(x, branches, mods, w_gates, w_branch, w_out, ln_g, ln_b, alpha, tm=256):
    bsz, t_len, d = x.shape
    tok = lambda n: pl.BlockSpec((1, tm, n), lambda b, i: (b, i, 0))
    const = lambda shape: pl.BlockSpec(shape, lambda b, i: (0,) * len(shape))
    mod = lambda k: _mod_spec(mods, k, 2)
    return pl.pallas_call(
        functools.partial(_post_kernel, alpha=alpha),
        grid=(bsz, t_len // tm),
        in_specs=[tok(d)] + [tok(BRANCH_W)] * 4 + [mod(MOD_SHIFT1), mod(MOD_SCALE1), mod(MOD_GATE1),
                  const((d, N_BRANCH * d)), const((N_BRANCH, BRANCH_W, d)), const((d, d)),
                  const((1, d)), const((1, d))],
        out_specs=tok(d),
        out_shape=jax.ShapeDtypeStruct((bsz, t_len, d), F32),
        compiler_params=_cparams(("parallel", "parallel")),
        name="post_mixer",
    )(x, *branches, mods, mods, mods, w_gates, w_branch, w_out, ln_g[None], ln_b[None])


ROUTER_LANES = 128
MOE_BT = 1024
MOE_ROWS = 128
MOE_STEPS = N_EXPERTS // 2
MOE_CHUNKS = 4


ROUTER_ROWS = 24


def _router_weights(logits):
    lt = jnp.transpose(logits)[0:ROUTER_ROWS, :]
    row = lax.broadcasted_iota(jnp.int32, lt.shape, 0)
    first_at = lambda mask: jnp.min(jnp.where(mask, row, ROUTER_LANES), axis=0, keepdims=True)
    is_g = row < N_GROUPS
    gl = jnp.where(is_g, lt, NEG_BIG)
    g_max = jnp.max(gl, axis=0, keepdims=True)
    g_idx = first_at(is_g & (gl == g_max))
    g_prob = 1.0 / jnp.sum(jnp.where(is_g, jnp.exp(gl - g_max), 0.0), axis=0, keepdims=True)
    lo = N_GROUPS + g_idx * EXPERTS_PER_GROUP
    in_grp = (row >= lo) & (row < lo + EXPERTS_PER_GROUP)
    el = jnp.where(in_grp, lt, NEG_BIG)
    v1 = jnp.max(el, axis=0, keepdims=True)
    i1 = first_at(in_grp & (el == v1))
    rest = in_grp & (row != i1)
    el2 = jnp.where(rest, lt, NEG_BIG)
    v2 = jnp.max(el2, axis=0, keepdims=True)
    i2 = first_at(rest & (el2 == v2))
    e2 = jnp.exp(v2 - v1)
    w1 = g_prob / (1.0 + e2)
    w2 = g_prob * e2 / (1.0 + e2)
    picked = jnp.where(row == g_idx, 1.0, jnp.where(row == i1, w1, jnp.where(row == i2, w2, 0.0)))
    pad = jnp.zeros((ROUTER_LANES - ROUTER_ROWS, lt.shape[1]), F32)
    return jnp.transpose(jnp.concatenate([picked, pad], axis=0))


def _moe_kernel(x_ref, sh_ref, sc_ref, g2_ref, wr_ref, br_ref, lt_ref, wg_ref, wu_ref, wd_ref, lg_ref, lb_ref, o_ref,
                xs_scr, pt_scr, cws_scr, ys_scr, seg_ref, *, alpha):
    s = pl.program_id(2)
    bt = xs_scr.shape[0]
    rows = MOE_ROWS

    @pl.when(s == 0)
    def _():
        x_parts, logit_parts = [], []
        for r in range(MOE_CHUNKS):
            sl = slice(r * bt // MOE_CHUNKS, (r + 1) * bt // MOE_CHUNKS)
            xm = _ln_plain(x_ref[0, sl, :]) * (1.0 + sc_ref[0]) + sh_ref[0]
            hi = xm.astype(BF16)
            lo = (xm - hi.astype(F32)).astype(BF16)
            x_parts.append(hi)
            logit_parts.append(_dot(hi, wr_ref[0]) + _dot(lo, wr_ref[0]) + _dot(hi, wr_ref[1]) + br_ref[...])
        x_hi = jnp.concatenate(x_parts, axis=0)
        logits = jnp.concatenate(logit_parts, axis=0)
        routed = _router_weights(logits)
        lane = lax.broadcasted_iota(jnp.int32, (bt, ROUTER_LANES), 1)
        onehot = jnp.where(lane < N_GROUPS, routed, 0.0)
        in_g = onehot > 0.0
        cw = routed - onehot
        before = _dot(lt_ref[...], onehot.astype(BF16))
        count = jnp.sum(onehot, axis=0, keepdims=True)
        lane1 = lax.broadcasted_iota(jnp.int32, (1, ROUTER_LANES), 1)
        start = jnp.zeros((1, ROUTER_LANES), F32)
        for k in range(N_GROUPS - 1):
            start = start + jnp.where(lane1 > k, count[:, k:k + 1], 0.0)
        dest = jnp.sum(jnp.where(in_g, start + before, 0.0), axis=-1, keepdims=True)
        dest_row = jnp.transpose(jnp.broadcast_to(dest, (bt, ROUTER_LANES)))[0:1, :]
        pt_scr[...] = jnp.where(dest.astype(jnp.int32) == lax.broadcasted_iota(jnp.int32, (bt, bt), 1),
                                1.0, 0.0).astype(BF16)
        perm = jnp.where(dest_row.astype(jnp.int32) == lax.broadcasted_iota(jnp.int32, (bt, bt), 0),
                         1.0, 0.0).astype(BF16)
        xs_scr[...] = _dot(perm, x_hi).astype(BF16)
        c_hi = cw.astype(BF16)
        c_lo = (cw - c_hi.astype(F32)).astype(BF16)
        cws_scr[...] = _dot(perm, c_hi) + _dot(perm, c_lo)
        ys_scr[...] = jnp.zeros_like(ys_scr)
        for g in range(1, N_GROUPS):
            seg_ref[g] = jnp.sum(jnp.where(lane1 == g, start, 0.0)).astype(jnp.int32)

    g = s // (MOE_STEPS // N_GROUPS)
    seg_lo = jnp.where(g == 0, 0, seg_ref[jnp.maximum(g, 1)])
    seg_hi = jnp.where(g == N_GROUPS - 1, bt, seg_ref[jnp.minimum(g + 1, N_GROUPS - 1)])
    t_lo = seg_lo // rows
    t_hi = jnp.where(seg_hi > seg_lo, (seg_hi + rows - 1) // rows, t_lo)
    lane_r = lax.broadcasted_iota(jnp.int32, (rows, ROUTER_LANES), 1)

    def gate_up(t):
        xt = xs_scr[pl.ds(pl.multiple_of(t * rows, rows), rows), :]
        return [(_dot(xt, wg_ref[e]), _dot(xt, wu_ref[e])) for e in range(2)]

    def down(t, gu):
        r0 = pl.multiple_of(t * rows, rows)
        cwt = cws_scr[pl.ds(r0, rows), :]
        hids = []
        for e, (gate, up) in enumerate(gu):
            w_e = jnp.sum(jnp.where(lane_r == N_GROUPS + 2 * s + e, cwt, 0.0), axis=-1, keepdims=True)
            hids.append((gate * _sigmoid(gate) * up * w_e).astype(BF16))
        ys_scr[pl.ds(r0, rows), :] += _dot(hids[0], wd_ref[0]) + _dot(hids[1], wd_ref[1])

    def tile_pair(i, carry):
        t = t_lo + 2 * i
        gu_a, gu_b = gate_up(t), gate_up(t + 1)
        down(t, gu_a)
        down(t + 1, gu_b)
        return carry

    n_tiles = t_hi - t_lo
    lax.fori_loop(0, n_tiles // 2, tile_pair, 0)

    @pl.when(n_tiles % 2 == 1)
    def _():
        down(t_hi - 1, gate_up(t_hi - 1))

    @pl.when(s == MOE_STEPS - 1)
    def _():
        ys = ys_scr[...]
        hi = ys.astype(BF16)
        lo = (ys - hi.astype(F32)).astype(BF16)
        rows_of = lambda r: slice(r * bt // MOE_CHUNKS, (r + 1) * bt // MOE_CHUNKS)
        back = lambda r: _dot(pt_scr[rows_of(r), :], hi) + _dot(pt_scr[rows_of(r), :], lo)
        y_next = back(0)
        for r in range(MOE_CHUNKS):
            y, y_next = y_next, (back(r + 1) if r + 1 < MOE_CHUNKS else None)
            sl = rows_of(r)
            o_ref[0, sl, :] = _ln_plain(alpha * x_ref[0, sl, :] + g2_ref[0] * y) * lg_ref[...] + lb_ref[...]


def _moe(x, mods, w_route, b_route, w_gate, w_up, w_down, ln_g, ln_b, alpha):
    bsz, t_len, d = x.shape
    bt = min(MOE_BT, t_len)
    assert t_len % bt == 0 and bt % MOE_ROWS == 0 and bt % (8 * MOE_CHUNKS) == 0
    tok = pl.BlockSpec((1, bt, d), lambda b, i, e: (b, i, 0))
    const = lambda shape: pl.BlockSpec(shape, lambda b, i, e: (0,) * len(shape))
    mod = lambda k: _mod_spec(mods, k, 3)
    pair = lambda shape: pl.BlockSpec((2,) + shape, lambda b, i, e: (e, 0, 0))
    lower = jnp.asarray(np.tril(np.ones((bt, bt), np.float32), -1), dtype=BF16)
    return pl.pallas_call(
        functools.partial(_moe_kernel, alpha=alpha),
        grid=(bsz, t_len // bt, MOE_STEPS),
        in_specs=[tok, mod(MOD_SHIFT2), mod(MOD_SCALE2), mod(MOD_GATE2), const((2, d, ROUTER_LANES)), const((1, ROUTER_LANES)), const((bt, bt)),
                  pair((d, D_EXPERT)), pair((d, D_EXPERT)), pair((D_EXPERT, d)), const((1, d)), const((1, d))],
        out_specs=tok,
        out_shape=jax.ShapeDtypeStruct((bsz, t_len, d), F32),
        scratch_shapes=[pltpu.VMEM((bt, d), BF16), pltpu.VMEM((bt, bt), BF16), pltpu.VMEM((bt, ROUTER_LANES), F32),
                        pltpu.VMEM((bt, d), F32), pltpu.SMEM((N_GROUPS,), jnp.int32)],
        compiler_params=_cparams(("parallel", "parallel", "arbitrary")),
        name="moe",
    )(x, mods, mods, mods, w_route, b_route, lower, w_gate, w_up, w_down, ln_g[None], ln_b[None])


def _layer(x, mods, lp, ctx, alpha):
    bsz, t_len, _ = x.shape
    latent = ctx is not None
    past = ctx["gk_t"].shape[-1] if latent else 0
    outs = _inproj(x, mods, lp["w_heads"], lp["gqa_qn"], lp["gqa_kn"], latent, past)
    qa, kat, va, qn, kn, vn, qd, kdt, vd, hg = outs[:10]
    if latent:
        kat = lax.dynamic_update_slice(kat, ctx["gk_t"], (0, 0, 0))
        va = lax.dynamic_update_slice(va, ctx["gv_ext"], (0, 0, 0))
        kdt = lax.dynamic_update_slice(kdt, ctx["dk_t"], (0, 0, 0))
        vd = lax.dynamic_update_slice(vd, ctx["dv"], (0, 0, 0))
        s0t = _state_to_blockdiag_t(ctx["state"].astype(F32))
        tq = 256
    else:
        s0t = jnp.zeros((bsz, 2, HG_W, HG_W), F32)
        tq = t_len
    o_a = _attention(qa, kat, va, GQA_Q_HEADS, GQA_KV_HEADS, tq)
    if latent:
        o_n = _na_attention(qn, kn, vn, ctx["nk_ext"], ctx["nv_ext"], lp["na_bias"])
    else:
        o_n = _attention(qn, kn, vn, NA_HEADS, NA_HEADS, tq)
    o_d = _diff_attention(qd, kdt, vd, lp["diff_par"], tq)
    o_h, sfin = _hgrn(hg, lp["hg_lb"], lp["hg_norm"], s0t)
    x1 = _post_mixer(x, (o_a, o_n, o_d, o_h), mods, lp["w_gates"], lp["w_branch"], lp["w_out"],
                     lp["ln_mix_g"], lp["ln_mix_b"], alpha)
    x1r = x1 if latent else x1.reshape(1, bsz * t_len, D_MODEL)
    y = _moe(x1r, mods, lp["w_route"], lp["b_route"], lp["w_gate"], lp["w_up"], lp["w_down"],
             lp["ln_ffn_g"], lp["ln_ffn_b"], alpha).reshape(bsz, t_len, D_MODEL)
    caches = None
    if not latent:
        ka, va32, kn32, vn32, kd32, vd32 = outs[10:]
        caches = (ka.reshape(bsz, t_len, GQA_KV_HEADS, HEAD_DIM), va32.reshape(bsz, t_len, GQA_KV_HEADS, HEAD_DIM),
                  kn32.reshape(bsz, t_len, NA_HEADS, HEAD_DIM), vn32.reshape(bsz, t_len, NA_HEADS, HEAD_DIM),
                  kd32.reshape(bsz, t_len, DIFF_HEADS, 2, DIFF_QK_DIM), vd32.reshape(bsz, t_len, DIFF_HEADS, DIFF_V_DIM),
                  _blockdiag_t_to_state(sfin))
    return y, caches


def _padded_heads(a, fill):
    a = a.astype(BF16)
    return jnp.concatenate([a, jnp.full(a.shape, fill, BF16)], axis=-1).reshape(a.shape[:3] + (-1,))


def _keys_t(a):
    a = a.astype(BF16).reshape(a.shape[:3] + (-1,))
    return jnp.swapaxes(a, 2, 3)


def kernel(x_prompt, x_sample, cache_gqa_k, cache_gqa_v, cache_na_k, cache_na_v, cache_diff_k, cache_diff_v, state_hgrn, c, c_ctx, w_ada, b_ada, w_in, gqa_q_norm, gqa_k_norm, na_rpb, diff_lambda, diff_subln, hgrn_lb, hgrn_norm, w_branch, w_out, ln_mix_g, ln_mix_b, ln_ffn_g, ln_ffn_b, w_group, b_group, w_router, b_router, w_gate, w_up, w_down):
    depth = w_in.shape[0]
    d = D_MODEL
    alpha = (2 * depth) ** 0.25
    dec_b = c.shape[0]

    lb = jax.nn.softmax(hgrn_lb.astype(F32), axis=1)
    lb = jnp.cumsum(lb, axis=1) - lb[:, :1]

    cond_rows = -(-(dec_b + 1) // 8) * 8
    cond = jnp.zeros((cond_rows, d), F32).at[:dec_b].set(c).at[dec_b].set(c_ctx)
    mods_all = _ada_mods(cond, w_ada, b_ada)

    na_bias = _na_bias_table(na_rpb)
    layers = []
    for l in range(depth):
        lam_init = 0.8 - 0.6 * math.exp(-0.3 * l)
        lp_lam = diff_lambda[l].astype(F32)
        lam = jnp.exp(jnp.sum(lp_lam[0] * lp_lam[1])) - jnp.exp(jnp.sum(lp_lam[2] * lp_lam[3])) + lam_init
        diff_par = jnp.zeros((8, DIFF_V_DIM), F32).at[0].set(lam).at[1].set(diff_subln[l] * (1.0 - lam_init))
        w_route = jnp.zeros((d, ROUTER_LANES), F32)
        w_route = w_route.at[:, :N_GROUPS].set(w_group[l]).at[:, N_GROUPS:N_GROUPS + N_EXPERTS].set(
            w_router[l].reshape(d, N_EXPERTS))
        w_route_hi = w_route.astype(BF16)
        b_route = jnp.zeros((1, ROUTER_LANES), F32)
        b_route = b_route.at[0, :N_GROUPS].set(b_group[l]).at[0, N_GROUPS:N_GROUPS + N_EXPERTS].set(
            b_router[l].reshape(N_EXPERTS))
        layers.append({
            "w_heads": w_in[l, :, :D_HEADS_IN].astype(BF16),
            "w_gates": w_in[l, :, D_HEADS_IN:].astype(BF16),
            "gqa_qn": gqa_q_norm[l], "gqa_kn": gqa_k_norm[l],
            "na_bias": na_bias[l],
            "diff_par": diff_par,
            "hg_lb": lb[:, l], "hg_norm": hgrn_norm[l],
            "w_branch": w_branch[l].astype(BF16), "w_out": w_out[l].astype(BF16),
            "ln_mix_g": ln_mix_g[l], "ln_mix_b": ln_mix_b[l], "ln_ffn_g": ln_ffn_g[l], "ln_ffn_b": ln_ffn_b[l],
            "w_route": jnp.stack([w_route_hi, (w_route - w_route_hi.astype(F32)).astype(BF16)]), "b_route": b_route,
            "w_gate": w_gate[l].reshape(N_EXPERTS, d, D_EXPERT).astype(BF16),
            "w_up": w_up[l].reshape(N_EXPERTS, d, D_EXPERT).astype(BF16),
            "w_down": w_down[l].reshape(N_EXPERTS, D_EXPERT, d).astype(BF16),
        })

    xp = x_prompt
    ctx_out = []
    for l in range(depth):
        xp, caches = _layer(xp, mods_all[l, dec_b:dec_b + 1, None, :], layers[l], None, alpha)
        ctx_out.append(caches)
    stack = lambda i: jnp.stack([t[i] for t in ctx_out], axis=1)

    cached = {"gk_t": _keys_t(cache_gqa_k), "gv_ext": _padded_heads(cache_gqa_v, 1.0),
              "nk_ext": _padded_heads(cache_na_k, 0.0), "nv_ext": _padded_heads(cache_na_v, 1.0),
              "dk_t": _keys_t(cache_diff_k), "dv": _padded_heads(cache_diff_v, 1.0), "state": state_hgrn}
    xs = x_sample
    for l in range(depth):
        ctx = {name: t[:, l] for name, t in cached.items()}
        xs, _ = _layer(xs, mods_all[l, :dec_b, None, :], layers[l], ctx, alpha)

    return (xp, xs, stack(0), stack(1), stack(2), stack(3), stack(4), stack(5), stack(6))
```

```python
import functools
import math

import jax
import jax.numpy as jnp
import numpy as np
from jax import lax
from jax.experimental import pallas as pl
from jax.experimental.pallas import tpu as pltpu

F32 = jnp.float32
BF16 = jnp.bfloat16

D_MODEL = 1024
GRID_W = 64
HEAD_DIM = 64
GQA_Q_HEADS = 4
GQA_KV_HEADS = 2
NA_HEADS = 4
NA_KH = 8
NA_KW = 16
DIFF_HEADS = 4
DIFF_QK_DIM = 32
DIFF_V_DIM = 64
HG_HEADS = 4
HG_DK = 64
HG_DV = 64
BRANCH_W = 256
N_BRANCH = 4
N_GROUPS = 4
EXPERTS_PER_GROUP = 4
N_EXPERTS = N_GROUPS * EXPERTS_PER_GROUP
D_EXPERT = 512
ROPE_THETA = 10000.0
LN_EPS = 1e-5
RMS_EPS = 1e-6
F_FLOOR = 1e-30
NEG_BIG = -1e30
LOG2E = math.log2(math.e)
MOD_SHIFT1, MOD_SCALE1, MOD_GATE1, MOD_SHIFT2, MOD_SCALE2, MOD_GATE2 = range(6)

C_GQ, C_GK, C_GV = 0, 256, 384
C_NQ, C_NK, C_NV = 512, 768, 1024
C_DQ, C_DK, C_DV = 1280, 1536, 1792
C_HG = 2048
D_HEADS_IN = 3328
HG_COLS = 1280

VMEM_LIMIT = 56 * 1024 * 1024


def _cparams(sem):
    return pltpu.CompilerParams(dimension_semantics=sem, vmem_limit_bytes=VMEM_LIMIT)


def _dot(a, b):
    return jnp.dot(a, b, preferred_element_type=F32)


def _dot_nt(a, b):
    return lax.dot_general(a, b, (((1,), (1,)), ((), ())), preferred_element_type=F32)


def _split_dot(x, j):
    hi = x.astype(BF16)
    lo = (x - hi.astype(F32)).astype(BF16)
    return _dot(hi, j) + _dot(lo, j)


def _sigmoid(z):
    return 1.0 / (1.0 + jnp.exp(-z))


def _ln_plain(x):
    mu = jnp.mean(x, axis=-1, keepdims=True)
    xc = x - mu
    var = jnp.mean(xc * xc, axis=-1, keepdims=True)
    return xc * lax.rsqrt(var + LN_EPS)


def _mod_spec(mods, k, rank):
    per_batch = mods.shape[0] > 1
    d = mods.shape[-1] // 6
    if rank == 2:
        return pl.BlockSpec((1, 1, d), lambda b, i: (b if per_batch else 0, 0, k))
    return pl.BlockSpec((1, 1, d), lambda b, i, e: (b if per_batch else 0, 0, k))


def _block_ones(n, blk):
    idx = np.arange(n) // blk
    return jnp.asarray((idx[:, None] == idx[None, :]).astype(np.float32), dtype=BF16)


def _ada_kernel(c_ref, w_ref, b_ref, o_ref):
    c = c_ref[...]
    s = (c * _sigmoid(c)).astype(BF16)
    o_ref[0] = _dot(s, w_ref[0].astype(BF16)) + b_ref[0]


def _ada_mods(cond, w_ada, b_ada):
    depth, d, n = w_ada.shape
    rows = cond.shape[0]
    tn = 1536
    return pl.pallas_call(
        _ada_kernel,
        grid=(depth, n // tn),
        in_specs=[
            pl.BlockSpec((rows, d), lambda l, j: (0, 0)),
            pl.BlockSpec((1, d, tn), lambda l, j: (l, 0, j)),
            pl.BlockSpec((1, 1, tn), lambda l, j: (l, 0, j)),
        ],
        out_specs=pl.BlockSpec((1, rows, tn), lambda l, j: (l, 0, j)),
        out_shape=jax.ShapeDtypeStruct((depth, rows, n), F32),
        compiler_params=_cparams(("parallel", "parallel")),
        name="ada_mods",
    )(cond, w_ada, b_ada.reshape(depth, 1, n))


def _rope_tables(t_len, d, reps):
    quarter = d // 4
    t = np.arange(t_len)
    inv = ROPE_THETA ** (-np.arange(quarter, dtype=np.float64) / quarter)
    ang_r = (t // GRID_W).astype(np.float64)[:, None] * inv
    ang_c = (t % GRID_W).astype(np.float64)[:, None] * inv
    ang = np.concatenate([ang_r, ang_r, ang_c, ang_c], axis=-1)
    cos, sin = np.cos(ang), np.sin(ang)
    first = (np.arange(d) % (2 * quarter)) < quarter
    sin_a = np.where(first, -sin, 0.0)
    sin_b = np.where(first, 0.0, sin)
    tile = lambda a: jnp.asarray(np.tile(a, (1, reps)), dtype=F32)
    return tile(cos), tile(sin_a), tile(sin_b)


def _rope(x, cos, sin_a, sin_b, quarter):
    w = x.shape[-1]
    return x * cos + pltpu.roll(x, w - quarter, 1) * sin_a + pltpu.roll(x, quarter, 1) * sin_b


def _seg_rms(x, j):
    ss = _split_dot(x * x, j)
    return x * lax.rsqrt(ss * (1.0 / HEAD_DIM) + RMS_EPS)


def _pad_heads(x, heads, fill):
    pad = jnp.full((x.shape[0], HEAD_DIM), fill, x.dtype)
    pieces = []
    for h in range(heads):
        pieces += [x[:, h * HEAD_DIM:(h + 1) * HEAD_DIM], pad]
    return jnp.concatenate(pieces, axis=-1)


def _inproj_kernel(*refs, latent, skip):
    n_in = (17 if skip else 13) if latent else 7
    ins, outs = refs[:n_in], refs[n_in:]
    x_ref, sh_ref, sc_ref, w_ref, qn_ref, kn_ref, j_ref = ins[:7]
    qa_ref, kat_ref, va_ref, qnb_ref, knb_ref, vnb_ref, qd_ref, kdt_ref, vd_ref, hg_ref = outs[:10]

    def project():
        h = (_ln_plain(x_ref[0]) * (1.0 + sc_ref[0]) + sh_ref[0]).astype(BF16)
        proj = lambda c0, n: _dot(h, w_ref[:, c0:c0 + n])

        gq = _seg_rms(proj(C_GQ, 256), j_ref[...]) * qn_ref[...]
        gk = _seg_rms(proj(C_GK, 128), j_ref[0:128, 0:128]) * kn_ref[...]
        gv = proj(C_GV, 128)
        nq = proj(C_NQ, 256) * (LOG2E * HEAD_DIM ** -0.5)
        nk = proj(C_NK, 256)
        nv = proj(C_NV, 256)
        dq = proj(C_DQ, 256)
        dk = proj(C_DK, 256)
        dv = proj(C_DV, 256)
        if latent:
            c64, a64, b64, c32, a32, b32 = (r[...] for r in ins[7:13])
            gq = _rope(gq, c64, a64, b64, HEAD_DIM // 4)
            gk = _rope(gk, c64[:, 0:128], a64[:, 0:128], b64[:, 0:128], HEAD_DIM // 4)
            dq = _rope(dq, c32, a32, b32, DIFF_QK_DIM // 4)
            dk = _rope(dk, c32, a32, b32, DIFF_QK_DIM // 4)
        else:
            for ref, val in zip(outs[10:], (gk, gv, nk, nv, dk, dv)):
                ref[0] = val
        qa_ref[0] = (gq * (LOG2E * HEAD_DIM ** -0.5)).astype(BF16)
        kat_ref[0] = jnp.transpose(gk).astype(BF16)
        va_ref[0] = _pad_heads(gv, GQA_KV_HEADS, 1.0).astype(BF16)
        if latent:
            qnb_ref[0] = _pad_heads(nq, NA_HEADS, 0.0).astype(BF16)
            knb_ref[0] = _pad_heads(nk, NA_HEADS, 0.0).astype(BF16)
        else:
            qnb_ref[0] = nq.astype(BF16)
            knb_ref[0] = jnp.transpose(nk).astype(BF16)
        vnb_ref[0] = _pad_heads(nv, NA_HEADS, 1.0).astype(BF16)
        qd_ref[0] = (dq * (LOG2E * DIFF_QK_DIM ** -0.5)).astype(BF16)
        kdt_ref[0] = jnp.transpose(dk).astype(BF16)
        vd_ref[0] = _pad_heads(dv, DIFF_HEADS, 1.0).astype(BF16)
        for s in range(HG_COLS // 256):
            hg_ref[0, :, s * 256:(s + 1) * 256] = proj(C_HG + s * 256, 256)

    if not skip:
        project()
        return
    step = pl.program_id(1)

    @pl.when(step < skip)
    def _():
        for dst, src in zip((kat_ref, va_ref, kdt_ref, vd_ref), ins[13:17]):
            dst[...] = src[...]

    pl.when(step >= skip)(project)


def _inproj(x, mods, w_heads, qn, kn, latent, cached=(), tm=256):
    bsz, t_len, d = x.shape
    past = cached[0].shape[-1] if cached else 0
    assert past % tm == 0
    skip = past // tm
    tile = lambda i: jnp.maximum(i - skip, 0)
    tok = lambda n: pl.BlockSpec((1, tm, n), lambda b, i: (b, tile(i), 0))
    tok_t = lambda n: pl.BlockSpec((1, n, tm), lambda b, i: (b, 0, tile(i)))
    const = lambda shape: pl.BlockSpec(shape, lambda b, i: (0,) * len(shape))
    in_specs = [tok(d), _mod_spec(mods, MOD_SHIFT1, 2), _mod_spec(mods, MOD_SCALE1, 2),
                const((d, D_HEADS_IN)), const((1, 256)), const((1, 128)), const((256, 256))]
    args = [x, mods, mods, w_heads, jnp.tile(qn, GQA_Q_HEADS)[None], jnp.tile(kn, GQA_KV_HEADS)[None],
            _block_ones(256, HEAD_DIM)]
    if latent:
        in_specs += [pl.BlockSpec((tm, 256), lambda b, i: (tile(i), 0))] * 6
        args += list(_rope_tables(t_len, HEAD_DIM, 4)) + list(_rope_tables(t_len, DIFF_QK_DIM, 8))
    if skip:
        head = lambda i: jnp.minimum(i, skip - 1)
        for c in cached:
            if c.shape[-1] == past:
                in_specs.append(pl.BlockSpec((1, c.shape[1], tm), lambda b, i: (b, 0, head(i))))
            else:
                in_specs.append(pl.BlockSpec((1, tm, c.shape[2]), lambda b, i: (b, head(i), 0)))
        args += list(cached)
    rows = lambda n, dt=BF16: (jax.ShapeDtypeStruct((bsz, t_len, n), dt), tok(n))
    cols = lambda n: (jax.ShapeDtypeStruct((bsz, n, t_len), BF16), tok_t(n))
    rows_kv = lambda n: (jax.ShapeDtypeStruct((bsz, past + t_len, n), BF16),
                         pl.BlockSpec((1, tm, n), lambda b, i: (b, i, 0)))
    cols_kv = lambda n: (jax.ShapeDtypeStruct((bsz, n, past + t_len), BF16),
                         pl.BlockSpec((1, n, tm), lambda b, i: (b, 0, i)))
    outs = [rows(256), cols_kv(128), rows_kv(256),
            rows(512) if latent else rows(256), rows(512) if latent else cols(256), rows(512),
            rows(256), cols_kv(256), rows_kv(512), rows(HG_COLS, F32)]
    if not latent:
        outs += [rows(n, F32) for n in (128, 128, 256, 256, 256, 256)]
    return pl.pallas_call(
        functools.partial(_inproj_kernel, latent=latent, skip=skip),
        grid=(bsz, skip + t_len // tm),
        in_specs=in_specs,
        out_specs=[o[1] for o in outs],
        out_shape=[o[0] for o in outs],
        compiler_params=_cparams(("parallel", "arbitrary" if skip else "parallel")),
        name="in_proj_rope" if latent else "in_proj",
    )(*args)


def _softmax_p(s):
    return jnp.exp2(s - jnp.max(s, axis=-1, keepdims=True)).astype(BF16)


def _pv_norm(p, v_ext):
    dv = v_ext.shape[-1] // 2
    o = _dot(p, v_ext)
    return o[:, :dv] * (1.0 / o[:, dv:])


def _attn_kernel(q_ref, kt_ref, v_ref, o_ref, *, hq, group, d, dv):
    q = q_ref[0]
    tq = q.shape[0]
    n_groups = hq // group

    def scores(g):
        qs = jnp.concatenate([q[:, h * d:(h + 1) * d] for h in range(g * group, (g + 1) * group)], axis=0)
        return _dot(qs, kt_ref[0, g * d:(g + 1) * d, :])

    outs = []
    s_next = scores(0)
    for g in range(n_groups):
        s_cur, s_next = s_next, (scores(g + 1) if g + 1 < n_groups else None)
        o = _pv_norm(_softmax_p(s_cur), v_ref[0, :, g * 2 * dv:(g + 1) * 2 * dv])
        outs += [o[j * tq:(j + 1) * tq] for j in range(group)]
    o_ref[0] = jnp.concatenate(outs, axis=-1).astype(o_ref.dtype)


def _attention(q, kt, v_ext, hq, hkv, tq):
    bsz, t_len, qw = q.shape
    l_len = kt.shape[-1]
    d = kt.shape[1] // hkv
    dv = v_ext.shape[-1] // (2 * hkv)
    return pl.pallas_call(
        functools.partial(_attn_kernel, hq=hq, group=hq // hkv, d=d, dv=dv),
        grid=(bsz, t_len // tq),
        in_specs=[
            pl.BlockSpec((1, tq, qw), lambda b, i: (b, i, 0)),
            pl.BlockSpec((1, hkv * d, l_len), lambda b, i: (b, 0, 0)),
            pl.BlockSpec((1, l_len, hkv * 2 * dv), lambda b, i: (b, 0, 0)),
        ],
        out_specs=pl.BlockSpec((1, tq, hq * dv), lambda b, i: (b, i, 0)),
        out_shape=jax.ShapeDtypeStruct((bsz, t_len, hq * dv), BF16),
        compiler_params=_cparams(("parallel", "parallel")),
        name="attention",
    )(q, kt, v_ext)


def _diff_attn_kernel(q_ref, kt_ref, v_ref, par_ref, o_ref):
    q = q_ref[0]
    tq = q.shape[0]
    lam = par_ref[0:1, :]
    gain = par_ref[1:2, :]
    dq = DIFF_QK_DIM
    scores = lambda i: _dot(q[:, i * dq:(i + 1) * dq], kt_ref[0, i * dq:(i + 1) * dq, :])
    outs = []
    s_next = (scores(0), scores(1))
    for h in range(DIFF_HEADS):
        s_cur, s_next = s_next, ((scores(2 * h + 2), scores(2 * h + 3)) if h + 1 < DIFF_HEADS else None)
        o12 = _pv_norm(jnp.concatenate([_softmax_p(s_cur[0]), _softmax_p(s_cur[1])], axis=0),
                       v_ref[0, :, h * 2 * DIFF_V_DIM:(h + 1) * 2 * DIFF_V_DIM])
        o = o12[:tq] - lam * o12[tq:]
        ms = jnp.mean(o * o, axis=-1, keepdims=True)
        outs.append(o * lax.rsqrt(ms + RMS_EPS) * gain)
    o_ref[0] = jnp.concatenate(outs, axis=-1).astype(o_ref.dtype)


def _diff_attention(q, kt, v_ext, par, tq):
    bsz, t_len, qw = q.shape
    l_len = kt.shape[-1]
    return pl.pallas_call(
        _diff_attn_kernel,
        grid=(bsz, t_len // tq),
        in_specs=[
            pl.BlockSpec((1, tq, qw), lambda b, i: (b, i, 0)),
            pl.BlockSpec((1, 2 * DIFF_HEADS * DIFF_QK_DIM, l_len), lambda b, i: (b, 0, 0)),
            pl.BlockSpec((1, l_len, DIFF_HEADS * 2 * DIFF_V_DIM), lambda b, i: (b, 0, 0)),
            pl.BlockSpec((8, DIFF_V_DIM), lambda b, i: (0, 0)),
        ],
        out_specs=pl.BlockSpec((1, tq, DIFF_HEADS * DIFF_V_DIM), lambda b, i: (b, i, 0)),
        out_shape=jax.ShapeDtypeStruct((bsz, t_len, DIFF_HEADS * DIFF_V_DIM), BF16),
        compiler_params=_cparams(("parallel", "parallel")),
        name="diff_attention",
    )(q, kt, v_ext, par)


def _na_bias_table(rpb):
    shift = np.arange(NA_KH)[:, None]
    win_row = np.arange(NA_KH)[None, :]
    row_off = win_row - shift + (NA_KH - 1)
    col = np.arange(GRID_W)
    col_start = np.clip(col - NA_KW // 2, 0, GRID_W - NA_KW)
    key_col = np.arange(GRID_W)[None, :]
    valid = (key_col >= col_start[:, None]) & (key_col < col_start[:, None] + NA_KW)
    col_off = key_col - col[:, None] + (NA_KW - 1)
    row_sel = (row_off[:, :, None] == np.arange(2 * NA_KH - 1)).astype(np.float32)
    col_sel = ((col_off[:, :, None] == np.arange(2 * NA_KW - 1)) & valid[:, :, None]).astype(np.float32)
    tbl = jnp.einsum("sia,lhab,cxb->lshcix", row_sel, rpb.astype(F32) * LOG2E, col_sel,
                     precision=lax.Precision.HIGHEST)
    tbl = jnp.where(valid[None, None, None, :, None, :], tbl, NEG_BIG)
    return tbl.reshape(rpb.shape[0], NA_KH, rpb.shape[1], GRID_W, NA_KH * GRID_W)


def _na_kernel(q_ref, k_ref, v_ref, kc_ref, vc_ref, bias_ref, o_ref, *, rows_per_step, rows):
    i = pl.program_id(1)
    n_win = NA_KH * GRID_W
    units = [(rr, h) for rr in range(rows_per_step) for h in range(NA_HEADS)]

    def window(rr):
        r = i * rows_per_step + rr
        r_start = jnp.clip(r - NA_KH // 2, 0, rows - NA_KH)
        return r - r_start, pl.multiple_of(r_start * GRID_W, GRID_W)

    def scores(unit):
        rr, h = unit
        shift, start = window(rr)
        lanes = slice(h * 2 * HEAD_DIM, (h + 1) * 2 * HEAD_DIM)
        qh = q_ref[0, rr * GRID_W:(rr + 1) * GRID_W, lanes]
        s_loc = _dot_nt(qh, k_ref[0, pl.ds(start, n_win), lanes]) + bias_ref[shift, h]
        return s_loc, _dot_nt(qh, kc_ref[0, :, lanes])

    def output(unit, s_loc, s_ctx):
        rr, h = unit
        _, start = window(rr)
        m = jnp.maximum(jnp.max(s_loc, axis=-1, keepdims=True), jnp.max(s_ctx, axis=-1, keepdims=True))
        p_loc = jnp.exp2(s_loc - m).astype(BF16)
        p_ctx = jnp.exp2(s_ctx - m).astype(BF16)
        lanes = slice(h * 2 * HEAD_DIM, (h + 1) * 2 * HEAD_DIM)
        o = _dot(p_loc, v_ref[0, pl.ds(start, n_win), lanes]) + _dot(p_ctx, vc_ref[0, :, lanes])
        return o[:, :HEAD_DIM] * (1.0 / o[:, HEAD_DIM:])

    depth = 4
    staged = [scores(u) for u in units[:depth]]
    outs = []
    for n, unit in enumerate(units):
        cur = staged.pop(0)
        if n + depth < len(units):
            staged.append(scores(units[n + depth]))
        outs.append(output(unit, *cur))
        if unit[1] == NA_HEADS - 1:
            rr = unit[0]
            o_ref[0, rr * GRID_W:(rr + 1) * GRID_W, :] = jnp.concatenate(outs, axis=-1).astype(o_ref.dtype)
            outs = []


def _na_attention(q, k, v_ext, kc, vc_ext, bias, rows_per_step=8):
    bsz, n, qw = q.shape
    rows = n // GRID_W
    assert rows >= NA_KH and rows % rows_per_step == 0
    l_ctx = kc.shape[1]
    tq = rows_per_step * GRID_W
    full = lambda length: pl.BlockSpec((1, length, qw), lambda b, i: (b, 0, 0))
    return pl.pallas_call(
        functools.partial(_na_kernel, rows_per_step=rows_per_step, rows=rows),
        grid=(bsz, rows // rows_per_step),
        in_specs=[
            pl.BlockSpec((1, tq, qw), lambda b, i: (b, i, 0)),
            full(n), full(n), full(l_ctx), full(l_ctx),
            pl.BlockSpec(bias.shape, lambda b, i: (0, 0, 0, 0)),
        ],
        out_specs=pl.BlockSpec((1, tq, NA_HEADS * HEAD_DIM), lambda b, i: (b, i, 0)),
        out_shape=jax.ShapeDtypeStruct((bsz, n, NA_HEADS * HEAD_DIM), BF16),
        compiler_params=_cparams(("parallel", "parallel")),
        name="na_attention",
    )(q, k, v_ext, kc, vc_ext, bias)


HG_SUB = 16
HG_W = HG_HEADS * HG_DK


def _hgrn_direction(hg_ref, q, lb, tri, jv, bd_mask, st_ref, z_col, reverse, tc):
    c = HG_SUB
    n_sub = tc // c
    z = hg_ref[0, :, z_col:z_col + HG_W]
    v = hg_ref[0, :, 3 * HG_W:4 * HG_W]
    sig = _sigmoid(z)
    f = lb + (1.0 - lb) * sig
    log_f = jnp.log(jnp.maximum(f, F_FLOOR))
    kk = (1.0 - lb) * (1.0 - sig)
    b = jnp.dot(tri, log_f * LOG2E, preferred_element_type=F32, precision=lax.Precision.HIGHEST)
    st = st_ref[...]
    o_inter = _dot_nt((q * jnp.exp2(b)).astype(BF16), st.astype(BF16))
    b_end = b[0:1] if reverse else b[tc - 1:tc]
    upd = _dot(jnp.transpose(v).astype(BF16), (kk * jnp.exp2(b_end - b)).astype(BF16))
    st_ref[...] = jnp.exp2(b_end) * st + jnp.where(bd_mask, upd, 0.0)

    half = c // 2
    row = lax.broadcasted_iota(jnp.int32, (half, HG_W), 0)
    lane_head = lax.broadcasted_iota(jnp.int32, (c, HG_W), 1) // HG_DK
    v_bf = v.astype(BF16)
    zeros = jnp.zeros((half, HG_W), F32)
    def reach(s, top):
        if reverse:
            return ("all" if s >= half else "tri") if top else ("none" if s < half else "tri")
        return ("none" if s >= half else "tri") if top else ("all" if s < half else "tri")

    def score_stage(j):
        sl = slice(j * c, (j + 1) * c)
        bj, qj, kj = b[sl], q[sl], kk[sl]
        cj = bj - jnp.log2(kj)
        prods = []
        for s in range(c):
            for top in (True, False):
                kind = reach(s, top)
                if kind == "none":
                    prods.append(zeros)
                    continue
                hs = slice(0, half) if top else slice(half, c)
                decay_k = jnp.exp2(bj[hs] - cj[s:s + 1])
                if kind == "tri":
                    s_in = s % half
                    decay_k = jnp.where((row <= s_in) if reverse else (row >= s_in), decay_k, 0.0)
                prods.append(qj[hs] * decay_k)
        diag = _dot(jnp.concatenate(prods, axis=0).astype(BF16), jv)
        if reverse:
            keys, edge = slice((j + 1) * c, tc), (j + 1) * c
        else:
            keys, edge = slice(0, j * c), j * c - 1
        if keys.stop == keys.start:
            return diag, None, keys
        ref_b = b[edge:edge + 1]
        qd = qj * jnp.exp2(bj - ref_b)
        kd = (kk[keys] * jnp.exp2(ref_b - b[keys])).astype(BF16)
        qm = jnp.concatenate([jnp.where(lane_head == h, qd, 0.0) for h in range(HG_HEADS)], axis=0)
        return diag, _dot_nt(qm.astype(BF16), kd), keys

    def output_stage(j, diag, off, keys):
        sl = slice(j * c, (j + 1) * c)
        vj = v[sl]
        o_top, o_bot = o_inter[j * c:j * c + half], o_inter[j * c + half:(j + 1) * c]
        for s in range(c):
            if reach(s, True) != "none":
                o_top = o_top + diag[s * c:s * c + half] * vj[s:s + 1]
            if reach(s, False) != "none":
                o_bot = o_bot + diag[s * c + half:(s + 1) * c] * vj[s:s + 1]
        o = jnp.concatenate([o_top, o_bot], axis=0)
        if off is not None:
            ov = _dot(off.astype(BF16), v_bf[keys])
            for h in range(HG_HEADS):
                o = o + jnp.where(lane_head == h, ov[h * c:(h + 1) * c], 0.0)
        return o

    depth = 2
    staged = [score_stage(j) for j in range(min(depth, n_sub))]
    outs = []
    for j in range(n_sub):
        cur = staged.pop(0)
        if j + depth < n_sub:
            staged.append(score_stage(j + depth))
        outs.append(output_stage(j, *cur))
    return jnp.concatenate(outs, axis=0)


def _hgrn_kernel(hg_ref, lb_ref, gn_ref, s0_ref, tri_ref, jv_ref, o_ref, sfin_ref, st_scr, of_scr, q_scr, *, tc, nc):
    p = pl.program_id(1)
    i = pl.program_id(2)
    r_i = lax.broadcasted_iota(jnp.int32, (HG_W, HG_W), 0) // HG_DV
    c_i = lax.broadcasted_iota(jnp.int32, (HG_W, HG_W), 1) // HG_DK
    bd_mask = r_i == c_i
    jv = jv_ref[...]

    @pl.when(i == 0)
    def _():
        st_scr[...] = s0_ref[0, p]

    @pl.when(p == 0)
    def _():
        hq = hg_ref[0, :, 0:HG_W]
        q = hq * _sigmoid(hq)
        rows = pl.ds(pl.multiple_of(i * tc, tc), tc)
        q_scr[rows, :] = q
        of_scr[rows, :] = _hgrn_direction(hg_ref, q, lb_ref[0:1, :], tri_ref[0], jv, bd_mask, st_scr, HG_W, False, tc)

    @pl.when(p == 1)
    def _():
        rows = pl.ds(pl.multiple_of((nc - 1 - i) * tc, tc), tc)
        o_b = _hgrn_direction(hg_ref, q_scr[rows, :], lb_ref[1:2, :], tri_ref[1], jv, bd_mask, st_scr, 2 * HG_W, True, tc)
        o = of_scr[rows, :] + o_b
        ms = _split_dot(o * o, jv) * (1.0 / HG_DV)
        g = hg_ref[0, :, 4 * HG_W:5 * HG_W]
        o_ref[0] = (o * lax.rsqrt(ms + RMS_EPS) * gn_ref[...] * (g * _sigmoid(g))).astype(o_ref.dtype)

    @pl.when(i == nc - 1)
    def _():
        sfin_ref[0, p] = st_scr[...]


def _hgrn(hg, lb2, gn, s0t, tc=256):
    bsz, t_len, _ = hg.shape
    nc = t_len // tc
    lower = np.arange(tc)[:, None] >= np.arange(tc)[None, :]
    tri = jnp.asarray(np.stack([lower, lower.T]).astype(np.float32))
    chunk = lambda b, p, i: (b, i + p * (nc - 1 - 2 * i), 0)
    o, sfin = pl.pallas_call(
        functools.partial(_hgrn_kernel, tc=tc, nc=nc),
        grid=(bsz, 2, nc),
        in_specs=[
            pl.BlockSpec((1, tc, HG_COLS), chunk),
            pl.BlockSpec((2, HG_W), lambda b, p, i: (0, 0)),
            pl.BlockSpec((1, HG_W), lambda b, p, i: (0, 0)),
            pl.BlockSpec((1, 2, HG_W, HG_W), lambda b, p, i: (b, 0, 0, 0)),
            pl.BlockSpec((2, tc, tc), lambda b, p, i: (0, 0, 0)),
            pl.BlockSpec((HG_W, HG_W), lambda b, p, i: (0, 0)),
        ],
        out_specs=[
            pl.BlockSpec((1, tc, HG_W), lambda b, p, i: (b, nc - 1 - i * p, 0)),
            pl.BlockSpec((1, 2, HG_W, HG_W), lambda b, p, i: (b, 0, 0, 0)),
        ],
        out_shape=[jax.ShapeDtypeStruct((bsz, t_len, HG_W), BF16),
                   jax.ShapeDtypeStruct((bsz, 2, HG_W, HG_W), F32)],
        scratch_shapes=[pltpu.VMEM((HG_W, HG_W), F32), pltpu.VMEM((t_len, HG_W), F32), pltpu.VMEM((t_len, HG_W), F32)],
        compiler_params=_cparams(("parallel", "arbitrary", "arbitrary")),
        name="hgrn",
    )(hg, lb2, jnp.tile(gn, HG_HEADS)[None], s0t, tri, _block_ones(HG_W, HG_DV))
    return o, sfin


def _state_to_blockdiag_t(s0):
    bsz = s0.shape[0]
    s0t = jnp.swapaxes(s0, -1, -2)
    eye = jnp.eye(HG_HEADS, dtype=s0.dtype)
    return jnp.einsum("bdhvk,hg->bdhvgk", s0t, eye).reshape(bsz, 2, HG_W, HG_W)


def _blockdiag_t_to_state(st):
    bsz = st.shape[0]
    s6 = st.reshape(bsz, 2, HG_HEADS, HG_DV, HG_HEADS, HG_DK)
    diag = jnp.stack([s6[:, :, h, :, h, :] for h in range(HG_HEADS)], axis=2)
    return jnp.swapaxes(diag, -1, -2)


def _post_kernel(x_ref, oa_ref, on_ref, od_ref, oh_ref, sh_ref, sc_ref, g1_ref, wg_ref, wb_ref, wo_ref,
                 lg_ref, lb_ref, o_ref, *, alpha):
    x = x_ref[0]
    h = (_ln_plain(x) * (1.0 + sc_ref[0]) + sh_ref[0]).astype(BF16)
    mixed = None
    for n, br_ref in enumerate((oa_ref, on_ref, od_ref, oh_ref)):
        gate = _sigmoid(_dot(h, wg_ref[:, n * D_MODEL:(n + 1) * D_MODEL]))
        term = gate * _dot(br_ref[0], wb_ref[n])
        mixed = term if mixed is None else mixed + term
    y = _dot(mixed.astype(BF16), wo_ref[...])
    o_ref[0] = _ln_plain(alpha * x + g1_ref[0] * y) * lg_ref[...] + lb_ref[...]


def _post_mixer(x, branches, mods, w_gates, w_branch, w_out, ln_g, ln_b, alpha, tm=256):
    bsz, t_len, d = x.shape
    tok = lambda n: pl.BlockSpec((1, tm, n), lambda b, i: (b, i, 0))
    const = lambda shape: pl.BlockSpec(shape, lambda b, i: (0,) * len(shape))
    mod = lambda k: _mod_spec(mods, k, 2)
    return pl.pallas_call(
        functools.partial(_post_kernel, alpha=alpha),
        grid=(bsz, t_len // tm),
        in_specs=[tok(d)] + [tok(BRANCH_W)] * 4 + [mod(MOD_SHIFT1), mod(MOD_SCALE1), mod(MOD_GATE1),
                  const((d, N_BRANCH * d)), const((N_BRANCH, BRANCH_W, d)), const((d, d)),
                  const((1, d)), const((1, d))],
        out_specs=tok(d),
        out_shape=jax.ShapeDtypeStruct((bsz, t_len, d), F32),
        compiler_params=_cparams(("parallel", "parallel")),
        name="post_mixer",
    )(x, *branches, mods, mods, mods, w_gates, w_branch, w_out, ln_g[None], ln_b[None])


ROUTER_LANES = 128
MOE_BT = 1024
MOE_ROWS = 128
MOE_STEPS = N_EXPERTS // 2
MOE_CHUNKS = 4


ROUTER_ROWS = 24


def _router_weights(logits):
    lt = jnp.transpose(logits)[0:ROUTER_ROWS, :]
    row = lax.broadcasted_iota(jnp.int32, lt.shape, 0)
    first_at = lambda mask: jnp.min(jnp.where(mask, row, ROUTER_LANES), axis=0, keepdims=True)
    is_g = row < N_GROUPS
    gl = jnp.where(is_g, lt, NEG_BIG)
    g_max = jnp.max(gl, axis=0, keepdims=True)
    g_idx = first_at(is_g & (gl == g_max))
    g_prob = 1.0 / jnp.sum(jnp.where(is_g, jnp.exp(gl - g_max), 0.0), axis=0, keepdims=True)
    lo = N_GROUPS + g_idx * EXPERTS_PER_GROUP
    in_grp = (row >= lo) & (row < lo + EXPERTS_PER_GROUP)
    el = jnp.where(in_grp, lt, NEG_BIG)
    v1 = jnp.max(el, axis=0, keepdims=True)
    i1 = first_at(in_grp & (el == v1))
    rest = in_grp & (row != i1)
    el2 = jnp.where(rest, lt, NEG_BIG)
    v2 = jnp.max(el2, axis=0, keepdims=True)
    i2 = first_at(rest & (el2 == v2))
    e2 = jnp.exp(v2 - v1)
    w1 = g_prob / (1.0 + e2)
    w2 = g_prob * e2 / (1.0 + e2)
    picked = jnp.where(row == g_idx, 1.0, jnp.where(row == i1, w1, jnp.where(row == i2, w2, 0.0)))
    pad = jnp.zeros((ROUTER_LANES - ROUTER_ROWS, lt.shape[1]), F32)
    return jnp.transpose(jnp.concatenate([picked, pad], axis=0))


def _moe_kernel(x_ref, sh_ref, sc_ref, g2_ref, wr_ref, br_ref, lt_ref, wg_ref, wu_ref, wd_ref, lg_ref, lb_ref, o_ref,
                xs_scr, pt_scr, cws_scr, ys_scr, seg_ref, *, alpha):
    s = pl.program_id(2)
    bt = xs_scr.shape[0]
    rows = MOE_ROWS

    @pl.when(s == 0)
    def _():
        x_parts, logit_parts = [], []
        for r in range(MOE_CHUNKS):
            sl = slice(r * bt // MOE_CHUNKS, (r + 1) * bt // MOE_CHUNKS)
            xm = _ln_plain(x_ref[0, sl, :]) * (1.0 + sc_ref[0]) + sh_ref[0]
            hi = xm.astype(BF16)
            lo = (xm - hi.astype(F32)).astype(BF16)
            x_parts.append(hi)
            logit_parts.append(_dot(hi, wr_ref[0]) + _dot(lo, wr_ref[0]) + _dot(hi, wr_ref[1]) + br_ref[...])
        x_hi = jnp.concatenate(x_parts, axis=0)
        logits = jnp.concatenate(logit_parts, axis=0)
        routed = _router_weights(logits)
        lane = lax.broadcasted_iota(jnp.int32, (bt, ROUTER_LANES), 1)
        onehot = jnp.where(lane < N_GROUPS, routed, 0.0)
        in_g = onehot > 0.0
        cw = routed - onehot
        before = _dot(lt_ref[...], onehot.astype(BF16))
        count = jnp.sum(onehot, axis=0, keepdims=True)
        lane1 = lax.broadcasted_iota(jnp.int32, (1, ROUTER_LANES), 1)
        start = jnp.zeros((1, ROUTER_LANES), F32)
        for k in range(N_GROUPS - 1):
            start = start + jnp.where(lane1 > k, count[:, k:k + 1], 0.0)
        dest = jnp.sum(jnp.where(in_g, start + before, 0.0), axis=-1, keepdims=True)
        dest_row = jnp.transpose(jnp.broadcast_to(dest, (bt, ROUTER_LANES)))[0:1, :]
        pt_scr[...] = jnp.where(dest.astype(jnp.int32) == lax.broadcasted_iota(jnp.int32, (bt, bt), 1),
                                1.0, 0.0).astype(BF16)
        perm = jnp.where(dest_row.astype(jnp.int32) == lax.broadcasted_iota(jnp.int32, (bt, bt), 0),
                         1.0, 0.0).astype(BF16)
        xs_scr[...] = _dot(perm, x_hi).astype(BF16)
        c_hi = cw.astype(BF16)
        c_lo = (cw - c_hi.astype(F32)).astype(BF16)
        cws_scr[...] = _dot(perm, c_hi) + _dot(perm, c_lo)
        ys_scr[...] = jnp.zeros_like(ys_scr)
        for g in range(1, N_GROUPS):
            seg_ref[g] = jnp.sum(jnp.where(lane1 == g, start, 0.0)).astype(jnp.int32)

    g = s // (MOE_STEPS // N_GROUPS)
    seg_lo = jnp.where(g == 0, 0, seg_ref[jnp.maximum(g, 1)])
    seg_hi = jnp.where(g == N_GROUPS - 1, bt, seg_ref[jnp.minimum(g + 1, N_GROUPS - 1)])
    t_lo = seg_lo // rows
    t_hi = jnp.where(seg_hi > seg_lo, (seg_hi + rows - 1) // rows, t_lo)
    lane_r = lax.broadcasted_iota(jnp.int32, (rows, ROUTER_LANES), 1)

    def gate_up(t):
        xt = xs_scr[pl.ds(pl.multiple_of(t * rows, rows), rows), :]
        return [(_dot(xt, wg_ref[e]), _dot(xt, wu_ref[e])) for e in range(2)]

    def down(t, gu):
        r0 = pl.multiple_of(t * rows, rows)
        cwt = cws_scr[pl.ds(r0, rows), :]
        hids = []
        for e, (gate, up) in enumerate(gu):
            w_e = jnp.sum(jnp.where(lane_r == N_GROUPS + 2 * s + e, cwt, 0.0), axis=-1, keepdims=True)
            hids.append((gate * _sigmoid(gate) * up * w_e).astype(BF16))
        ys_scr[pl.ds(r0, rows), :] += _dot(hids[0], wd_ref[0]) + _dot(hids[1], wd_ref[1])

    def tile_pair(i, carry):
        t = t_lo + 2 * i
        gu_a, gu_b = gate_up(t), gate_up(t + 1)
        down(t, gu_a)
        down(t + 1, gu_b)
        return carry

    n_tiles = t_hi - t_lo
    lax.fori_loop(0, n_tiles // 2, tile_pair, 0)

    @pl.when(n_tiles % 2 == 1)
    def _():
        down(t_hi - 1, gate_up(t_hi - 1))

    @pl.when(s == MOE_STEPS - 1)
    def _():
        ys = ys_scr[...]
        hi = ys.astype(BF16)
        lo = (ys - hi.astype(F32)).astype(BF16)
        rows_of = lambda r: slice(r * bt // MOE_CHUNKS, (r + 1) * bt // MOE_CHUNKS)
        back = lambda r: _dot(pt_scr[rows_of(r), :], hi) + _dot(pt_scr[rows_of(r), :], lo)
        y_next = back(0)
        for r in range(MOE_CHUNKS):
            y, y_next = y_next, (back(r + 1) if r + 1 < MOE_CHUNKS else None)
            sl = rows_of(r)
            o_ref[0, sl, :] = _ln_plain(alpha * x_ref[0, sl, :] + g2_ref[0] * y) * lg_ref[...] + lb_ref[...]


def _moe(x, mods, w_route, b_route, w_gate, w_up, w_down, ln_g, ln_b, alpha):
    bsz, t_len, d = x.shape
    bt = min(MOE_BT, t_len)
    assert t_len % bt == 0 and bt % MOE_ROWS == 0 and bt % (8 * MOE_CHUNKS) == 0
    tok = pl.BlockSpec((1, bt, d), lambda b, i, e: (b, i, 0))
    const = lambda shape: pl.BlockSpec(shape, lambda b, i, e: (0,) * len(shape))
    mod = lambda k: _mod_spec(mods, k, 3)
    pair = lambda shape: pl.BlockSpec((2,) + shape, lambda b, i, e: (e, 0, 0))
    lower = jnp.asarray(np.tril(np.ones((bt, bt), np.float32), -1), dtype=BF16)
    return pl.pallas_call(
        functools.partial(_moe_kernel, alpha=alpha),
        grid=(bsz, t_len // bt, MOE_STEPS),
        in_specs=[tok, mod(MOD_SHIFT2), mod(MOD_SCALE2), mod(MOD_GATE2), const((2, d, ROUTER_LANES)), const((1, ROUTER_LANES)), const((bt, bt)),
                  pair((d, D_EXPERT)), pair((d, D_EXPERT)), pair((D_EXPERT, d)), const((1, d)), const((1, d))],
        out_specs=tok,
        out_shape=jax.ShapeDtypeStruct((bsz, t_len, d), F32),
        scratch_shapes=[pltpu.VMEM((bt, d), BF16), pltpu.VMEM((bt, bt), BF16), pltpu.VMEM((bt, ROUTER_LANES), F32),
                        pltpu.VMEM((bt, d), F32), pltpu.SMEM((N_GROUPS,), jnp.int32)],
        compiler_params=_cparams(("parallel", "parallel", "arbitrary")),
        name="moe",
    )(x, mods, mods, mods, w_route, b_route, lower, w_gate, w_up, w_down, ln_g[None], ln_b[None])


def _layer(x, mods, lp, ctx, alpha):
    bsz, t_len, _ = x.shape
    latent = ctx is not None
    cached = (ctx["gk_t"], ctx["gv_ext"], ctx["dk_t"], ctx["dv"]) if latent else ()
    outs = _inproj(x, mods, lp["w_heads"], lp["gqa_qn"], lp["gqa_kn"], latent, cached)
    qa, kat, va, qn, kn, vn, qd, kdt, vd, hg = outs[:10]
    if latent:
        s0t = _state_to_blockdiag_t(ctx["state"].astype(F32))
        tq = 256
    else:
        s0t = jnp.zeros((bsz, 2, HG_W, HG_W), F32)
        tq = t_len
    o_a = _attention(qa, kat, va, GQA_Q_HEADS, GQA_KV_HEADS, tq)
    if latent:
        o_n = _na_attention(qn, kn, vn, ctx["nk_ext"], ctx["nv_ext"], lp["na_bias"])
    else:
        o_n = _attention(qn, kn, vn, NA_HEADS, NA_HEADS, tq)
    o_d = _diff_attention(qd, kdt, vd, lp["diff_par"], tq)
    o_h, sfin = _hgrn(hg, lp["hg_lb"], lp["hg_norm"], s0t)
    x1 = _post_mixer(x, (o_a, o_n, o_d, o_h), mods, lp["w_gates"], lp["w_branch"], lp["w_out"],
                     lp["ln_mix_g"], lp["ln_mix_b"], alpha)
    x1r = x1 if latent else x1.reshape(1, bsz * t_len, D_MODEL)
    y = _moe(x1r, mods, lp["w_route"], lp["b_route"], lp["w_gate"], lp["w_up"], lp["w_down"],
             lp["ln_ffn_g"], lp["ln_ffn_b"], alpha).reshape(bsz, t_len, D_MODEL)
    caches = None
    if not latent:
        ka, va32, kn32, vn32, kd32, vd32 = outs[10:]
        caches = (ka.reshape(bsz, t_len, GQA_KV_HEADS, HEAD_DIM), va32.reshape(bsz, t_len, GQA_KV_HEADS, HEAD_DIM),
                  kn32.reshape(bsz, t_len, NA_HEADS, HEAD_DIM), vn32.reshape(bsz, t_len, NA_HEADS, HEAD_DIM),
                  kd32.reshape(bsz, t_len, DIFF_HEADS, 2, DIFF_QK_DIM), vd32.reshape(bsz, t_len, DIFF_HEADS, DIFF_V_DIM),
                  _blockdiag_t_to_state(sfin))
    return y, caches


def _padded_heads(a, fill):
    a = a.astype(BF16)
    return jnp.concatenate([a, jnp.full(a.shape, fill, BF16)], axis=-1).reshape(a.shape[:3] + (-1,))


def _keys_t(a):
    a = a.astype(BF16).reshape(a.shape[:3] + (-1,))
    return jnp.swapaxes(a, 2, 3)


def kernel(x_prompt, x_sample, cache_gqa_k, cache_gqa_v, cache_na_k, cache_na_v, cache_diff_k, cache_diff_v, state_hgrn, c, c_ctx, w_ada, b_ada, w_in, gqa_q_norm, gqa_k_norm, na_rpb, diff_lambda, diff_subln, hgrn_lb, hgrn_norm, w_branch, w_out, ln_mix_g, ln_mix_b, ln_ffn_g, ln_ffn_b, w_group, b_group, w_router, b_router, w_gate, w_up, w_down):
    depth = w_in.shape[0]
    d = D_MODEL
    alpha = (2 * depth) ** 0.25
    dec_b = c.shape[0]

    lb = jax.nn.softmax(hgrn_lb.astype(F32), axis=1)
    lb = jnp.cumsum(lb, axis=1) - lb[:, :1]

    cond_rows = -(-(dec_b + 1) // 8) * 8
    cond = jnp.zeros((cond_rows, d), F32).at[:dec_b].set(c).at[dec_b].set(c_ctx)
    mods_all = _ada_mods(cond, w_ada, b_ada)

    na_bias = _na_bias_table(na_rpb)
    layers = []
    for l in range(depth):
        lam_init = 0.8 - 0.6 * math.exp(-0.3 * l)
        lp_lam = diff_lambda[l].astype(F32)
        lam = jnp.exp(jnp.sum(lp_lam[0] * lp_lam[1])) - jnp.exp(jnp.sum(lp_lam[2] * lp_lam[3])) + lam_init
        diff_par = jnp.zeros((8, DIFF_V_DIM), F32).at[0].set(lam).at[1].set(diff_subln[l] * (1.0 - lam_init))
        w_route = jnp.zeros((d, ROUTER_LANES), F32)
        w_route = w_route.at[:, :N_GROUPS].set(w_group[l]).at[:, N_GROUPS:N_GROUPS + N_EXPERTS].set(
            w_router[l].reshape(d, N_EXPERTS))
        w_route_hi = w_route.astype(BF16)
        b_route = jnp.zeros((1, ROUTER_LANES), F32)
        b_route = b_route.at[0, :N_GROUPS].set(b_group[l]).at[0, N_GROUPS:N_GROUPS + N_EXPERTS].set(
            b_router[l].reshape(N_EXPERTS))
        layers.append({
            "w_heads": w_in[l, :, :D_HEADS_IN].astype(BF16),
            "w_gates": w_in[l, :, D_HEADS_IN:].astype(BF16),
            "gqa_qn": gqa_q_norm[l], "gqa_kn": gqa_k_norm[l],
            "na_bias": na_bias[l],
            "diff_par": diff_par,
            "hg_lb": lb[:, l], "hg_norm": hgrn_norm[l],
            "w_branch": w_branch[l].astype(BF16), "w_out": w_out[l].astype(BF16),
            "ln_mix_g": ln_mix_g[l], "ln_mix_b": ln_mix_b[l], "ln_ffn_g": ln_ffn_g[l], "ln_ffn_b": ln_ffn_b[l],
            "w_route": jnp.stack([w_route_hi, (w_route - w_route_hi.astype(F32)).astype(BF16)]), "b_route": b_route,
            "w_gate": w_gate[l].reshape(N_EXPERTS, d, D_EXPERT).astype(BF16),
            "w_up": w_up[l].reshape(N_EXPERTS, d, D_EXPERT).astype(BF16),
            "w_down": w_down[l].reshape(N_EXPERTS, D_EXPERT, d).astype(BF16),
        })

    xp = x_prompt
    ctx_out = []
    for l in range(depth):
        xp, caches = _layer(xp, mods_all[l, dec_b:dec_b + 1, None, :], layers[l], None, alpha)
        ctx_out.append(caches)
    stack = lambda i: jnp.stack([t[i] for t in ctx_out], axis=1)

    cached = {"gk_t": _keys_t(cache_gqa_k), "gv_ext": _padded_heads(cache_gqa_v, 1.0),
              "nk_ext": _padded_heads(cache_na_k, 0.0), "nv_ext": _padded_heads(cache_na_v, 1.0),
              "dk_t": _keys_t(cache_diff_k), "dv": _padded_heads(cache_diff_v, 1.0), "state": state_hgrn}
    xs = x_sample
    for l in range(depth):
        ctx = {name: t[:, l] for name, t in cached.items()}
        xs, _ = _layer(xs, mods_all[l, :dec_b, None, :], layers[l], ctx, alpha)

    return (xp, xs, stack(0), stack(1), stack(2), stack(3), stack(4), stack(5), stack(6))
```

```python
import functools
import math

import jax
import jax.numpy as jnp
import numpy as np
from jax import lax
from jax.experimental import pallas as pl
from jax.experimental.pallas import tpu as pltpu

F32 = jnp.float32
BF16 = jnp.bfloat16

D_MODEL = 1024
GRID_W = 64
HEAD_DIM = 64
GQA_Q_HEADS = 4
GQA_KV_HEADS = 2
NA_HEADS = 4
NA_KH = 8
NA_KW = 16
DIFF_HEADS = 4
DIFF_QK_DIM = 32
DIFF_V_DIM = 64
HG_HEADS = 4
HG_DK = 64
HG_DV = 64
BRANCH_W = 256
N_BRANCH = 4
N_GROUPS = 4
EXPERTS_PER_GROUP = 4
N_EXPERTS = N_GROUPS * EXPERTS_PER_GROUP
D_EXPERT = 512
ROPE_THETA = 10000.0
LN_EPS = 1e-5
RMS_EPS = 1e-6
F_FLOOR = 1e-30
NEG_BIG = -1e30
LOG2E = math.log2(math.e)
MOD_SHIFT1, MOD_SCALE1, MOD_GATE1, MOD_SHIFT2, MOD_SCALE2, MOD_GATE2 = range(6)

C_GQ, C_GK, C_GV = 0, 256, 384
C_NQ, C_NK, C_NV = 512, 768, 1024
C_DQ, C_DK, C_DV = 1280, 1536, 1792
C_HG = 2048
D_HEADS_IN = 3328
HG_COLS = 1280

VMEM_LIMIT = 56 * 1024 * 1024


def _cparams(sem):
    return pltpu.CompilerParams(dimension_semantics=sem, vmem_limit_bytes=VMEM_LIMIT)


def _dot(a, b):
    return jnp.dot(a, b, preferred_element_type=F32)


def _dot_nt(a, b):
    return lax.dot_general(a, b, (((1,), (1,)), ((), ())), preferred_element_type=F32)


def _split_dot(x, j):
    hi = x.astype(BF16)
    lo = (x - hi.astype(F32)).astype(BF16)
    return _dot(hi, j) + _dot(lo, j)


def _sigmoid(z):
    return 1.0 / (1.0 + jnp.exp(-z))


def _ln_plain(x):
    mu = jnp.mean(x, axis=-1, keepdims=True)
    xc = x - mu
    var = jnp.mean(xc * xc, axis=-1, keepdims=True)
    return xc * lax.rsqrt(var + LN_EPS)


def _mod_spec(mods, k, rank):
    per_batch = mods.shape[0] > 1
    d = mods.shape[-1] // 6
    if rank == 2:
        return pl.BlockSpec((1, 1, d), lambda b, i: (b if per_batch else 0, 0, k))
    return pl.BlockSpec((1, 1, d), lambda b, i, e: (b if per_batch else 0, 0, k))


def _block_ones(n, blk):
    idx = np.arange(n) // blk
    return jnp.asarray((idx[:, None] == idx[None, :]).astype(np.float32), dtype=BF16)


def _ada_kernel(c_ref, w_ref, b_ref, o_ref):
    c = c_ref[...]
    s = (c * _sigmoid(c)).astype(BF16)
    o_ref[0] = _dot(s, w_ref[0].astype(BF16)) + b_ref[0]


def _ada_mods(cond, w_ada, b_ada):
    depth, d, n = w_ada.shape
    rows = cond.shape[0]
    tn = 1536
    return pl.pallas_call(
        _ada_kernel,
        grid=(depth, n // tn),
        in_specs=[
            pl.BlockSpec((rows, d), lambda l, j: (0, 0)),
            pl.BlockSpec((1, d, tn), lambda l, j: (l, 0, j)),
            pl.BlockSpec((1, 1, tn), lambda l, j: (l, 0, j)),
        ],
        out_specs=pl.BlockSpec((1, rows, tn), lambda l, j: (l, 0, j)),
        out_shape=jax.ShapeDtypeStruct((depth, rows, n), F32),
        compiler_params=_cparams(("parallel", "parallel")),
        name="ada_mods",
    )(cond, w_ada, b_ada.reshape(depth, 1, n))


def _rope_tables(t_len, d, reps):
    quarter = d // 4
    t = np.arange(t_len)
    inv = ROPE_THETA ** (-np.arange(quarter, dtype=np.float64) / quarter)
    ang_r = (t // GRID_W).astype(np.float64)[:, None] * inv
    ang_c = (t % GRID_W).astype(np.float64)[:, None] * inv
    ang = np.concatenate([ang_r, ang_r, ang_c, ang_c], axis=-1)
    cos, sin = np.cos(ang), np.sin(ang)
    first = (np.arange(d) % (2 * quarter)) < quarter
    sin_a = np.where(first, -sin, 0.0)
    sin_b = np.where(first, 0.0, sin)
    tile = lambda a: jnp.asarray(np.tile(a, (1, reps)), dtype=F32)
    return tile(cos), tile(sin_a), tile(sin_b)


def _rope(x, cos, sin_a, sin_b, quarter):
    w = x.shape[-1]
    return x * cos + pltpu.roll(x, w - quarter, 1) * sin_a + pltpu.roll(x, quarter, 1) * sin_b


def _seg_rms(x, j):
    ss = _split_dot(x * x, j)
    return x * lax.rsqrt(ss * (1.0 / HEAD_DIM) + RMS_EPS)


def _pad_heads(x, heads, fill):
    pad = jnp.full((x.shape[0], HEAD_DIM), fill, x.dtype)
    pieces = []
    for h in range(heads):
        pieces += [x[:, h * HEAD_DIM:(h + 1) * HEAD_DIM], pad]
    return jnp.concatenate(pieces, axis=-1)


def _inproj_kernel(*refs, latent, skip):
    n_in = (17 if skip else 13) if latent else 7
    ins, outs = refs[:n_in], refs[n_in:]
    x_ref, sh_ref, sc_ref, w_ref, qn_ref, kn_ref, j_ref = ins[:7]
    qa_ref, kat_ref, va_ref, qnb_ref, knb_ref, vnb_ref, qd_ref, kdt_ref, vd_ref, hg_ref = outs[:10]

    def project():
        h = (_ln_plain(x_ref[0]) * (1.0 + sc_ref[0]) + sh_ref[0]).astype(BF16)
        proj = lambda c0, n: _dot(h, w_ref[:, c0:c0 + n])

        gq = _seg_rms(proj(C_GQ, 256), j_ref[...]) * qn_ref[...]
        gk = _seg_rms(proj(C_GK, 128), j_ref[0:128, 0:128]) * kn_ref[...]
        gv = proj(C_GV, 128)
        nq = proj(C_NQ, 256) * (LOG2E * HEAD_DIM ** -0.5)
        nk = proj(C_NK, 256)
        nv = proj(C_NV, 256)
        dq = proj(C_DQ, 256)
        dk = proj(C_DK, 256)
        dv = proj(C_DV, 256)
        if latent:
            c64, a64, b64, c32, a32, b32 = (r[...] for r in ins[7:13])
            gq = _rope(gq, c64, a64, b64, HEAD_DIM // 4)
            gk = _rope(gk, c64[:, 0:128], a64[:, 0:128], b64[:, 0:128], HEAD_DIM // 4)
            dq = _rope(dq, c32, a32, b32, DIFF_QK_DIM // 4)
            dk = _rope(dk, c32, a32, b32, DIFF_QK_DIM // 4)
        else:
            for ref, val in zip(outs[10:], (gk, gv, nk, nv, dk, dv)):
                ref[0] = val
        qa_ref[0] = (gq * (LOG2E * HEAD_DIM ** -0.5)).astype(BF16)
        kat_ref[0] = jnp.transpose(gk).astype(BF16)
        va_ref[0] = _pad_heads(gv, GQA_KV_HEADS, 1.0).astype(BF16)
        if latent:
            qnb_ref[0] = _pad_heads(nq, NA_HEADS, 0.0).astype(BF16)
            knb_ref[0] = _pad_heads(nk, NA_HEADS, 0.0).astype(BF16)
        else:
            qnb_ref[0] = nq.astype(BF16)
            knb_ref[0] = jnp.transpose(nk).astype(BF16)
        vnb_ref[0] = _pad_heads(nv, NA_HEADS, 1.0).astype(BF16)
        qd_ref[0] = (dq * (LOG2E * DIFF_QK_DIM ** -0.5)).astype(BF16)
        kdt_ref[0] = jnp.transpose(dk).astype(BF16)
        vd_ref[0] = _pad_heads(dv, DIFF_HEADS, 1.0).astype(BF16)
        for s in range(HG_COLS // 256):
            hg_ref[0, :, s * 256:(s + 1) * 256] = proj(C_HG + s * 256, 256)

    if not skip:
        project()
        return
    step = pl.program_id(1)

    @pl.when(step < skip)
    def _():
        for dst, src in zip((kat_ref, va_ref, kdt_ref, vd_ref), ins[13:17]):
            dst[...] = src[...]

    pl.when(step >= skip)(project)


def _inproj(x, mods, w_heads, qn, kn, latent, cached=(), tm=256):
    bsz, t_len, d = x.shape
    past = cached[0].shape[-1] if cached else 0
    assert past % tm == 0
    skip = past // tm
    tile = lambda i: jnp.maximum(i - skip, 0)
    tok = lambda n: pl.BlockSpec((1, tm, n), lambda b, i: (b, tile(i), 0))
    tok_t = lambda n: pl.BlockSpec((1, n, tm), lambda b, i: (b, 0, tile(i)))
    const = lambda shape: pl.BlockSpec(shape, lambda b, i: (0,) * len(shape))
    in_specs = [tok(d), _mod_spec(mods, MOD_SHIFT1, 2), _mod_spec(mods, MOD_SCALE1, 2),
                const((d, D_HEADS_IN)), const((1, 256)), const((1, 128)), const((256, 256))]
    args = [x, mods, mods, w_heads, jnp.tile(qn, GQA_Q_HEADS)[None], jnp.tile(kn, GQA_KV_HEADS)[None],
            _block_ones(256, HEAD_DIM)]
    if latent:
        in_specs += [pl.BlockSpec((tm, 256), lambda b, i: (tile(i), 0))] * 6
        args += list(_rope_tables(t_len, HEAD_DIM, 4)) + list(_rope_tables(t_len, DIFF_QK_DIM, 8))
    if skip:
        head = lambda i: jnp.minimum(i, skip - 1)
        for c in cached:
            if c.shape[-1] == past:
                in_specs.append(pl.BlockSpec((1, c.shape[1], tm), lambda b, i: (b, 0, head(i))))
            else:
                in_specs.append(pl.BlockSpec((1, tm, c.shape[2]), lambda b, i: (b, head(i), 0)))
        args += list(cached)
    rows = lambda n, dt=BF16: (jax.ShapeDtypeStruct((bsz, t_len, n), dt), tok(n))
    cols = lambda n: (jax.ShapeDtypeStruct((bsz, n, t_len), BF16), tok_t(n))
    rows_kv = lambda n: (jax.ShapeDtypeStruct((bsz, past + t_len, n), BF16),
                         pl.BlockSpec((1, tm, n), lambda b, i: (b, i, 0)))
    cols_kv = lambda n: (jax.ShapeDtypeStruct((bsz, n, past + t_len), BF16),
                         pl.BlockSpec((1, n, tm), lambda b, i: (b, 0, i)))
    outs = [rows(256), cols_kv(128), rows_kv(256),
            rows(512) if latent else rows(256), rows(512) if latent else cols(256), rows(512),
            rows(256), cols_kv(256), rows_kv(512), rows(HG_COLS, F32)]
    if not latent:
        outs += [rows(n, F32) for n in (128, 128, 256, 256, 256, 256)]
    return pl.pallas_call(
        functools.partial(_inproj_kernel, latent=latent, skip=skip),
        grid=(bsz, skip + t_len // tm),
        in_specs=in_specs,
        out_specs=[o[1] for o in outs],
        out_shape=[o[0] for o in outs],
        compiler_params=_cparams(("parallel", "arbitrary" if skip else "parallel")),
        name="in_proj_rope" if latent else "in_proj",
    )(*args)


def _softmax_p(s):
    return jnp.exp2(s - jnp.max(s, axis=-1, keepdims=True)).astype(BF16)


def _pv_norm(p, v_ext):
    dv = v_ext.shape[-1] // 2
    o = _dot(p, v_ext)
    return o[:, :dv] * (1.0 / o[:, dv:])


def _attn_kernel(q_ref, kt_ref, v_ref, o_ref, *, hq, group, d, dv):
    q = q_ref[0]
    tq = q.shape[0]
    n_groups = hq // group

    def scores(g):
        qs = jnp.concatenate([q[:, h * d:(h + 1) * d] for h in range(g * group, (g + 1) * group)], axis=0)
        return _dot(qs, kt_ref[0, g * d:(g + 1) * d, :])

    outs = []
    s_next = scores(0)
    for g in range(n_groups):
        s_cur, s_next = s_next, (scores(g + 1) if g + 1 < n_groups else None)
        o = _pv_norm(_softmax_p(s_cur), v_ref[0, :, g * 2 * dv:(g + 1) * 2 * dv])
        outs += [o[j * tq:(j + 1) * tq] for j in range(group)]
    o_ref[0] = jnp.concatenate(outs, axis=-1).astype(o_ref.dtype)


def _attention(q, kt, v_ext, hq, hkv, tq):
    bsz, t_len, qw = q.shape
    l_len = kt.shape[-1]
    d = kt.shape[1] // hkv
    dv = v_ext.shape[-1] // (2 * hkv)
    return pl.pallas_call(
        functools.partial(_attn_kernel, hq=hq, group=hq // hkv, d=d, dv=dv),
        grid=(bsz, t_len // tq),
        in_specs=[
            pl.BlockSpec((1, tq, qw), lambda b, i: (b, i, 0)),
            pl.BlockSpec((1, hkv * d, l_len), lambda b, i: (b, 0, 0)),
            pl.BlockSpec((1, l_len, hkv * 2 * dv), lambda b, i: (b, 0, 0)),
        ],
        out_specs=pl.BlockSpec((1, tq, hq * dv), lambda b, i: (b, i, 0)),
        out_shape=jax.ShapeDtypeStruct((bsz, t_len, hq * dv), BF16),
        compiler_params=_cparams(("parallel", "parallel")),
        name="attention",
    )(q, kt, v_ext)


def _diff_attn_kernel(q_ref, kt_ref, v_ref, par_ref, o_ref):
    q = q_ref[0]
    tq = q.shape[0]
    lam = par_ref[0:1, :]
    gain = par_ref[1:2, :]
    dq = DIFF_QK_DIM
    scores = lambda i: _dot(q[:, i * dq:(i + 1) * dq], kt_ref[0, i * dq:(i + 1) * dq, :])
    outs = []
    s_next = (scores(0), scores(1))
    for h in range(DIFF_HEADS):
        s_cur, s_next = s_next, ((scores(2 * h + 2), scores(2 * h + 3)) if h + 1 < DIFF_HEADS else None)
        o12 = _pv_norm(jnp.concatenate([_softmax_p(s_cur[0]), _softmax_p(s_cur[1])], axis=0),
                       v_ref[0, :, h * 2 * DIFF_V_DIM:(h + 1) * 2 * DIFF_V_DIM])
        o = o12[:tq] - lam * o12[tq:]
        ms = jnp.mean(o * o, axis=-1, keepdims=True)
        outs.append(o * lax.rsqrt(ms + RMS_EPS) * gain)
    o_ref[0] = jnp.concatenate(outs, axis=-1).astype(o_ref.dtype)


def _diff_attention(q, kt, v_ext, par, tq):
    bsz, t_len, qw = q.shape
    l_len = kt.shape[-1]
    return pl.pallas_call(
        _diff_attn_kernel,
        grid=(bsz, t_len // tq),
        in_specs=[
            pl.BlockSpec((1, tq, qw), lambda b, i: (b, i, 0)),
            pl.BlockSpec((1, 2 * DIFF_HEADS * DIFF_QK_DIM, l_len), lambda b, i: (b, 0, 0)),
            pl.BlockSpec((1, l_len, DIFF_HEADS * 2 * DIFF_V_DIM), lambda b, i: (b, 0, 0)),
            pl.BlockSpec((8, DIFF_V_DIM), lambda b, i: (0, 0)),
        ],
        out_specs=pl.BlockSpec((1, tq, DIFF_HEADS * DIFF_V_DIM), lambda b, i: (b, i, 0)),
        out_shape=jax.ShapeDtypeStruct((bsz, t_len, DIFF_HEADS * DIFF_V_DIM), BF16),
        compiler_params=_cparams(("parallel", "parallel")),
        name="diff_attention",
    )(q, kt, v_ext, par)


def _na_bias_table(rpb):
    shift = np.arange(NA_KH)[:, None]
    win_row = np.arange(NA_KH)[None, :]
    row_off = win_row - shift + (NA_KH - 1)
    col = np.arange(GRID_W)
    col_start = np.clip(col - NA_KW // 2, 0, GRID_W - NA_KW)
    key_col = np.arange(GRID_W)[None, :]
    valid = (key_col >= col_start[:, None]) & (key_col < col_start[:, None] + NA_KW)
    col_off = key_col - col[:, None] + (NA_KW - 1)
    row_sel = (row_off[:, :, None] == np.arange(2 * NA_KH - 1)).astype(np.float32)
    col_sel = ((col_off[:, :, None] == np.arange(2 * NA_KW - 1)) & valid[:, :, None]).astype(np.float32)
    tbl = jnp.einsum("sia,lhab,cxb->lshcix", row_sel, rpb.astype(F32) * LOG2E, col_sel,
                     precision=lax.Precision.HIGHEST)
    tbl = jnp.where(valid[None, None, None, :, None, :], tbl, NEG_BIG)
    return tbl.reshape(rpb.shape[0], NA_KH, rpb.shape[1], GRID_W, NA_KH * GRID_W)


def _na_kernel(q_ref, k_ref, v_ref, kc_ref, vc_ref, bias_ref, o_ref, *, rows_per_step, rows):
    i = pl.program_id(1)
    n_win = NA_KH * GRID_W
    units = [(rr, h) for rr in range(rows_per_step) for h in range(NA_HEADS)]

    def window(rr):
        r = i * rows_per_step + rr
        r_start = jnp.clip(r - NA_KH // 2, 0, rows - NA_KH)
        return r - r_start, pl.multiple_of(r_start * GRID_W, GRID_W)

    def scores(unit):
        rr, h = unit
        shift, start = window(rr)
        lanes = slice(h * 2 * HEAD_DIM, (h + 1) * 2 * HEAD_DIM)
        qh = q_ref[0, rr * GRID_W:(rr + 1) * GRID_W, lanes]
        s_loc = _dot_nt(qh, k_ref[0, pl.ds(start, n_win), lanes]) + bias_ref[shift, h]
        return s_loc, _dot_nt(qh, kc_ref[0, :, lanes])

    def output(unit, s_loc, s_ctx):
        rr, h = unit
        _, start = window(rr)
        m = jnp.maximum(jnp.max(s_loc, axis=-1, keepdims=True), jnp.max(s_ctx, axis=-1, keepdims=True))
        p_loc = jnp.exp2(s_loc - m).astype(BF16)
        p_ctx = jnp.exp2(s_ctx - m).astype(BF16)
        lanes = slice(h * 2 * HEAD_DIM, (h + 1) * 2 * HEAD_DIM)
        o = _dot(p_loc, v_ref[0, pl.ds(start, n_win), lanes]) + _dot(p_ctx, vc_ref[0, :, lanes])
        return o[:, :HEAD_DIM] * (1.0 / o[:, HEAD_DIM:])

    depth = 4
    staged = [scores(u) for u in units[:depth]]
    outs = []
    for n, unit in enumerate(units):
        cur = staged.pop(0)
        if n + depth < len(units):
            staged.append(scores(units[n + depth]))
        outs.append(output(unit, *cur))
        if unit[1] == NA_HEADS - 1:
            rr = unit[0]
            o_ref[0, rr * GRID_W:(rr + 1) * GRID_W, :] = jnp.concatenate(outs, axis=-1).astype(o_ref.dtype)
            outs = []


def _na_attention(q, k, v_ext, kc, vc_ext, bias, rows_per_step=8):
    bsz, n, qw = q.shape
    rows = n // GRID_W
    assert rows >= NA_KH and rows % rows_per_step == 0
    l_ctx = kc.shape[1]
    tq = rows_per_step * GRID_W
    full = lambda length: pl.BlockSpec((1, length, qw), lambda b, i: (b, 0, 0))
    return pl.pallas_call(
        functools.partial(_na_kernel, rows_per_step=rows_per_step, rows=rows),
        grid=(bsz, rows // rows_per_step),
        in_specs=[
            pl.BlockSpec((1, tq, qw), lambda b, i: (b, i, 0)),
            full(n), full(n), full(l_ctx), full(l_ctx),
            pl.BlockSpec(bias.shape, lambda b, i: (0, 0, 0, 0)),
        ],
        out_specs=pl.BlockSpec((1, tq, NA_HEADS * HEAD_DIM), lambda b, i: (b, i, 0)),
        out_shape=jax.ShapeDtypeStruct((bsz, n, NA_HEADS * HEAD_DIM), BF16),
        compiler_params=_cparams(("parallel", "parallel")),
        name="na_attention",
    )(q, k, v_ext, kc, vc_ext, bias)


HG_SUB = 16
HG_W = HG_HEADS * HG_DK


def _hgrn_direction(hg_ref, q, lb, tri, jv, bd_mask, st_ref, z_col, reverse, tc):
    c = HG_SUB
    n_sub = tc // c
    z = hg_ref[0, :, z_col:z_col + HG_W]
    v = hg_ref[0, :, 3 * HG_W:4 * HG_W]
    sig = _sigmoid(z)
    f = lb + (1.0 - lb) * sig
    log_f = jnp.log(jnp.maximum(f, F_FLOOR))
    kk = (1.0 - lb) * (1.0 - sig)
    b = jnp.dot(tri, log_f * LOG2E, preferred_element_type=F32, precision=lax.Precision.HIGHEST)
    st = st_ref[...]
    o_inter = _dot_nt((q * jnp.exp2(b)).astype(BF16), st.astype(BF16))
    b_end = b[0:1] if reverse else b[tc - 1:tc]
    upd = _dot(jnp.transpose(v).astype(BF16), (kk * jnp.exp2(b_end - b)).astype(BF16))
    st_ref[...] = jnp.exp2(b_end) * st + jnp.where(bd_mask, upd, 0.0)

    half = c // 2
    row = lax.broadcasted_iota(jnp.int32, (half, HG_W), 0)
    lane_head = lax.broadcasted_iota(jnp.int32, (c, HG_W), 1) // HG_DK
    v_bf = v.astype(BF16)
    zeros = jnp.zeros((half, HG_W), F32)
    def reach(s, top):
        if reverse:
            return ("all" if s >= half else "tri") if top else ("none" if s < half else "tri")
        return ("none" if s >= half else "tri") if top else ("all" if s < half else "tri")

    def score_stage(j):
        sl = slice(j * c, (j + 1) * c)
        bj, qj, kj = b[sl], q[sl], kk[sl]
        cj = bj - jnp.log2(kj)
        prods = []
        for s in range(c):
            for top in (True, False):
                kind = reach(s, top)
                if kind == "none":
                    prods.append(zeros)
                    continue
                hs = slice(0, half) if top else slice(half, c)
                decay_k = jnp.exp2(bj[hs] - cj[s:s + 1])
                if kind == "tri":
                    s_in = s % half
                    decay_k = jnp.where((row <= s_in) if reverse else (row >= s_in), decay_k, 0.0)
                prods.append(qj[hs] * decay_k)
        diag = _dot(jnp.concatenate(prods, axis=0).astype(BF16), jv)
        if reverse:
            keys, edge = slice((j + 1) * c, tc), (j + 1) * c
        else:
            keys, edge = slice(0, j * c), j * c - 1
        if keys.stop == keys.start:
            return diag, None, keys
        ref_b = b[edge:edge + 1]
        qd = qj * jnp.exp2(bj - ref_b)
        kd = (kk[keys] * jnp.exp2(ref_b - b[keys])).astype(BF16)
        qm = jnp.concatenate([jnp.where(lane_head == h, qd, 0.0) for h in range(HG_HEADS)], axis=0)
        return diag, _dot_nt(qm.astype(BF16), kd), keys

    def output_stage(j, diag, off, keys):
        sl = slice(j * c, (j + 1) * c)
        vj = v[sl]
        o_top, o_bot = o_inter[j * c:j * c + half], o_inter[j * c + half:(j + 1) * c]
        for s in range(c):
            if reach(s, True) != "none":
                o_top = o_top + diag[s * c:s * c + half] * vj[s:s + 1]
            if reach(s, False) != "none":
                o_bot = o_bot + diag[s * c + half:(s + 1) * c] * vj[s:s + 1]
        o = jnp.concatenate([o_top, o_bot], axis=0)
        if off is not None:
            ov = _dot(off.astype(BF16), v_bf[keys])
            for h in range(HG_HEADS):
                o = o + jnp.where(lane_head == h, ov[h * c:(h + 1) * c], 0.0)
        return o

    depth = 2
    staged = [score_stage(j) for j in range(min(depth, n_sub))]
    outs = []
    for j in range(n_sub):
        cur = staged.pop(0)
        if j + depth < n_sub:
            staged.append(score_stage(j + depth))
        outs.append(output_stage(j, *cur))
    return jnp.concatenate(outs, axis=0)


def _hgrn_kernel(hg_ref, lb_ref, gn_ref, s0_ref, tri_ref, jv_ref, o_ref, sfin_ref, st_scr, of_scr, q_scr, *, tc, nc):
    p = pl.program_id(1)
    i = pl.program_id(2)
    r_i = lax.broadcasted_iota(jnp.int32, (HG_W, HG_W), 0) // HG_DV
    c_i = lax.broadcasted_iota(jnp.int32, (HG_W, HG_W), 1) // HG_DK
    bd_mask = r_i == c_i
    jv = jv_ref[...]

    @pl.when(i == 0)
    def _():
        st_scr[...] = s0_ref[0, p]

    @pl.when(p == 0)
    def _():
        hq = hg_ref[0, :, 0:HG_W]
        q = hq * _sigmoid(hq)
        rows = pl.ds(pl.multiple_of(i * tc, tc), tc)
        q_scr[rows, :] = q
        of_scr[rows, :] = _hgrn_direction(hg_ref, q, lb_ref[0:1, :], tri_ref[0], jv, bd_mask, st_scr, HG_W, False, tc)

    @pl.when(p == 1)
    def _():
        rows = pl.ds(pl.multiple_of((nc - 1 - i) * tc, tc), tc)
        o_b = _hgrn_direction(hg_ref, q_scr[rows, :], lb_ref[1:2, :], tri_ref[1], jv, bd_mask, st_scr, 2 * HG_W, True, tc)
        o = of_scr[rows, :] + o_b
        ms = _split_dot(o * o, jv) * (1.0 / HG_DV)
        g = hg_ref[0, :, 4 * HG_W:5 * HG_W]
        o_ref[0] = (o * lax.rsqrt(ms + RMS_EPS) * gn_ref[...] * (g * _sigmoid(g))).astype(o_ref.dtype)

    @pl.when(i == nc - 1)
    def _():
        sfin_ref[0, p] = st_scr[...]


def _hgrn(hg, lb2, gn, s0t, tc=256):
    bsz, t_len, _ = hg.shape
    nc = t_len // tc
    lower = np.arange(tc)[:, None] >= np.arange(tc)[None, :]
    tri = jnp.asarray(np.stack([lower, lower.T]).astype(np.float32))
    chunk = lambda b, p, i: (b, i + p * (nc - 1 - 2 * i), 0)
    o, sfin = pl.pallas_call(
        functools.partial(_hgrn_kernel, tc=tc, nc=nc),
        grid=(bsz, 2, nc),
        in_specs=[
            pl.BlockSpec((1, tc, HG_COLS), chunk),
            pl.BlockSpec((2, HG_W), lambda b, p, i: (0, 0)),
            pl.BlockSpec((1, HG_W), lambda b, p, i: (0, 0)),
            pl.BlockSpec((1, 2, HG_W, HG_W), lambda b, p, i: (b, 0, 0, 0)),
            pl.BlockSpec((2, tc, tc), lambda b, p, i: (0, 0, 0)),
            pl.BlockSpec((HG_W, HG_W), lambda b, p, i: (0, 0)),
        ],
        out_specs=[
            pl.BlockSpec((1, tc, HG_W), lambda b, p, i: (b, nc - 1 - i * p, 0)),
            pl.BlockSpec((1, 2, HG_W, HG_W), lambda b, p, i: (b, 0, 0, 0)),
        ],
        out_shape=[jax.ShapeDtypeStruct((bsz, t_len, HG_W), BF16),
                   jax.ShapeDtypeStruct((bsz, 2, HG_W, HG_W), F32)],
        scratch_shapes=[pltpu.VMEM((HG_W, HG_W), F32), pltpu.VMEM((t_len, HG_W), F32), pltpu.VMEM((t_len, HG_W), F32)],
        compiler_params=_cparams(("parallel", "arbitrary", "arbitrary")),
        name="hgrn",
    )(hg, lb2, jnp.tile(gn, HG_HEADS)[None], s0t, tri, _block_ones(HG_W, HG_DV))
    return o, sfin


def _state_to_blockdiag_t(s0):
    bsz = s0.shape[0]
    s0t = jnp.swapaxes(s0, -1, -2)
    eye = jnp.eye(HG_HEADS, dtype=s0.dtype)
    return jnp.einsum("bdhvk,hg->bdhvgk", s0t, eye).reshape(bsz, 2, HG_W, HG_W)


def _blockdiag_t_to_state(st):
    bsz = st.shape[0]
    s6 = st.reshape(bsz, 2, HG_HEADS, HG_DV, HG_HEADS, HG_DK)
    diag = jnp.stack([s6[:, :, h, :, h, :] for h in range(HG_HEADS)], axis=2)
    return jnp.swapaxes(diag, -1, -2)


def _post_kernel(x_ref, oa_ref, on_ref, od_ref, oh_ref, sh_ref, sc_ref, g1_ref, wg_ref, wb_ref, wo_ref,
                 lg_ref, lb_ref, o_ref, *, alpha):
    x = x_ref[0]
    h = (_ln_plain(x) * (1.0 + sc_ref[0]) + sh_ref[0]).astype(BF16)
    mixed = None
    for n, br_ref in enumerate((oa_ref, on_ref, od_ref, oh_ref)):
        gate = _sigmoid(_dot(h, wg_ref[:, n * D_MODEL:(n + 1) * D_MODEL]))
        term = gate * _dot(br_ref[0], wb_ref[n])
        mixed = term if mixed is None else mixed + term
    y = _dot(mixed.astype(BF16), wo_ref[...])
    o_ref[0] = _ln_plain(alpha * x + g1_ref[0] * y) * lg_ref[...] + lb_ref[...]


def _post_mixer(x, branches, mods, w_gates, w_branch, w_out, ln_g, ln_b, alpha, tm=256):
    bsz, t_len, d = x.shape
    tok = lambda n: pl.BlockSpec((1, tm, n), lambda b, i: (b, i, 0))
    const = lambda shape: pl.BlockSpec(shape, lambda b, i: (0,) * len(shape))
    mod = lambda k: _mod_spec(mods, k, 2)
    return pl.pallas_call(
        functools.partial(_post_kernel, alpha=alpha),
        grid=(bsz, t_len // tm),
        in_specs=[tok(d)] + [tok(BRANCH_W)] * 4 + [mod(MOD_SHIFT1), mod(MOD_SCALE1), mod(MOD_GATE1),
                  const((d, N_BRANCH * d)), const((N_BRANCH, BRANCH_W, d)), const((d, d)),
                  const((1, d)), const((1, d))],
        out_specs=tok(d),
        out_shape=jax.ShapeDtypeStruct((bsz, t_len, d), F32),
        compiler_params=_cparams(("parallel", "parallel")),
        name="post_mixer",
    )(x, *branches, mods, mods, mods, w_gates, w_branch, w_out, ln_g[None], ln_b[None])


ROUTER_LANES = 128
MOE_BT = 1024
MOE_ROWS = 128
MOE_STEPS = N_EXPERTS // 2
MOE_CHUNKS = 4


ROUTER_ROWS = 24


def _router_weights(logits):
    lt = jnp.transpose(logits)[0:ROUTER_ROWS, :]
    row = lax.broadcasted_iota(jnp.int32, lt.shape, 0)
    first_at = lambda mask: jnp.min(jnp.where(mask, row, ROUTER_LANES), axis=0, keepdims=True)
    is_g = row < N_GROUPS
    gl = jnp.where(is_g, lt, NEG_BIG)
    g_max = jnp.max(gl, axis=0, keepdims=True)
    g_idx = first_at(is_g & (gl == g_max))
    g_prob = 1.0 / jnp.sum(jnp.where(is_g, jnp.exp(gl - g_max), 0.0), axis=0, keepdims=True)
    lo = N_GROUPS + g_idx * EXPERTS_PER_GROUP
    in_grp = (row >= lo) & (row < lo + EXPERTS_PER_GROUP)
    el = jnp.where(in_grp, lt, NEG_BIG)
    v1 = jnp.max(el, axis=0, keepdims=True)
    i1 = first_at(in_grp & (el == v1))
    rest = in_grp & (row != i1)
    el2 = jnp.where(rest, lt, NEG_BIG)
    v2 = jnp.max(el2, axis=0, keepdims=True)
    i2 = first_at(rest & (el2 == v2))
    e2 = jnp.exp(v2 - v1)
    w1 = g_prob / (1.0 + e2)
    w2 = g_prob * e2 / (1.0 + e2)
    picked = jnp.where(row == g_idx, 1.0, jnp.where(row == i1, w1, jnp.where(row == i2, w2, 0.0)))
    pad = jnp.zeros((ROUTER_LANES - ROUTER_ROWS, lt.shape[1]), F32)
    return jnp.transpose(jnp.concatenate([picked, pad], axis=0))


def _moe_kernel(x_ref, sh_ref, sc_ref, g2_ref, wr_ref, br_ref, lt_ref, wg_ref, wu_ref, wd_ref, lg_ref, lb_ref, o_ref,
                xs_scr, pt_scr, cws_scr, ys_scr, seg_ref, *, alpha):
    s = pl.program_id(2)
    bt = xs_scr.shape[0]
    rows = MOE_ROWS

    @pl.when(s == 0)
    def _():
        x_parts, logit_parts = [], []
        for r in range(MOE_CHUNKS):
            sl = slice(r * bt // MOE_CHUNKS, (r + 1) * bt // MOE_CHUNKS)
            xm = _ln_plain(x_ref[0, sl, :]) * (1.0 + sc_ref[0]) + sh_ref[0]
            hi = xm.astype(BF16)
            lo = (xm - hi.astype(F32)).astype(BF16)
            x_parts.append(hi)
            logit_parts.append(_dot(hi, wr_ref[0]) + _dot(lo, wr_ref[0]) + _dot(hi, wr_ref[1]) + br_ref[...])
        x_hi = jnp.concatenate(x_parts, axis=0)
        logits = jnp.concatenate(logit_parts, axis=0)
        routed = _router_weights(logits)
        lane = lax.broadcasted_iota(jnp.int32, (bt, ROUTER_LANES), 1)
        onehot = jnp.where(lane < N_GROUPS, routed, 0.0)
        in_g = onehot > 0.0
        cw = routed - onehot
        before = _dot(lt_ref[...], onehot.astype(BF16))
        count = jnp.sum(onehot, axis=0, keepdims=True)
        lane1 = lax.broadcasted_iota(jnp.int32, (1, ROUTER_LANES), 1)
        start = jnp.zeros((1, ROUTER_LANES), F32)
        for k in range(N_GROUPS - 1):
            start = start + jnp.where(lane1 > k, count[:, k:k + 1], 0.0)
        dest = jnp.sum(jnp.where(in_g, start + before, 0.0), axis=-1, keepdims=True)
        dest_row = jnp.transpose(jnp.broadcast_to(dest, (bt, ROUTER_LANES)))[0:1, :]
        pt_scr[...] = jnp.where(dest.astype(jnp.int32) == lax.broadcasted_iota(jnp.int32, (bt, bt), 1),
                                1.0, 0.0).astype(BF16)
        perm = jnp.where(dest_row.astype(jnp.int32) == lax.broadcasted_iota(jnp.int32, (bt, bt), 0),
                         1.0, 0.0).astype(BF16)
        c_hi = cw.astype(BF16)
        c_lo = (cw - c_hi.astype(F32)).astype(BF16)
        moved = _dot(perm, jnp.concatenate([x_hi, c_hi, c_lo], axis=1))
        d_model = x_hi.shape[1]
        xs_scr[...] = moved[:, :d_model].astype(BF16)
        cws_scr[...] = moved[:, d_model:d_model + ROUTER_LANES] + moved[:, d_model + ROUTER_LANES:]
        ys_scr[...] = jnp.zeros_like(ys_scr)
        for g in range(1, N_GROUPS):
            seg_ref[g] = jnp.sum(jnp.where(lane1 == g, start, 0.0)).astype(jnp.int32)

    g = s // (MOE_STEPS // N_GROUPS)
    seg_lo = jnp.where(g == 0, 0, seg_ref[jnp.maximum(g, 1)])
    seg_hi = jnp.where(g == N_GROUPS - 1, bt, seg_ref[jnp.minimum(g + 1, N_GROUPS - 1)])
    t_lo = seg_lo // rows
    t_hi = jnp.where(seg_hi > seg_lo, (seg_hi + rows - 1) // rows, t_lo)
    lane_r = lax.broadcasted_iota(jnp.int32, (rows, ROUTER_LANES), 1)

    def gate_up(t):
        xt = xs_scr[pl.ds(pl.multiple_of(t * rows, rows), rows), :]
        return [(_dot(xt, wg_ref[e]), _dot(xt, wu_ref[e])) for e in range(2)]

    def down(t, gu):
        r0 = pl.multiple_of(t * rows, rows)
        cwt = cws_scr[pl.ds(r0, rows), :]
        hids = []
        for e, (gate, up) in enumerate(gu):
            w_e = jnp.sum(jnp.where(lane_r == N_GROUPS + 2 * s + e, cwt, 0.0), axis=-1, keepdims=True)
            hids.append((gate * _sigmoid(gate) * up * w_e).astype(BF16))
        ys_scr[pl.ds(r0, rows), :] += _dot(hids[0], wd_ref[0]) + _dot(hids[1], wd_ref[1])

    def tile_pair(i, carry):
        t = t_lo + 2 * i
        gu_a, gu_b = gate_up(t), gate_up(t + 1)
        down(t, gu_a)
        down(t + 1, gu_b)
        return carry

    n_tiles = t_hi - t_lo
    lax.fori_loop(0, n_tiles // 2, tile_pair, 0)

    @pl.when(n_tiles % 2 == 1)
    def _():
        down(t_hi - 1, gate_up(t_hi - 1))

    @pl.when(s == MOE_STEPS - 1)
    def _():
        ys = ys_scr[...]
        hi = ys.astype(BF16)
        lo = (ys - hi.astype(F32)).astype(BF16)
        rows_of = lambda r: slice(r * bt // MOE_CHUNKS, (r + 1) * bt // MOE_CHUNKS)
        back = lambda r: _dot(pt_scr[rows_of(r), :], hi) + _dot(pt_scr[rows_of(r), :], lo)
        y_next = back(0)
        for r in range(MOE_CHUNKS):
            y, y_next = y_next, (back(r + 1) if r + 1 < MOE_CHUNKS else None)
            sl = rows_of(r)
            o_ref[0, sl, :] = _ln_plain(alpha * x_ref[0, sl, :] + g2_ref[0] * y) * lg_ref[...] + lb_ref[...]


def _moe(x, mods, w_route, b_route, w_gate, w_up, w_down, ln_g, ln_b, alpha):
    bsz, t_len, d = x.shape
    bt = min(MOE_BT, t_len)
    assert t_len % bt == 0 and bt % MOE_ROWS == 0 and bt % (8 * MOE_CHUNKS) == 0
    tok = pl.BlockSpec((1, bt, d), lambda b, i, e: (b, i, 0))
    const = lambda shape: pl.BlockSpec(shape, lambda b, i, e: (0,) * len(shape))
    mod = lambda k: _mod_spec(mods, k, 3)
    pair = lambda shape: pl.BlockSpec((2,) + shape, lambda b, i, e: (e, 0, 0))
    lower = jnp.asarray(np.tril(np.ones((bt, bt), np.float32), -1), dtype=BF16)
    return pl.pallas_call(
        functools.partial(_moe_kernel, alpha=alpha),
        grid=(bsz, t_len // bt, MOE_STEPS),
        in_specs=[tok, mod(MOD_SHIFT2), mod(MOD_SCALE2), mod(MOD_GATE2), const((2, d, ROUTER_LANES)), const((1, ROUTER_LANES)), const((bt, bt)),
                  pair((d, D_EXPERT)), pair((d, D_EXPERT)), pair((D_EXPERT, d)), const((1, d)), const((1, d))],
        out_specs=tok,
        out_shape=jax.ShapeDtypeStruct((bsz, t_len, d), F32),
        scratch_shapes=[pltpu.VMEM((bt, d), BF16), pltpu.VMEM((bt, bt), BF16), pltpu.VMEM((bt, ROUTER_LANES), F32),
                        pltpu.VMEM((bt, d), F32), pltpu.SMEM((N_GROUPS,), jnp.int32)],
        compiler_params=_cparams(("parallel", "parallel", "arbitrary")),
        name="moe",
    )(x, mods, mods, mods, w_route, b_route, lower, w_gate, w_up, w_down, ln_g[None], ln_b[None])


def _layer(x, mods, lp, ctx, alpha):
    bsz, t_len, _ = x.shape
    latent = ctx is not None
    cached = (ctx["gk_t"], ctx["gv_ext"], ctx["dk_t"], ctx["dv"]) if latent else ()
    outs = _inproj(x, mods, lp["w_heads"], lp["gqa_qn"], lp["gqa_kn"], latent, cached)
    qa, kat, va, qn, kn, vn, qd, kdt, vd, hg = outs[:10]
    if latent:
        s0t = _state_to_blockdiag_t(ctx["state"].astype(F32))
        tq = 256
    else:
        s0t = jnp.zeros((bsz, 2, HG_W, HG_W), F32)
        tq = t_len
    o_a = _attention(qa, kat, va, GQA_Q_HEADS, GQA_KV_HEADS, tq)
    if latent:
        o_n = _na_attention(qn, kn, vn, ctx["nk_ext"], ctx["nv_ext"], lp["na_bias"])
    else:
        o_n = _attention(qn, kn, vn, NA_HEADS, NA_HEADS, tq)
    o_d = _diff_attention(qd, kdt, vd, lp["diff_par"], tq)
    o_h, sfin = _hgrn(hg, lp["hg_lb"], lp["hg_norm"], s0t)
    x1 = _post_mixer(x, (o_a, o_n, o_d, o_h), mods, lp["w_gates"], lp["w_branch"], lp["w_out"],
                     lp["ln_mix_g"], lp["ln_mix_b"], alpha)
    x1r = x1 if latent else x1.reshape(1, bsz * t_len, D_MODEL)
    y = _moe(x1r, mods, lp["w_route"], lp["b_route"], lp["w_gate"], lp["w_up"], lp["w_down"],
             lp["ln_ffn_g"], lp["ln_ffn_b"], alpha).reshape(bsz, t_len, D_MODEL)
    caches = None
    if not latent:
        ka, va32, kn32, vn32, kd32, vd32 = outs[10:]
        caches = (ka.reshape(bsz, t_len, GQA_KV_HEADS, HEAD_DIM), va32.reshape(bsz, t_len, GQA_KV_HEADS, HEAD_DIM),
                  kn32.reshape(bsz, t_len, NA_HEADS, HEAD_DIM), vn32.reshape(bsz, t_len, NA_HEADS, HEAD_DIM),
                  kd32.reshape(bsz, t_len, DIFF_HEADS, 2, DIFF_QK_DIM), vd32.reshape(bsz, t_len, DIFF_HEADS, DIFF_V_DIM),
                  _blockdiag_t_to_state(sfin))
    return y, caches


def _padded_heads(a, fill):
    a = a.astype(BF16)
    return jnp.concatenate([a, jnp.full(a.shape, fill, BF16)], axis=-1).reshape(a.shape[:3] + (-1,))


def _keys_t(a):
    a = a.astype(BF16).reshape(a.shape[:3] + (-1,))
    return jnp.swapaxes(a, 2, 3)


def kernel(x_prompt, x_sample, cache_gqa_k, cache_gqa_v, cache_na_k, cache_na_v, cache_diff_k, cache_diff_v, state_hgrn, c, c_ctx, w_ada, b_ada, w_in, gqa_q_norm, gqa_k_norm, na_rpb, diff_lambda, diff_subln, hgrn_lb, hgrn_norm, w_branch, w_out, ln_mix_g, ln_mix_b, ln_ffn_g, ln_ffn_b, w_group, b_group, w_router, b_router, w_gate, w_up, w_down):
    depth = w_in.shape[0]
    d = D_MODEL
    alpha = (2 * depth) ** 0.25
    dec_b = c.shape[0]

    lb = jax.nn.softmax(hgrn_lb.astype(F32), axis=1)
    lb = jnp.cumsum(lb, axis=1) - lb[:, :1]

    cond_rows = -(-(dec_b + 1) // 8) * 8
    cond = jnp.zeros((cond_rows, d), F32).at[:dec_b].set(c).at[dec_b].set(c_ctx)
    mods_all = _ada_mods(cond, w_ada, b_ada)

    na_bias = _na_bias_table(na_rpb)
    layers = []
    for l in range(depth):
        lam_init = 0.8 - 0.6 * math.exp(-0.3 * l)
        lp_lam = diff_lambda[l].astype(F32)
        lam = jnp.exp(jnp.sum(lp_lam[0] * lp_lam[1])) - jnp.exp(jnp.sum(lp_lam[2] * lp_lam[3])) + lam_init
        diff_par = jnp.zeros((8, DIFF_V_DIM), F32).at[0].set(lam).at[1].set(diff_subln[l] * (1.0 - lam_init))
        w_route = jnp.zeros((d, ROUTER_LANES), F32)
        w_route = w_route.at[:, :N_GROUPS].set(w_group[l]).at[:, N_GROUPS:N_GROUPS + N_EXPERTS].set(
            w_router[l].reshape(d, N_EXPERTS))
        w_route_hi = w_route.astype(BF16)
        b_route = jnp.zeros((1, ROUTER_LANES), F32)
        b_route = b_route.at[0, :N_GROUPS].set(b_group[l]).at[0, N_GROUPS:N_GROUPS + N_EXPERTS].set(
            b_router[l].reshape(N_EXPERTS))
        layers.append({
            "w_heads": w_in[l, :, :D_HEADS_IN].astype(BF16),
            "w_gates": w_in[l, :, D_HEADS_IN:].astype(BF16),
            "gqa_qn": gqa_q_norm[l], "gqa_kn": gqa_k_norm[l],
            "na_bias": na_bias[l],
            "diff_par": diff_par,
            "hg_lb": lb[:, l], "hg_norm": hgrn_norm[l],
            "w_branch": w_branch[l].astype(BF16), "w_out": w_out[l].astype(BF16),
            "ln_mix_g": ln_mix_g[l], "ln_mix_b": ln_mix_b[l], "ln_ffn_g": ln_ffn_g[l], "ln_ffn_b": ln_ffn_b[l],
            "w_route": jnp.stack([w_route_hi, (w_route - w_route_hi.astype(F32)).astype(BF16)]), "b_route": b_route,
            "w_gate": w_gate[l].reshape(N_EXPERTS, d, D_EXPERT).astype(BF16),
            "w_up": w_up[l].reshape(N_EXPERTS, d, D_EXPERT).astype(BF16),
            "w_down": w_down[l].reshape(N_EXPERTS, D_EXPERT, d).astype(BF16),
        })

    xp = x_prompt
    ctx_out = []
    for l in range(depth):
        xp, caches = _layer(xp, mods_all[l, dec_b:dec_b + 1, None, :], layers[l], None, alpha)
        ctx_out.append(caches)
    stack = lambda i: jnp.stack([t[i] for t in ctx_out], axis=1)

    cached = {"gk_t": _keys_t(cache_gqa_k), "gv_ext": _padded_heads(cache_gqa_v, 1.0),
              "nk_ext": _padded_heads(cache_na_k, 0.0), "nv_ext": _padded_heads(cache_na_v, 1.0),
              "dk_t": _keys_t(cache_diff_k), "dv": _padded_heads(cache_diff_v, 1.0), "state": state_hgrn}
    xs = x_sample
    for l in range(depth):
        ctx = {name: t[:, l] for name, t in cached.items()}
        xs, _ = _layer(xs, mods_all[l, :dec_b, None, :], layers[l], ctx, alpha)

    return (xp, xs, stack(0), stack(1), stack(2), stack(3), stack(4), stack(5), stack(6))
```
